```python
import math
import jax
import jax.numpy as jnp
from jax import lax
import numpy as np

D_MODEL = 2048
BATCH = 4
SEQ = 8192
DEPTH = 4

N_EVEN = (DEPTH + 1) // 2
N_ODD = DEPTH // 2
BRANCH = D_MODEL // 2
RMS_EPS = 1e-6
NEG = -1e30
GATE_FLOOR = 1e-20
Q_BLOCK = 128

HG_HEADS = 8
HG_DK = BRANCH // HG_HEADS
HG_DV = BRANCH // HG_HEADS
HG_CHUNK = 64

DF_HEADS = 8
DF_DH = BRANCH // (2 * DF_HEADS)
DF_DV = 2 * DF_DH

GD_HEADS = 8
GD_DK = BRANCH // GD_HEADS
GD_DV = BRANCH // GD_HEADS
GD_CONV = 4
GD_CHUNK = 64

SB_HEADS = 8
SB_DH = BRANCH // SB_HEADS

EVEN_IN = 8 * BRANCH
ODD_SIZES = (BRANCH, BRANCH, BRANCH, BRANCH, GD_HEADS, GD_HEADS, BRANCH, BRANCH, BRANCH, BRANCH)
ODD_IN = sum(ODD_SIZES)

kernel_name = 'hybrid_hgrn2_diffattn_gdn_stickbreak'


def rmsnorm(x, g):
    xf = x.astype(jnp.float32)
    y = xf * lax.rsqrt(jnp.mean(xf * xf, axis=-1, keepdims=True) + RMS_EPS)
    return (y * g.astype(jnp.float32)).astype(x.dtype)


def l2norm(x):
    return x * lax.rsqrt(jnp.sum(x * x, axis=-1, keepdims=True) + RMS_EPS)


def to_heads(t, n_heads):
    b, l, _ = t.shape
    return t.reshape(b, l, n_heads, -1).transpose(0, 2, 1, 3)


def from_heads(t):
    b, h, l, d = t.shape
    return t.transpose(0, 2, 1, 3).reshape(b, l, h * d)


def to_chunks(t, c):
    b, h, l, d = t.shape
    return jnp.moveaxis(t.reshape(b, h, l // c, c, d), 2, 0)


def from_chunks(t):
    n, b, h, c, d = t.shape
    return jnp.moveaxis(t, 0, 2).reshape(b, h, n * c, d)


def causal_conv(x, w):
    return lax.conv_general_dilated(
        x, w[:, None, :], window_strides=(1,), padding=[(GD_CONV - 1, 0)],
        dimension_numbers=('NWC', 'WIO', 'NWC'), feature_group_count=x.shape[-1])


def hgrn2_recurrence(q, k, log_f, v):
    b, h, l, dk = q.shape
    dv = v.shape[-1]
    causal = jnp.tril(jnp.ones((HG_CHUNK, HG_CHUNK), dtype=bool))[:, :, None]

    def step(state, xs):
        qc, kc, gc, vc = xs
        cum = jnp.cumsum(gc, axis=-2)
        rel = cum[:, :, :, None, :] - cum[:, :, None, :, :]
        decay = jnp.exp(jnp.where(causal, rel, NEG))
        scores = jnp.einsum('bhtd,bhsd,bhtsd->bhts', qc, kc, decay)
        out = (jnp.einsum('bhts,bhsv->bhtv', scores, vc)
               + jnp.einsum('bhtd,bhdv->bhtv', qc * jnp.exp(cum), state))
        last = cum[:, :, -1:, :]
        state = (jnp.exp(last[:, :, 0, :])[..., None] * state
                 + jnp.einsum('bhsd,bhsv->bhdv', kc * jnp.exp(last - cum), vc))
        return state, out

    s0 = jnp.zeros((b, h, dk, dv), jnp.float32)
    xs = (to_chunks(q, HG_CHUNK), to_chunks(k, HG_CHUNK), to_chunks(log_f, HG_CHUNK), to_chunks(v, HG_CHUNK))
    _, out = lax.scan(step, s0, xs)
    return from_chunks(out)


def diff_attention(q_raw, k_raw, v_raw, lam_vec, lam_init):
    b, l, _ = q_raw.shape
    nb = l // Q_BLOCK
    q = q_raw.reshape(b, l, DF_HEADS, 2, DF_DH).transpose(0, 2, 3, 1, 4)
    k = k_raw.reshape(b, l, DF_HEADS, 2, DF_DH).transpose(0, 2, 3, 1, 4)
    v = to_heads(v_raw, DF_HEADS)
    lam = (jnp.exp(jnp.sum(lam_vec[0] * lam_vec[1])) - jnp.exp(jnp.sum(lam_vec[2] * lam_vec[3]))
           + lam_init)
    qb = jnp.moveaxis(q.reshape(b, DF_HEADS, 2, nb, Q_BLOCK, DF_DH), 3, 0)
    key_pos = jnp.arange(l)
    scale = DF_DH ** -0.5

    def block(args):
        q_blk, bi = args
        s = jnp.einsum('bhcqd,bhckd->bhcqk', q_blk, k) * scale
        q_pos = bi * Q_BLOCK + jnp.arange(Q_BLOCK)
        mask = key_pos[None, :] <= q_pos[:, None]
        p = jax.nn.softmax(jnp.where(mask, s, NEG), axis=-1)
        wts = p[:, :, 0] - lam * p[:, :, 1]
        return jnp.einsum('bhqk,bhkv->bhqv', wts, v)

    out = lax.map(block, (qb, jnp.arange(nb)))
    return from_chunks(out)


def gated_delta_rule(q, k, v, beta, g):
    b, h, l, dk = q.shape
    dv = v.shape[-1]
    c = GD_CHUNK
    n = l // c
    qc = q.reshape(b, h, n, c, dk)
    kc = k.reshape(b, h, n, c, dk)
    vc = v.reshape(b, h, n, c, dv)
    bc = beta.reshape(b, h, n, c)[..., None]
    gc = jnp.cumsum(g.reshape(b, h, n, c), axis=-1)
    incl = jnp.tril(jnp.ones((c, c), dtype=bool))
    strict = jnp.tril(jnp.ones((c, c), dtype=bool), -1)
    decay = jnp.exp(jnp.where(incl, gc[..., :, None] - gc[..., None, :], NEG))
    kb = kc * bc
    lower = jnp.where(strict, jnp.einsum('bhnid,bhnjd->bhnij', kb, kc) * decay, 0.0)
    eye = jnp.eye(c, dtype=q.dtype)
    tinv = lax.linalg.triangular_solve(eye + lower, jnp.broadcast_to(eye, lower.shape),
                                       left_side=True, lower=True)
    u = tinv @ (vc * bc)
    w = tinv @ (kb * jnp.exp(gc)[..., None])
    attn = jnp.einsum('bhnid,bhnjd->bhnij', qc, kc) * decay
    qg = qc * jnp.exp(gc)[..., None]
    kdec = kc * jnp.exp(gc[..., -1:] - gc)[..., None]
    glast = jnp.exp(gc[..., -1])[..., None, None]

    def step(state, xs):
        u_n, w_n, a_n, qg_n, kd_n, gl_n = xs
        v_new = u_n - w_n @ state
        out = qg_n @ state + a_n @ v_new
        state = gl_n * state + jnp.swapaxes(kd_n, -1, -2) @ v_new
        return state, out

    xs = tuple(jnp.moveaxis(t, 2, 0) for t in (u, w, attn, qg, kdec, glast))
    s0 = jnp.zeros((b, h, dk, dv), jnp.float32)
    _, out = lax.scan(step, s0, xs)
    return from_chunks(out)


def stick_breaking(q_raw, k_raw, v_raw):
    b, l, _ = q_raw.shape
    nb = l // Q_BLOCK
    q = to_heads(q_raw, SB_HEADS)
    k = to_heads(k_raw, SB_HEADS)
    v = to_heads(v_raw, SB_HEADS)
    qb = to_chunks(q, Q_BLOCK)
    key_pos = jnp.arange(l)
    scale = SB_DH ** -0.5

    def block(args):
        q_blk, bi = args
        z = jnp.einsum('bhqd,bhkd->bhqk', q_blk, k) * scale
        q_pos = bi * Q_BLOCK + jnp.arange(Q_BLOCK)
        earlier = key_pos[None, :] < q_pos[:, None]
        log_rest = jnp.where(earlier, jax.nn.log_sigmoid(-z), 0.0)
        after = lax.cumsum(log_rest, axis=3, reverse=True) - log_rest
        wts = jnp.where(earlier, jnp.exp(jax.nn.log_sigmoid(z) + after), 0.0)
        return jnp.einsum('bhqk,bhkv->bhqv', wts, v)

    return from_chunks(lax.map(block, (qb, jnp.arange(nb))))


def even_layer(h, w_in, w_out, lb, hg_gain, lam_vec, df_gain, layer_idx):
    f32 = jnp.float32
    proj = jnp.einsum('bld,de->ble', h, w_in).astype(f32)
    hq, hf, hi, hz, dq, dk, dv, dz = jnp.split(proj, 8, axis=-1)
    lb_h = lb.astype(f32).reshape(HG_HEADS, 1, HG_DK)
    fp = to_heads(hf, HG_HEADS)
    sig = jax.nn.sigmoid(fp)
    log_f = jnp.log(jnp.maximum(lb_h + (1.0 - lb_h) * sig, GATE_FLOOR))
    k_in = (1.0 - lb_h) * (1.0 - sig)
    o_a = hgrn2_recurrence(to_heads(jax.nn.silu(hq), HG_HEADS), k_in, log_f, to_heads(hi, HG_HEADS))
    o_a = from_heads(rmsnorm(o_a, hg_gain)) * jax.nn.silu(hz)
    lam_init = 0.8 - 0.6 * math.exp(-0.3 * layer_idx)
    o_b = diff_attention(dq, dk, dv, lam_vec.astype(f32), lam_init)
    o_b = from_heads(rmsnorm(o_b, df_gain)) * (1.0 - lam_init) * jax.nn.silu(dz)
    mixed = jnp.concatenate([o_a, o_b], axis=-1).astype(h.dtype)
    return jnp.einsum('ble,ed->bld', mixed, w_out)


def odd_layer(h, w_in, w_out, conv_w, a_log, dt_bias, gd_gain):
    f32 = jnp.float32
    proj = jnp.einsum('bld,de->ble', h, w_in).astype(f32)
    splits = [int(s) for s in np.cumsum(ODD_SIZES)[:-1]]
    cq, ck, cv, cz, ca, cb, sq, sk, sv, sz = jnp.split(proj, splits, axis=-1)
    qkv = jax.nn.silu(causal_conv(jnp.concatenate([cq, ck, cv], axis=-1), conv_w.astype(f32)))
    cq, ck, cv = jnp.split(qkv, 3, axis=-1)
    q = l2norm(to_heads(cq, GD_HEADS)) * (GD_DK ** -0.5)
    k = l2norm(to_heads(ck, GD_HEADS))
    v = to_heads(cv, GD_HEADS)
    beta = jax.nn.sigmoid(cb).transpose(0, 2, 1)
    g = (-jnp.exp(a_log.astype(f32)) * jax.nn.softplus(ca + dt_bias.astype(f32))).transpose(0, 2, 1)
    o_c = gated_delta_rule(q, k, v, beta, g)
    o_c = from_heads(rmsnorm(o_c, gd_gain)) * jax.nn.silu(cz)
    o_d = from_heads(stick_breaking(sq, sk, sv)) * jax.nn.silu(sz)
    mixed = jnp.concatenate([o_c, o_d], axis=-1).astype(h.dtype)
    return jnp.einsum('ble,ed->bld', mixed, w_out)


def setup_inputs(seed: int = 0) -> dict:
    key = jax.random.key(seed)
    ks = jax.random.split(key, 16)
    f32 = jnp.float32

    def gain(k, shape):
        return 1.0 + 0.02 * jax.random.normal(k, shape, f32)

    x = jax.random.normal(ks[0], (BATCH, SEQ, D_MODEL), f32)
    norm_pre = gain(ks[1], (DEPTH, D_MODEL))
    norm_post = gain(ks[2], (DEPTH, D_MODEL))
    ev_w_in = jax.random.normal(ks[3], (N_EVEN, D_MODEL, EVEN_IN), f32) * D_MODEL ** -0.5
    ev_w_out = jax.random.normal(ks[4], (N_EVEN, 2 * BRANCH, D_MODEL), f32) * (2 * BRANCH) ** -0.5
    hg_lb_logits = 0.5 * jax.random.normal(ks[5], (N_EVEN, BRANCH), f32)
    hg_norm = gain(ks[6], (N_EVEN, HG_DV))
    df_lambda = 0.1 * jax.random.normal(ks[7], (N_EVEN, 4, DF_DH), f32)
    df_norm = gain(ks[8], (N_EVEN, DF_DV))
    od_w_in = jax.random.normal(ks[9], (N_ODD, D_MODEL, ODD_IN), f32) * D_MODEL ** -0.5
    od_w_out = jax.random.normal(ks[10], (N_ODD, 2 * BRANCH, D_MODEL), f32) * (2 * BRANCH) ** -0.5
    gd_conv = jax.random.normal(ks[11], (N_ODD, GD_CONV, 3 * BRANCH), f32) * GD_CONV ** -0.5
    gd_a_log = jnp.log(jax.random.uniform(ks[12], (N_ODD, GD_HEADS), f32, 1.0, 16.0))
    dt = jnp.exp(jax.random.uniform(ks[13], (N_ODD, GD_HEADS), f32, math.log(1e-3), math.log(1e-1)))
    gd_dt_bias = dt + jnp.log(-jnp.expm1(-dt))
    gd_norm = gain(ks[14], (N_ODD, GD_DV))
    return {'x': x, 'norm_pre': norm_pre, 'norm_post': norm_post,
            'ev_w_in': ev_w_in, 'ev_w_out': ev_w_out, 'hg_lb_logits': hg_lb_logits,
            'hg_norm': hg_norm, 'df_lambda': df_lambda, 'df_norm': df_norm,
            'od_w_in': od_w_in, 'od_w_out': od_w_out, 'gd_conv': gd_conv,
            'gd_a_log': gd_a_log, 'gd_dt_bias': gd_dt_bias, 'gd_norm': gd_norm}


def reference(x, norm_pre, norm_post, ev_w_in, ev_w_out, hg_lb_logits, hg_norm, df_lambda, df_norm,
              od_w_in, od_w_out, gd_conv, gd_a_log, gd_dt_bias, gd_norm):
    p = jax.nn.softmax(hg_lb_logits.astype(jnp.float32), axis=0)
    lower_bounds = jnp.cumsum(p, axis=0) - p[0:1]
    for layer in range(DEPTH):
        h = rmsnorm(x, norm_pre[layer])
        j = layer // 2
        if layer % 2 == 0:
            y = even_layer(h, ev_w_in[j], ev_w_out[j], lower_bounds[j], hg_norm[j],
                           df_lambda[j], df_norm[j], layer)
        else:
            y = odd_layer(h, od_w_in[j], od_w_out[j], gd_conv[j], gd_a_log[j],
                          gd_dt_bias[j], gd_norm[j])
        x = x + rmsnorm(y, norm_post[layer])
    return x
```

```python
import functools
import math

import numpy as np
import jax
import jax.numpy as jnp
from jax import lax
from jax.experimental import pallas as pl
from jax.experimental.pallas import tpu as pltpu

F32 = jnp.float32
BF16 = jnp.bfloat16
HIGHEST = lax.Precision.HIGHEST

D_MODEL = 2048
BRANCH = D_MODEL // 2
N_HEADS = 8
HEAD = BRANCH // N_HEADS
DF_DH = HEAD // 2
GD_CONV = 4
RMS_EPS = 1e-6
NEG = -1e30
GATE_FLOOR = 1e-20
SB_SKIP = 104.0

V7X_VMEM_LIMIT = 56 * 1024 * 1024

CHUNK = 64
REC_BLOCK = 256
ATT_BLOCK = 256
INV_BLOCK = 16


def _params(*sem):
    return pltpu.CompilerParams(dimension_semantics=sem, vmem_limit_bytes=V7X_VMEM_LIMIT)


def _sigmoid(x):
    return 1.0 / (1.0 + jnp.exp(-x))


def _silu(x):
    return x * _sigmoid(x)


def _softplus(x):
    return jnp.maximum(x, 0.0) + jnp.log(1.0 + jnp.exp(-jnp.abs(x)))


def _dot(a, b):
    return jnp.dot(a, b, preferred_element_type=F32)


def _dot_nt(a, b):
    return lax.dot_general(a, b, (((1,), (1,)), ((), ())), preferred_element_type=F32)


def _dot_exact(a, b):
    return jnp.dot(a, b, precision=HIGHEST, preferred_element_type=F32)


def _inproj_kernel(x_ref, g_ref, w_ref, o_ref, h_scr):
    @pl.when(pl.program_id(1) == 0)
    def _():
        x = x_ref[...]
        ms = jnp.mean(x * x, axis=-1, keepdims=True)
        h_scr[...] = (x * lax.rsqrt(ms + RMS_EPS) * g_ref[...]).astype(BF16)

    o_ref[...] = _dot(h_scr[...], w_ref[...]).astype(o_ref.dtype)


def _inproj_ab_kernel(x_ref, g_ref, w_ref, wab_ref, o_ref, oab_ref, h_scr):
    @pl.when(pl.program_id(1) == 0)
    def _():
        x = x_ref[...]
        ms = jnp.mean(x * x, axis=-1, keepdims=True)
        h_scr[...] = (x * lax.rsqrt(ms + RMS_EPS) * g_ref[...]).astype(BF16)
        oab_ref[...] = _dot(h_scr[...], wab_ref[...])

    o_ref[...] = _dot(h_scr[...], w_ref[...]).astype(o_ref.dtype)


def _inproj(x2, gain, w, w_ab=None, *, tm, tn):
    m, d = x2.shape
    n = w.shape[1]
    grid = (m // tm, n // tn)
    x_spec = pl.BlockSpec((tm, d), lambda i, j: (i, 0))
    g_spec = pl.BlockSpec((1, d), lambda i, j: (0, 0))
    w_spec = pl.BlockSpec((d, tn), lambda i, j: (0, j))
    o_spec = pl.BlockSpec((tm, tn), lambda i, j: (i, j))
    scratch = [pltpu.VMEM((tm, d), BF16)]
    if w_ab is None:
        return pl.pallas_call(
            _inproj_kernel, grid=grid, in_specs=[x_spec, g_spec, w_spec], out_specs=o_spec,
            out_shape=jax.ShapeDtypeStruct((m, n), BF16), scratch_shapes=scratch,
            compiler_params=_params("parallel", "arbitrary"), name="inproj",
        )(x2, gain, w)
    nab = w_ab.shape[1]
    return pl.pallas_call(
        _inproj_ab_kernel, grid=grid,
        in_specs=[x_spec, g_spec, w_spec, pl.BlockSpec((d, nab), lambda i, j: (0, 0))],
        out_specs=[o_spec, pl.BlockSpec((tm, nab), lambda i, j: (i, 0))],
        out_shape=[jax.ShapeDtypeStruct((m, n), BF16), jax.ShapeDtypeStruct((m, nab), F32)],
        scratch_shapes=scratch, compiler_params=_params("parallel", "arbitrary"), name="inproj_ab",
    )(x2, gain, w, w_ab)


def _outproj_kernel(ma_ref, mb_ref, wa_ref, wb_ref, g_ref, x_ref, o_ref):
    y = _dot(ma_ref[...], wa_ref[...]) + _dot(mb_ref[...], wb_ref[...])
    ms = jnp.mean(y * y, axis=-1, keepdims=True)
    o_ref[...] = x_ref[...] + y * lax.rsqrt(ms + RMS_EPS) * g_ref[...]


def _outproj(mix_a, mix_b, w_a, w_b, gain, x2, *, tm):
    m, d = x2.shape
    k = mix_a.shape[1]
    return pl.pallas_call(
        _outproj_kernel, grid=(m // tm,),
        in_specs=[pl.BlockSpec((tm, k), lambda i: (i, 0)), pl.BlockSpec((tm, k), lambda i: (i, 0)),
                  pl.BlockSpec((k, d), lambda i: (0, 0)), pl.BlockSpec((k, d), lambda i: (0, 0)),
                  pl.BlockSpec((1, d), lambda i: (0, 0)), pl.BlockSpec((tm, d), lambda i: (i, 0))],
        out_specs=pl.BlockSpec((tm, d), lambda i: (i, 0)),
        out_shape=jax.ShapeDtypeStruct((m, d), F32),
        compiler_params=_params("parallel"), name="outproj",
    )(mix_a, mix_b, w_a, w_b, gain, x2)


def _hgrn_constants(c):
    t = np.arange(c)
    tri = (t[None, :] <= t[:, None]).astype(np.float32)
    ops = [tri]
    masks = [np.eye(c, dtype=np.float32)]
    b = 1
    while b < c:
        mid = (t // (2 * b)) * (2 * b) + b
        ops.append(tri - (t[None, :] <= mid[:, None]).astype(np.float32))
        same = (t[:, None] // (2 * b)) == (t[None, :] // (2 * b))
        masks.append((same & (t[:, None] >= mid[:, None]) & (t[None, :] < mid[:, None])).astype(np.float32))
        b *= 2
    return np.concatenate(ops, axis=0), np.stack(masks, axis=0)


def _hgrn_kernel(q_ref, f_ref, i_ref, z_ref, lbl_ref, gain_ref, ops_ref, mask_ref, o_ref, st_scr,
                 *, layer_j, n_chunks, c):
    @pl.when(pl.program_id(2) == 0)
    def _():
        st_scr[...] = jnp.zeros_like(st_scr)

    logits = lbl_ref[...]
    e = jnp.exp(logits - jnp.max(logits, axis=0, keepdims=True))
    p = e / jnp.sum(e, axis=0, keepdims=True)
    lb = jnp.sum(p[0:layer_j + 1], axis=0, keepdims=True) - p[0:1]

    n_levels = mask_ref.shape[0] - 1
    gain = gain_ref[...]
    st = st_scr[...]
    for ci in range(n_chunks):
        rows = pl.ds(ci * c, c)
        q = _silu(q_ref[0, rows, :].astype(F32))
        sig = _sigmoid(f_ref[0, rows, :].astype(F32))
        g = jnp.log(jnp.maximum(lb + (1.0 - lb) * sig, GATE_FLOOR))
        k = (1.0 - lb) * (1.0 - sig)
        v = i_ref[0, rows, :].astype(F32)
        vb = v.astype(BF16)

        expo = _dot_exact(ops_ref[...], g)
        a = expo[0:c]
        scores = mask_ref[0] * _dot_nt(q.astype(BF16), k.astype(BF16))
        for lv in range(n_levels):
            ex = expo[(lv + 1) * c:(lv + 2) * c]
            qd = (q * jnp.exp(jnp.minimum(ex, 0.0))).astype(BF16)
            kd = (k * jnp.exp(jnp.minimum(-ex, 0.0))).astype(BF16)
            scores = scores + mask_ref[lv + 1] * _dot_nt(qd, kd)
        o = _dot(scores.astype(BF16), vb)
        o = o + _dot_nt((q * jnp.exp(a)).astype(BF16), st.astype(BF16))
        a_last = a[c - 1:c, :]
        k_end = (k * jnp.exp(a_last - a)).astype(BF16)
        st = jnp.exp(a_last) * st + lax.dot_general(vb, k_end, (((0,), (0,)), ((), ())),
                                                    preferred_element_type=F32)

        ms = jnp.mean(o * o, axis=-1, keepdims=True)
        y = o * lax.rsqrt(ms + RMS_EPS) * gain
        o_ref[0, rows, :] = (y * _silu(z_ref[0, rows, :].astype(F32))).astype(o_ref.dtype)
    st_scr[...] = st


def _hgrn(proj, lb_logits, gain, layer_j, *, t_blk, c):
    b, l, _ = proj.shape
    ops, masks = _hgrn_constants(c)
    n_even = lb_logits.shape[0]

    def col(seg):
        return pl.BlockSpec((1, t_blk, HEAD), lambda bi, h, t, seg=seg: (bi, t, seg * N_HEADS + h))

    kern = functools.partial(_hgrn_kernel, layer_j=layer_j, n_chunks=t_blk // c, c=c)
    return pl.pallas_call(
        kern, grid=(b, N_HEADS, l // t_blk),
        in_specs=[col(0), col(1), col(2), col(3),
                  pl.BlockSpec((n_even, HEAD), lambda bi, h, t: (0, h)),
                  pl.BlockSpec((1, HEAD), lambda bi, h, t: (0, 0)),
                  pl.BlockSpec(ops.shape, lambda bi, h, t: (0, 0)),
                  pl.BlockSpec(masks.shape, lambda bi, h, t: (0, 0, 0))],
        out_specs=pl.BlockSpec((1, t_blk, HEAD), lambda bi, h, t: (bi, t, h)),
        out_shape=jax.ShapeDtypeStruct((b, l, BRANCH), BF16),
        scratch_shapes=[pltpu.VMEM((HEAD, HEAD), F32)],
        compiler_params=_params("parallel", "parallel", "arbitrary"), name="hgrn2",
    )(proj, proj, proj, proj, lb_logits, gain, jnp.asarray(ops), jnp.asarray(masks))


def _diff_kernel(q_ref, k_ref, v_ref, z_ref, lamv_ref, gain_ref, o_ref, *, lam_init, tq):
    qi = pl.program_id(2)
    lv = lamv_ref[...]
    d01 = jnp.sum(jnp.sum(lv[0:1] * lv[1:2], axis=-1, keepdims=True), axis=0, keepdims=True)
    d23 = jnp.sum(jnp.sum(lv[2:3] * lv[3:4], axis=-1, keepdims=True), axis=0, keepdims=True)
    lam = jnp.exp(d01) - jnp.exp(d23) + lam_init

    q = q_ref[0].astype(F32) * (DF_DH ** -0.5)
    lane = lax.broadcasted_iota(jnp.int32, q.shape, 1)
    zero = jnp.zeros_like(q)
    qq = jnp.concatenate([jnp.where(lane < DF_DH, q, zero), jnp.where(lane >= DF_DH, q, zero)],
                         axis=0).astype(BF16)

    def step(kj, carry, masked):
        m, l, acc = carry
        rows = pl.ds(pl.multiple_of(kj * tq, tq), tq)
        s = _dot_nt(qq, k_ref[0, rows, :])
        if masked:
            r = lax.broadcasted_iota(jnp.int32, (tq, tq), 0)
            cc = lax.broadcasted_iota(jnp.int32, (tq, tq), 1)
            keep = jnp.concatenate([cc <= r, cc <= r], axis=0)
            s = jnp.where(keep, s, NEG)
        m_new = jnp.maximum(m, jnp.max(s, axis=-1, keepdims=True))
        alpha = jnp.exp(m - m_new)
        p = jnp.exp(s - m_new)
        l = alpha * l + jnp.sum(p, axis=-1, keepdims=True)
        acc = alpha * acc + _dot(p.astype(BF16), v_ref[0, rows, :])
        return m_new, l, acc

    init = (jnp.full((2 * tq, 1), NEG, F32), jnp.zeros((2 * tq, 1), F32), jnp.zeros((2 * tq, HEAD), F32))
    carry = lax.fori_loop(0, qi, functools.partial(step, masked=False), init)
    m, l, acc = step(qi, carry, True)
    out = acc / l
    o = out[0:tq] - lam * out[tq:2 * tq]
    ms = jnp.mean(o * o, axis=-1, keepdims=True)
    y = o * lax.rsqrt(ms + RMS_EPS) * gain_ref[...]
    o_ref[0] = (y * (1.0 - lam_init) * _silu(z_ref[0].astype(F32))).astype(o_ref.dtype)


def _diff_attn(proj, lam_vec, gain, lam_init, *, tq):
    b, l, _ = proj.shape

    def qcol(seg):
        return pl.BlockSpec((1, tq, HEAD), lambda bi, h, t, seg=seg: (bi, t, seg * N_HEADS + h))

    def kvcol(seg):
        return pl.BlockSpec((1, l, HEAD), lambda bi, h, t, seg=seg: (bi, 0, seg * N_HEADS + h))

    kern = functools.partial(_diff_kernel, lam_init=lam_init, tq=tq)
    return pl.pallas_call(
        kern, grid=(b, N_HEADS, l // tq),
        in_specs=[qcol(4), kvcol(5), kvcol(6), qcol(7),
                  pl.BlockSpec(lam_vec.shape, lambda bi, h, t: (0, 0)),
                  pl.BlockSpec((1, HEAD), lambda bi, h, t: (0, 0))],
        out_specs=pl.BlockSpec((1, tq, HEAD), lambda bi, h, t: (bi, t, h)),
        out_shape=jax.ShapeDtypeStruct((b, l, BRANCH), BF16),
        compiler_params=_params("parallel", "parallel", "arbitrary"), name="diff_attn",
    )(proj, proj, proj, proj, lam_vec, gain)


def _sb_kernel(q_ref, k_ref, v_ref, z_ref, o_ref, *, tq):
    qi = pl.program_id(2)
    q = (q_ref[0].astype(F32) * (HEAD ** -0.5)).astype(BF16)
    r = lax.broadcasted_iota(jnp.int32, (tq, tq), 0)
    cc = lax.broadcasted_iota(jnp.int32, (tq, tq), 1)
    later = (r > cc).astype(BF16)

    def block(kj, run, acc, masked):
        rows = pl.ds(pl.multiple_of(kj * tq, tq), tq)
        zz = _dot_nt(q, k_ref[0, rows, :])
        sp = _softplus(zz)
        log_rest = -sp
        if masked:
            earlier = cc < r
            log_rest = jnp.where(earlier, log_rest, 0.0)
        hi = log_rest.astype(BF16)
        lo = (log_rest - hi.astype(F32)).astype(BF16)
        after = _dot(hi, later) + _dot(lo, later) + run
        w = jnp.exp(zz - sp + after)
        if masked:
            w = jnp.where(earlier, w, 0.0)
        acc = acc + _dot(w.astype(BF16), v_ref[0, rows, :])
        run = run + jnp.sum(log_rest, axis=-1, keepdims=True)
        return run, acc

    run, acc = block(qi, jnp.zeros((tq, 1), F32), jnp.zeros((tq, HEAD), F32), True)

    def cond(carry):
        kj, go, _, _ = carry
        return jnp.logical_and(kj >= 0, go)

    def body(carry):
        kj, _, run, acc = carry
        run, acc = block(kj, run, acc, False)
        return kj - 1, jnp.max(run) > -SB_SKIP, run, acc

    _, _, _, acc = lax.while_loop(cond, body, (qi - 1, jnp.max(run) > -SB_SKIP, run, acc))
    o_ref[0] = (acc * _silu(z_ref[0].astype(F32))).astype(o_ref.dtype)


def _stick_breaking(proj, *, tq):
    b, l, _ = proj.shape

    def qcol(seg):
        return pl.BlockSpec((1, tq, HEAD), lambda bi, h, t, seg=seg: (bi, t, seg * N_HEADS + h))

    def kvcol(seg):
        return pl.BlockSpec((1, l, HEAD), lambda bi, h, t, seg=seg: (bi, 0, seg * N_HEADS + h))

    return pl.pallas_call(
        functools.partial(_sb_kernel, tq=tq), grid=(b, N_HEADS, l // tq),
        in_specs=[qcol(4), kvcol(5), kvcol(6), qcol(7)],
        out_specs=pl.BlockSpec((1, tq, HEAD), lambda bi, h, t: (bi, t, h)),
        out_shape=jax.ShapeDtypeStruct((b, l, BRANCH), BF16),
        compiler_params=_params("parallel", "parallel", "arbitrary"), name="stick_breaking",
    )(proj, proj, proj, proj)


def _unit_lower_inverse(lw, eye, blockdiag):
    ld = lw * blockdiag
    lo = lw - ld
    td = eye - ld
    p = _dot_exact(ld, ld)
    n_factors = int(math.log2(INV_BLOCK)) - 1
    for i in range(n_factors):
        td = td + _dot_exact(td, p)
        if i < n_factors - 1:
            p = _dot_exact(p, p)
    n = _dot_exact(td, lo)
    n2 = _dot_exact(n, n)
    y = eye - n
    y = y + _dot_exact(y, n2)
    return _dot_exact(y, td)


def _gdn_kernel(q_ref, k_ref, v_ref, z_ref, ab_ref, wq_ref, wk_ref, wv_ref, alog_ref, dtb_ref, gain_ref,
                o_ref, ext_scr, tail_scr, s_scr, *, n_chunks, c):
    h = pl.program_id(1)
    t_blk = n_chunks * c

    @pl.when(pl.program_id(2) == 0)
    def _():
        tail_scr[...] = jnp.zeros_like(tail_scr)
        s_scr[...] = jnp.zeros_like(s_scr)

    def conv(x_ref, w_ref, idx):
        x = x_ref[0].astype(F32)
        ext_scr[idx, 0:8, :] = tail_scr[idx]
        ext_scr[idx, 8:8 + t_blk, :] = x
        tail_scr[idx] = x[t_blk - 8:t_blk]
        w = w_ref[...]
        y = w[GD_CONV - 1:GD_CONV] * x
        for tap in range(GD_CONV - 1):
            y = y + w[tap:tap + 1] * ext_scr[idx, pl.ds(8 - (GD_CONV - 1) + tap, t_blk), :]
        return _silu(y)

    def l2n(x):
        return x * lax.rsqrt(jnp.sum(x * x, axis=-1, keepdims=True) + RMS_EPS)

    q_all = l2n(conv(q_ref, wq_ref, 0)) * (HEAD ** -0.5)
    k_all = l2n(conv(k_ref, wk_ref, 1))
    v_all = conv(v_ref, wv_ref, 2)

    ab = ab_ref[0]
    lane = lax.broadcasted_iota(jnp.int32, ab.shape, 1)
    g_lanes = -jnp.exp(alog_ref[...]) * _softplus(ab + dtb_ref[...])
    g_all = jnp.sum(jnp.where(lane == h, g_lanes, 0.0), axis=-1, keepdims=True)
    beta_all = jnp.sum(jnp.where(lane == h + N_HEADS, _sigmoid(ab), 0.0), axis=-1, keepdims=True)

    r = lax.broadcasted_iota(jnp.int32, (c, c), 0)
    cc = lax.broadcasted_iota(jnp.int32, (c, c), 1)
    incl = cc <= r
    strict = cc < r
    tri = incl.astype(F32)
    strict_f = strict.astype(F32)
    eye = (cc == r).astype(F32)
    blockdiag = ((r // INV_BLOCK) == (cc // INV_BLOCK)).astype(F32)
    gain = gain_ref[...]

    pre = []
    for ci in range(n_chunks):
        sl = slice(ci * c, (ci + 1) * c)
        q, k, v = q_all[sl], k_all[sl], v_all[sl]
        g = g_all[sl]
        beta = beta_all[sl]
        gc = _dot_exact(tri, jnp.broadcast_to(g, (c, HEAD)))
        dg = _dot_exact(tri, jnp.broadcast_to(g, (c, c)) * strict_f)
        decay = jnp.exp(jnp.where(incl, dg, NEG))
        kb = k * beta
        kbf = k.astype(BF16)
        lw = jnp.where(strict, _dot_nt(kb.astype(BF16), kbf) * decay, 0.0)
        tinv = _unit_lower_inverse(lw, eye, blockdiag).astype(BF16)
        egc = jnp.exp(gc)
        u = _dot(tinv, (v * beta).astype(BF16))
        w = _dot(tinv, (kb * egc).astype(BF16))
        attn = _dot_nt(q.astype(BF16), kbf) * decay
        qg = q * egc
        gc_last = gc[c - 1:c, :]
        kdec = k * jnp.exp(gc_last - gc)
        pre.append((u, w.astype(BF16), attn.astype(BF16), qg.astype(BF16), kdec.astype(BF16), jnp.exp(gc_last)))

    s = s_scr[...]
    for ci in range(n_chunks):
        u, w, attn, qg, kdec, gl = pre[ci]
        sb = s.astype(BF16)
        v_new = u - _dot(w, sb)
        o = _dot(qg, sb) + _dot(attn, v_new.astype(BF16))
        s = gl * s + lax.dot_general(kdec, v_new.astype(BF16), (((0,), (0,)), ((), ())),
                                     preferred_element_type=F32)
        ms = jnp.mean(o * o, axis=-1, keepdims=True)
        y = o * lax.rsqrt(ms + RMS_EPS) * gain
        rows = pl.ds(ci * c, c)
        o_ref[0, rows, :] = (y * _silu(z_ref[0, rows, :].astype(F32))).astype(o_ref.dtype)
    s_scr[...] = s


def _gdn(proj, ab, conv_w, a_log, dt_bias, gain, *, t_blk, c):
    b, l, _ = proj.shape

    def col(seg):
        return pl.BlockSpec((1, t_blk, HEAD), lambda bi, h, t, seg=seg: (bi, t, seg * N_HEADS + h))

    def wcol(seg):
        return pl.BlockSpec((GD_CONV, HEAD), lambda bi, h, t, seg=seg: (0, seg * N_HEADS + h))

    one = pl.BlockSpec((1, HEAD), lambda bi, h, t: (0, 0))
    kern = functools.partial(_gdn_kernel, n_chunks=t_blk // c, c=c)
    return pl.pallas_call(
        kern, grid=(b, N_HEADS, l // t_blk),
        in_specs=[col(0), col(1), col(2), col(3),
                  pl.BlockSpec((1, t_blk, HEAD), lambda bi, h, t: (bi, t, 0)),
                  wcol(0), wcol(1), wcol(2), one, one, one],
        out_specs=pl.BlockSpec((1, t_blk, HEAD), lambda bi, h, t: (bi, t, h)),
        out_shape=jax.ShapeDtypeStruct((b, l, BRANCH), BF16),
        scratch_shapes=[pltpu.VMEM((3, t_blk + 8, HEAD), F32), pltpu.VMEM((3, 8, HEAD), F32),
                        pltpu.VMEM((HEAD, HEAD), F32)],
        compiler_params=_params("parallel", "parallel", "arbitrary"), name="gated_deltanet",
    )(proj, proj, proj, proj, ab, conv_w, conv_w, conv_w, a_log, dt_bias, gain)


def _pad_lanes(vec):
    return jnp.zeros((1, HEAD), F32).at[0, :vec.shape[0]].set(vec.astype(F32))


def kernel(x, norm_pre, norm_post, ev_w_in, ev_w_out, hg_lb_logits, hg_norm, df_lambda, df_norm,
           od_w_in, od_w_out, gd_conv, gd_a_log, gd_dt_bias, gd_norm):
    p = dict(norm_pre=norm_pre, norm_post=norm_post, ev_w_in=ev_w_in, ev_w_out=ev_w_out,
             hg_lb_logits=hg_lb_logits, hg_norm=hg_norm, df_lambda=df_lambda, df_norm=df_norm,
             od_w_in=od_w_in, od_w_out=od_w_out, gd_conv=gd_conv, gd_a_log=gd_a_log,
             gd_dt_bias=gd_dt_bias, gd_norm=gd_norm)
    return _run_layers(x, p, range(norm_pre.shape[0]))


def _run_layers(x, p, layers):
    norm_pre, norm_post = p["norm_pre"], p["norm_post"]
    ev_w_in, ev_w_out, od_w_in, od_w_out = p["ev_w_in"], p["ev_w_out"], p["od_w_in"], p["od_w_out"]
    hg_lb_logits, hg_norm, df_lambda, df_norm = p["hg_lb_logits"], p["hg_norm"], p["df_lambda"], p["df_norm"]
    gd_conv, gd_a_log, gd_dt_bias, gd_norm = p["gd_conv"], p["gd_a_log"], p["gd_dt_bias"], p["gd_norm"]
    b, l, d = x.shape
    m = b * l
    tm_in = min(1024, m)
    tm_out = min(256, m)
    t_rec = min(REC_BLOCK, l)
    t_att = min(ATT_BLOCK, l)
    x2 = x.reshape(m, d)
    for layer in layers:
        j = layer // 2
        g_pre = norm_pre[layer].reshape(1, d)
        g_post = norm_post[layer].reshape(1, d)
        if layer % 2 == 0:
            proj = _inproj(x2, g_pre, ev_w_in[j].astype(BF16), tm=tm_in, tn=1024).reshape(b, l, -1)
            lam_init = 0.8 - 0.6 * math.exp(-0.3 * layer)
            mix_a = _hgrn(proj, hg_lb_logits.astype(F32), hg_norm[j].reshape(1, HEAD), j, t_blk=t_rec, c=CHUNK)
            mix_b = _diff_attn(proj, df_lambda[j].astype(F32), df_norm[j].reshape(1, HEAD), lam_init, tq=t_att)
            w_out = ev_w_out[j].astype(BF16)
        else:
            w = od_w_in[j]
            w_main = jnp.concatenate([w[:, :4 * BRANCH], w[:, 4 * BRANCH + 2 * N_HEADS:]], axis=1).astype(BF16)
            w_ab = jnp.pad(w[:, 4 * BRANCH:4 * BRANCH + 2 * N_HEADS], ((0, 0), (0, HEAD - 2 * N_HEADS))).astype(BF16)
            proj, ab = _inproj(x2, g_pre, w_main, w_ab, tm=tm_in, tn=1024)
            proj = proj.reshape(b, l, -1)
            mix_a = _gdn(proj, ab.reshape(b, l, HEAD), gd_conv[j].astype(F32), _pad_lanes(gd_a_log[j]),
                         _pad_lanes(gd_dt_bias[j]), gd_norm[j].reshape(1, HEAD), t_blk=t_rec, c=CHUNK)
            mix_b = _stick_breaking(proj, tq=t_att)
            w_out = od_w_out[j].astype(BF16)
        x2 = _outproj(mix_a.reshape(m, BRANCH), mix_b.reshape(m, BRANCH), w_out[:BRANCH], w_out[BRANCH:],
                      g_post, x2, tm=tm_out)
    return x2.reshape(b, l, d)
```

```python
import functools
import math

import numpy as np
import jax
import jax.numpy as jnp
from jax import lax
from jax.experimental import pallas as pl
from jax.experimental.pallas import tpu as pltpu

F32 = jnp.float32
BF16 = jnp.bfloat16
HIGHEST = lax.Precision.HIGHEST

D_MODEL = 2048
BRANCH = D_MODEL // 2
N_HEADS = 8
HEAD = BRANCH // N_HEADS
DF_DH = HEAD // 2
GD_CONV = 4
RMS_EPS = 1e-6
NEG = -1e30
GATE_FLOOR = 1e-20
SB_SKIP = 104.0

V7X_VMEM_LIMIT = 56 * 1024 * 1024

CHUNK = 64
REC_BLOCK = 256
ATT_BLOCK = 256
DIFF_TRIP = 1024
INV_BLOCK = 16
GD_PREP_BLOCK = 512
GD_HALO = 16


def _params(*sem):
    return pltpu.CompilerParams(dimension_semantics=sem, vmem_limit_bytes=V7X_VMEM_LIMIT)


def _sigmoid(x):
    return 1.0 / (1.0 + jnp.exp(-x))


def _silu(x):
    return x * _sigmoid(x)


def _softplus(x):
    return jnp.maximum(x, 0.0) + jnp.log(1.0 + jnp.exp(-jnp.abs(x)))


def _dot(a, b):
    return jnp.dot(a, b, preferred_element_type=F32)


def _dot_nt(a, b):
    return lax.dot_general(a, b, (((1,), (1,)), ((), ())), preferred_element_type=F32)


def _dot_exact(a, b):
    return jnp.dot(a, b, precision=HIGHEST, preferred_element_type=F32)


def _inproj_kernel(x_ref, g_ref, w_ref, o_ref, h_scr):
    @pl.when(pl.program_id(1) == 0)
    def _():
        x = x_ref[...]
        ms = jnp.mean(x * x, axis=-1, keepdims=True)
        h_scr[...] = (x * lax.rsqrt(ms + RMS_EPS) * g_ref[...]).astype(BF16)

    o_ref[...] = _dot(h_scr[...], w_ref[...]).astype(o_ref.dtype)


def _inproj_ab_kernel(x_ref, g_ref, w_ref, wab_ref, o_ref, oab_ref, h_scr):
    @pl.when(pl.program_id(1) == 0)
    def _():
        x = x_ref[...]
        ms = jnp.mean(x * x, axis=-1, keepdims=True)
        h_scr[...] = (x * lax.rsqrt(ms + RMS_EPS) * g_ref[...]).astype(BF16)
        oab_ref[...] = _dot(h_scr[...], wab_ref[...])

    o_ref[...] = _dot(h_scr[...], w_ref[...]).astype(o_ref.dtype)


def _inproj(x2, gain, w, w_ab=None, *, tm, tn):
    m, d = x2.shape
    n = w.shape[1]
    grid = (m // tm, n // tn)
    x_spec = pl.BlockSpec((tm, d), lambda i, j: (i, 0))
    g_spec = pl.BlockSpec((1, d), lambda i, j: (0, 0))
    w_spec = pl.BlockSpec((d, tn), lambda i, j: (0, j))
    o_spec = pl.BlockSpec((tm, tn), lambda i, j: (i, j))
    scratch = [pltpu.VMEM((tm, d), BF16)]
    if w_ab is None:
        return pl.pallas_call(
            _inproj_kernel, grid=grid, in_specs=[x_spec, g_spec, w_spec], out_specs=o_spec,
            out_shape=jax.ShapeDtypeStruct((m, n), BF16), scratch_shapes=scratch,
            compiler_params=_params("parallel", "arbitrary"), name="inproj",
        )(x2, gain, w)
    nab = w_ab.shape[1]
    return pl.pallas_call(
        _inproj_ab_kernel, grid=grid,
        in_specs=[x_spec, g_spec, w_spec, pl.BlockSpec((d, nab), lambda i, j: (0, 0))],
        out_specs=[o_spec, pl.BlockSpec((tm, nab), lambda i, j: (i, 0))],
        out_shape=[jax.ShapeDtypeStruct((m, n), BF16), jax.ShapeDtypeStruct((m, nab), F32)],
        scratch_shapes=scratch, compiler_params=_params("parallel", "arbitrary"), name="inproj_ab",
    )(x2, gain, w, w_ab)


def _outproj_kernel(ma_ref, mb_ref, wa_ref, wb_ref, g_ref, x_ref, o_ref):
    y = _dot(ma_ref[...], wa_ref[...]) + _dot(mb_ref[...], wb_ref[...])
    ms = jnp.mean(y * y, axis=-1, keepdims=True)
    o_ref[...] = x_ref[...] + y * lax.rsqrt(ms + RMS_EPS) * g_ref[...]


def _outproj(mix_a, mix_b, w_a, w_b, gain, x2, *, tm):
    m, d = x2.shape
    k = mix_a.shape[1]
    return pl.pallas_call(
        _outproj_kernel, grid=(m // tm,),
        in_specs=[pl.BlockSpec((tm, k), lambda i: (i, 0)), pl.BlockSpec((tm, k), lambda i: (i, 0)),
                  pl.BlockSpec((k, d), lambda i: (0, 0)), pl.BlockSpec((k, d), lambda i: (0, 0)),
                  pl.BlockSpec((1, d), lambda i: (0, 0)), pl.BlockSpec((tm, d), lambda i: (i, 0))],
        out_specs=pl.BlockSpec((tm, d), lambda i: (i, 0)),
        out_shape=jax.ShapeDtypeStruct((m, d), F32),
        compiler_params=_params("parallel"), name="outproj",
    )(mix_a, mix_b, w_a, w_b, gain, x2)


def _hgrn_constants(c):
    t = np.arange(c)
    tri = (t[None, :] <= t[:, None]).astype(np.float32)
    ops = [tri]
    masks = [np.eye(c, dtype=np.float32)]
    b = 1
    while b < c:
        mid = (t // (2 * b)) * (2 * b) + b
        ops.append(tri - (t[None, :] <= mid[:, None]).astype(np.float32))
        same = (t[:, None] // (2 * b)) == (t[None, :] // (2 * b))
        masks.append((same & (t[:, None] >= mid[:, None]) & (t[None, :] < mid[:, None])).astype(np.float32))
        b *= 2
    return np.concatenate(ops, axis=0), np.stack(masks, axis=0)


def _hgrn_kernel(q_ref, f_ref, i_ref, z_ref, lbl_ref, gain_ref, ops_ref, mask_ref, o_ref, st_scr,
                 *, layer_j, n_chunks, c):
    @pl.when(pl.program_id(2) == 0)
    def _():
        st_scr[...] = jnp.zeros_like(st_scr)

    logits = lbl_ref[...]
    e = jnp.exp(logits - jnp.max(logits, axis=0, keepdims=True))
    p = e / jnp.sum(e, axis=0, keepdims=True)
    lb = jnp.sum(p[0:layer_j + 1], axis=0, keepdims=True) - p[0:1]

    n_levels = mask_ref.shape[0] - 1
    gain = gain_ref[...]
    st = st_scr[...]
    for ci in range(n_chunks):
        rows = pl.ds(ci * c, c)
        q = _silu(q_ref[0, rows, :].astype(F32))
        sig = _sigmoid(f_ref[0, rows, :].astype(F32))
        g = jnp.log(jnp.maximum(lb + (1.0 - lb) * sig, GATE_FLOOR))
        k = (1.0 - lb) * (1.0 - sig)
        v = i_ref[0, rows, :].astype(F32)
        vb = v.astype(BF16)

        expo = _dot_exact(ops_ref[...], g)
        a = expo[0:c]
        scores = mask_ref[0] * _dot_nt(q.astype(BF16), k.astype(BF16))
        for lv in range(n_levels):
            ex = expo[(lv + 1) * c:(lv + 2) * c]
            qd = (q * jnp.exp(jnp.minimum(ex, 0.0))).astype(BF16)
            kd = (k * jnp.exp(jnp.minimum(-ex, 0.0))).astype(BF16)
            scores = scores + mask_ref[lv + 1] * _dot_nt(qd, kd)
        o = _dot(scores.astype(BF16), vb)
        o = o + _dot_nt((q * jnp.exp(a)).astype(BF16), st.astype(BF16))
        a_last = a[c - 1:c, :]
        k_end = (k * jnp.exp(a_last - a)).astype(BF16)
        st = jnp.exp(a_last) * st + lax.dot_general(vb, k_end, (((0,), (0,)), ((), ())),
                                                    preferred_element_type=F32)

        ms = jnp.mean(o * o, axis=-1, keepdims=True)
        y = o * lax.rsqrt(ms + RMS_EPS) * gain
        o_ref[0, rows, :] = (y * _silu(z_ref[0, rows, :].astype(F32))).astype(o_ref.dtype)
    st_scr[...] = st


def _hgrn(proj, lb_logits, gain, layer_j, *, t_blk, c):
    b, l, _ = proj.shape
    ops, masks = _hgrn_constants(c)
    n_even = lb_logits.shape[0]

    def col(seg):
        return pl.BlockSpec((1, t_blk, HEAD), lambda bi, h, t, seg=seg: (bi, t, seg * N_HEADS + h))

    kern = functools.partial(_hgrn_kernel, layer_j=layer_j, n_chunks=t_blk // c, c=c)
    return pl.pallas_call(
        kern, grid=(b, N_HEADS, l // t_blk),
        in_specs=[col(0), col(1), col(2), col(3),
                  pl.BlockSpec((n_even, HEAD), lambda bi, h, t: (0, h)),
                  pl.BlockSpec((1, HEAD), lambda bi, h, t: (0, 0)),
                  pl.BlockSpec(ops.shape, lambda bi, h, t: (0, 0)),
                  pl.BlockSpec(masks.shape, lambda bi, h, t: (0, 0, 0))],
        out_specs=pl.BlockSpec((1, t_blk, HEAD), lambda bi, h, t: (bi, t, h)),
        out_shape=jax.ShapeDtypeStruct((b, l, BRANCH), BF16),
        scratch_shapes=[pltpu.VMEM((HEAD, HEAD), F32)],
        compiler_params=_params("parallel", "parallel", "arbitrary"), name="hgrn2",
    )(proj, proj, proj, proj, lb_logits, gain, jnp.asarray(ops), jnp.asarray(masks))


def _diff_kernel(q_ref, k_ref, v_ref, z_ref, lamv_ref, gain_ref, o_ref, vt_scr, sa_scr, sb_scr, *, lam_init, tq, tk):
    qi = pl.program_id(2)
    n_trips = k_ref.shape[1] // tk

    @pl.when(qi == 0)
    def _():
        def transpose_block(i, carry):
            rows = pl.ds(pl.multiple_of(i * tk, tk), tk)
            vt_scr[i] = v_ref[0, rows, :].astype(F32).T.astype(BF16)
            return carry
        lax.fori_loop(0, n_trips, transpose_block, 0)

    lv = lamv_ref[...]
    d01 = jnp.sum(jnp.sum(lv[0:1] * lv[1:2], axis=-1, keepdims=True), axis=0, keepdims=True)
    d23 = jnp.sum(jnp.sum(lv[2:3] * lv[3:4], axis=-1, keepdims=True), axis=0, keepdims=True)
    lam = jnp.exp(d01) - jnp.exp(d23) + lam_init

    q = q_ref[0].astype(F32) * (DF_DH ** -0.5 * math.log2(math.e))
    lane = lax.broadcasted_iota(jnp.int32, q.shape, 1)
    zero = jnp.zeros_like(q)
    halves = (jnp.where(lane < DF_DH, q, zero).astype(BF16), jnp.where(lane >= DF_DH, q, zero).astype(BF16))

    def scores(t, s_buf):
        k = k_ref[0, pl.ds(pl.multiple_of(t * tk, tk), tk), :]
        tops = []
        for hf, qh in enumerate(halves):
            s = _dot_nt(k, qh)
            s_buf[hf] = s
            tops.append(jnp.max(s, axis=0, keepdims=True))
        return tuple(tops)

    def absorb(t, s_buf, state, tops, key_limit=None):
        vt = vt_scr[t]
        out = []
        for hf in range(2):
            m, l, acc = state[hf]
            s = s_buf[hf]
            top = tops[hf]
            if key_limit is not None:
                r = lax.broadcasted_iota(jnp.int32, (tk, tq), 0)
                cc = lax.broadcasted_iota(jnp.int32, (tk, tq), 1)
                s = jnp.where(r - cc <= key_limit, s, NEG)
                top = jnp.max(s, axis=0, keepdims=True)
            m_new = jnp.maximum(m, top)
            alpha = jnp.exp2(m - m_new)
            p = jnp.exp2(s - m_new)
            l = alpha * l + jnp.sum(p, axis=0, keepdims=True)
            acc = alpha * acc + _dot(vt, p.astype(BF16))
            out.append((m_new, l, acc))
        return tuple(out)

    n_full = (qi * tq) // tk
    state = tuple((jnp.full((1, tq), NEG, F32), jnp.zeros((1, tq), F32), jnp.zeros((HEAD, tq), F32))
                  for _ in halves)
    tops = scores(0, sa_scr)

    def pair(i, carry):
        state, tops_a = carry
        t = 2 * i
        tops_b = scores(t + 1, sb_scr)
        state = absorb(t, sa_scr, state, tops_a)
        tops_a = scores(t + 2, sa_scr)
        state = absorb(t + 1, sb_scr, state, tops_b)
        return state, tops_a

    state, tops = lax.fori_loop(0, n_full // 2, pair, (state, tops))
    key_limit = qi * tq - n_full * tk

    def finish(state):
        (_, l1, acc1), (_, l2, acc2) = state
        o = (acc1 / l1 - lam * (acc2 / l2)).T
        ms = jnp.mean(o * o, axis=-1, keepdims=True)
        y = o * lax.rsqrt(ms + RMS_EPS) * gain_ref[...]
        o_ref[0] = (y * (1.0 - lam_init) * _silu(z_ref[0].astype(F32))).astype(o_ref.dtype)

    @pl.when(n_full % 2 == 0)
    def _():
        finish(absorb(n_full, sa_scr, state, tops, key_limit=key_limit))

    @pl.when(n_full % 2 == 1)
    def _():
        tops_b = scores(n_full, sb_scr)
        mid = absorb(n_full - 1, sa_scr, state, tops)
        finish(absorb(n_full, sb_scr, mid, tops_b, key_limit=key_limit))


def _diff_attn(proj, lam_vec, gain, lam_init, *, tq):
    b, l, _ = proj.shape

    def qcol(seg):
        return pl.BlockSpec((1, tq, HEAD), lambda bi, h, t, seg=seg: (bi, t, seg * N_HEADS + h))

    def kvcol(seg):
        return pl.BlockSpec((1, l, HEAD), lambda bi, h, t, seg=seg: (bi, 0, seg * N_HEADS + h))

    tk = min(DIFF_TRIP, l)
    kern = functools.partial(_diff_kernel, lam_init=lam_init, tq=tq, tk=tk)
    return pl.pallas_call(
        kern, grid=(b, N_HEADS, l // tq),
        in_specs=[qcol(4), kvcol(5), kvcol(6), qcol(7),
                  pl.BlockSpec(lam_vec.shape, lambda bi, h, t: (0, 0)),
                  pl.BlockSpec((1, HEAD), lambda bi, h, t: (0, 0))],
        out_specs=pl.BlockSpec((1, tq, HEAD), lambda bi, h, t: (bi, t, h)),
        out_shape=jax.ShapeDtypeStruct((b, l, BRANCH), BF16),
        scratch_shapes=[pltpu.VMEM((l // tk, HEAD, tk), BF16), pltpu.VMEM((2, tk, tq), F32), pltpu.VMEM((2, tk, tq), F32)],
        compiler_params=_params("parallel", "parallel", "arbitrary"), name="diff_attn",
    )(proj, proj, proj, proj, lam_vec, gain)


def _sb_kernel(q_ref, k_ref, v_ref, z_ref, o_ref, *, tq):
    qi = pl.program_id(2)
    q = (q_ref[0].astype(F32) * (HEAD ** -0.5)).astype(BF16)
    r = lax.broadcasted_iota(jnp.int32, (tq, tq), 0)
    cc = lax.broadcasted_iota(jnp.int32, (tq, tq), 1)
    later = (r > cc).astype(BF16)

    def block(kj, run, acc, masked):
        rows = pl.ds(pl.multiple_of(kj * tq, tq), tq)
        zz = _dot_nt(q, k_ref[0, rows, :])
        sp = _softplus(zz)
        log_rest = -sp
        if masked:
            earlier = cc < r
            log_rest = jnp.where(earlier, log_rest, 0.0)
        hi = log_rest.astype(BF16)
        lo = (log_rest - hi.astype(F32)).astype(BF16)
        after = _dot(hi, later) + _dot(lo, later) + run
        w = jnp.exp(zz - sp + after)
        if masked:
            w = jnp.where(earlier, w, 0.0)
        acc = acc + _dot(w.astype(BF16), v_ref[0, rows, :])
        run = run + jnp.sum(log_rest, axis=-1, keepdims=True)
        return run, acc

    run, acc = block(qi, jnp.zeros((tq, 1), F32), jnp.zeros((tq, HEAD), F32), True)

    def cond(carry):
        kj, go, _, _ = carry
        return jnp.logical_and(kj >= 0, go)

    def body(carry):
        kj, _, run, acc = carry
        run, acc = block(kj, run, acc, False)
        return kj - 1, jnp.max(run) > -SB_SKIP, run, acc

    _, _, _, acc = lax.while_loop(cond, body, (qi - 1, jnp.max(run) > -SB_SKIP, run, acc))
    o_ref[0] = (acc * _silu(z_ref[0].astype(F32))).astype(o_ref.dtype)


def _stick_breaking(proj, *, tq):
    b, l, _ = proj.shape

    def qcol(seg):
        return pl.BlockSpec((1, tq, HEAD), lambda bi, h, t, seg=seg: (bi, t, seg * N_HEADS + h))

    def kvcol(seg):
        return pl.BlockSpec((1, l, HEAD), lambda bi, h, t, seg=seg: (bi, 0, seg * N_HEADS + h))

    return pl.pallas_call(
        functools.partial(_sb_kernel, tq=tq), grid=(b, N_HEADS, l // tq),
        in_specs=[qcol(4), kvcol(5), kvcol(6), qcol(7)],
        out_specs=pl.BlockSpec((1, tq, HEAD), lambda bi, h, t: (bi, t, h)),
        out_shape=jax.ShapeDtypeStruct((b, l, BRANCH), BF16),
        compiler_params=_params("parallel", "parallel", "arbitrary"), name="stick_breaking",
    )(proj, proj, proj, proj)


def _split2(x):
    hi = x.astype(BF16)
    return hi, (x - hi.astype(F32)).astype(BF16)


def _dot3(a, b):
    a_hi, a_lo = _split2(a)
    b_hi, b_lo = _split2(b)
    return _dot(a_hi, b_hi) + (_dot(a_hi, b_lo) + _dot(a_lo, b_hi))


def _dot_const3(const_bf16, x):
    hi = x.astype(BF16)
    r1 = x - hi.astype(F32)
    mid = r1.astype(BF16)
    lo = (r1 - mid.astype(F32)).astype(BF16)
    return _dot(const_bf16, hi) + (_dot(const_bf16, mid) + _dot(const_bf16, lo))


def _each(fn, *lists):
    return [fn(*args) for args in zip(*lists)]


def _unit_lower_inverse(lws, eye, blockdiag):
    ld = [lw * blockdiag for lw in lws]
    lo = _each(lambda a, b: a - b, lws, ld)
    td = [eye - x for x in ld]
    p = _each(_dot3, ld, ld)
    n_factors = int(math.log2(INV_BLOCK)) - 1
    for i in range(n_factors):
        td = _each(lambda a, b: a + _dot3(a, b), td, p)
        if i < n_factors - 1:
            p = _each(_dot3, p, p)
    n = _each(_dot3, td, lo)
    n2 = _each(_dot3, n, n)
    y = [eye - x for x in n]
    y = _each(lambda a, b: a + _dot3(a, b), y, n2)
    return _each(_dot3, y, td)


def _gdn_prep_kernel(q_ref, k_ref, v_ref, qh_ref, kh_ref, vh_ref, ab_ref, wq_ref, wk_ref, wv_ref, alog_ref, dtb_ref,
                     u_ref, w_ref, qg_ref, kd_ref, at_ref, gl_ref, ext_scr, *, n_chunks, c):
    h = pl.program_id(1)
    t = pl.program_id(2)
    t_blk = n_chunks * c

    def conv(x_ref, halo_ref, w_ref, idx):
        x = x_ref[0].astype(F32)
        ext_scr[idx, 0:GD_HALO, :] = jnp.where(t > 0, halo_ref[0].astype(F32), 0.0)
        ext_scr[idx, GD_HALO:GD_HALO + t_blk, :] = x
        w = w_ref[...]
        y = w[GD_CONV - 1:GD_CONV] * x
        for tap in range(GD_CONV - 1):
            y = y + w[tap:tap + 1] * ext_scr[idx, pl.ds(GD_HALO - (GD_CONV - 1) + tap, t_blk), :]
        return _silu(y)

    def l2n(x):
        return x * lax.rsqrt(jnp.sum(x * x, axis=-1, keepdims=True) + RMS_EPS)

    q_all = l2n(conv(q_ref, qh_ref, wq_ref, 0)) * (HEAD ** -0.5)
    k_all = l2n(conv(k_ref, kh_ref, wk_ref, 1))
    v_all = conv(v_ref, vh_ref, wv_ref, 2)

    ab = ab_ref[0]
    lane = lax.broadcasted_iota(jnp.int32, ab.shape, 1)
    g_lanes = -jnp.exp(alog_ref[...]) * _softplus(ab + dtb_ref[...])
    g_all = jnp.sum(jnp.where(lane == h, g_lanes, 0.0), axis=-1, keepdims=True)
    beta_all = jnp.sum(jnp.where(lane == h + N_HEADS, _sigmoid(ab), 0.0), axis=-1, keepdims=True)

    r = lax.broadcasted_iota(jnp.int32, (c, c), 0)
    cc = lax.broadcasted_iota(jnp.int32, (c, c), 1)
    incl = cc <= r
    strict = cc < r
    tri = incl.astype(BF16)
    eye = (cc == r).astype(F32)
    blockdiag = ((r // INV_BLOCK) == (cc // INV_BLOCK)).astype(F32)
    r2 = lax.broadcasted_iota(jnp.int32, (c, HEAD), 0)
    c2 = lax.broadcasted_iota(jnp.int32, (c, HEAD), 1)
    strict_wide = jnp.logical_and(c2 < r2, c2 < c).astype(F32)

    chunks = [slice(ci * c, (ci + 1) * c) for ci in range(n_chunks)]
    q = [q_all[sl] for sl in chunks]
    k = [k_all[sl] for sl in chunks]
    v = [v_all[sl] for sl in chunks]
    beta = [beta_all[sl] for sl in chunks]
    g_wide = [jnp.broadcast_to(g_all[sl], (c, HEAD)) for sl in chunks]
    sums = [_dot_const3(tri, jnp.concatenate([gw, gw * strict_wide], axis=1)) for gw in g_wide]
    gc = [s[:, 0:HEAD] for s in sums]
    decay = [jnp.exp(jnp.where(incl, s[:, HEAD:HEAD + c], NEG)) for s in sums]
    kb = _each(lambda a, b: a * b, k, beta)
    kbf = [x.astype(BF16) for x in k]
    lw = _each(lambda a, b, d: jnp.where(strict, _dot_nt(a.astype(BF16), b) * d, 0.0), kb, kbf, decay)
    tinv = [x.astype(BF16) for x in _unit_lower_inverse(lw, eye, blockdiag)]
    egc = [jnp.exp(x) for x in gc]
    u = _each(lambda ti, a, b: _dot(ti, (a * b).astype(BF16)), tinv, v, beta)
    w = _each(lambda ti, a, e: _dot(ti, (a * e).astype(BF16)), tinv, kb, egc)
    attn = _each(lambda a, b, d: _dot_nt(a.astype(BF16), b) * d, q, kbf, decay)
    for ci, sl in enumerate(chunks):
        gc_last = gc[ci][c - 1:c, :]
        u_ref[0, 0, sl, :] = u[ci]
        w_ref[0, 0, sl, :] = w[ci].astype(BF16)
        qg_ref[0, 0, sl, :] = (q[ci] * egc[ci]).astype(BF16)
        kd_ref[0, 0, sl, :] = (k[ci] * jnp.exp(gc_last - gc[ci])).astype(BF16)
        at_ref[0, 0, sl, :] = attn[ci].astype(BF16)
        gl_ref[0, 0, ci] = jnp.exp(gc_last)


def _gdn_scan_kernel(u_ref, w_ref, qg_ref, kd_ref, at_ref, gl_ref, z_ref, gain_ref, o_ref, s_scr, *, n_chunks, c):
    @pl.when(pl.program_id(1) == 0)
    def _():
        s_scr[...] = jnp.zeros_like(s_scr)

    heads = range(N_HEADS)
    gain = gain_ref[...]
    s = [s_scr[h] for h in heads]
    for ci in range(n_chunks):
        rows = pl.ds(ci * c, c)
        sb = [x.astype(BF16) for x in s]
        v_new = [u_ref[0, h, rows, :] - _dot(w_ref[0, h, rows, :], sb[h]) for h in heads]
        vb = [x.astype(BF16) for x in v_new]
        o = [_dot(qg_ref[0, h, rows, :], sb[h]) + _dot(at_ref[0, h, rows, :], vb[h]) for h in heads]
        s = [gl_ref[0, h, ci] * s[h]
             + lax.dot_general(kd_ref[0, h, rows, :], vb[h], (((0,), (0,)), ((), ())), preferred_element_type=F32)
             for h in heads]
        for h in heads:
            cols = slice(h * HEAD, (h + 1) * HEAD)
            ms = jnp.mean(o[h] * o[h], axis=-1, keepdims=True)
            y = o[h] * lax.rsqrt(ms + RMS_EPS) * gain
            o_ref[0, rows, cols] = (y * _silu(z_ref[0, rows, cols].astype(F32))).astype(o_ref.dtype)
    for h in heads:
        s_scr[h] = s[h]


def _gdn(proj, ab, conv_w, a_log, dt_bias, gain, *, t_prep, t_scan, c):
    b, l, _ = proj.shape
    halo_blocks = t_prep // GD_HALO

    def col(seg):
        return pl.BlockSpec((1, t_prep, HEAD), lambda bi, h, t, seg=seg: (bi, t, seg * N_HEADS + h))

    def halo(seg):
        return pl.BlockSpec((1, GD_HALO, HEAD),
                            lambda bi, h, t, seg=seg: (bi, jnp.maximum(t * halo_blocks - 1, 0), seg * N_HEADS + h))

    def wcol(seg):
        return pl.BlockSpec((GD_CONV, HEAD), lambda bi, h, t, seg=seg: (0, seg * N_HEADS + h))

    one = pl.BlockSpec((1, HEAD), lambda bi, h, t: (0, 0))

    def per_head(width):
        return pl.BlockSpec((1, 1, t_prep, width), lambda bi, h, t: (bi, h, t, 0))

    n_prep = t_prep // c
    u, w, qg, kd, at, gl = pl.pallas_call(
        functools.partial(_gdn_prep_kernel, n_chunks=n_prep, c=c), grid=(b, N_HEADS, l // t_prep),
        in_specs=[col(0), col(1), col(2), halo(0), halo(1), halo(2),
                  pl.BlockSpec((1, t_prep, HEAD), lambda bi, h, t: (bi, t, 0)),
                  wcol(0), wcol(1), wcol(2), one, one],
        out_specs=[per_head(HEAD), per_head(HEAD), per_head(HEAD), per_head(HEAD), per_head(c),
                   pl.BlockSpec((1, 1, n_prep, 1, HEAD), lambda bi, h, t: (bi, h, t, 0, 0))],
        out_shape=[jax.ShapeDtypeStruct((b, N_HEADS, l, HEAD), F32),
                   jax.ShapeDtypeStruct((b, N_HEADS, l, HEAD), BF16),
                   jax.ShapeDtypeStruct((b, N_HEADS, l, HEAD), BF16),
                   jax.ShapeDtypeStruct((b, N_HEADS, l, HEAD), BF16),
                   jax.ShapeDtypeStruct((b, N_HEADS, l, c), BF16),
                   jax.ShapeDtypeStruct((b, N_HEADS, l // c, 1, HEAD), F32)],
        scratch_shapes=[pltpu.VMEM((3, t_prep + GD_HALO, HEAD), F32)],
        compiler_params=_params("parallel", "parallel", "parallel"), name="gdn_prep",
    )(proj, proj, proj, proj, proj, proj, ab, conv_w, conv_w, conv_w, a_log, dt_bias)

    def all_heads(width):
        return pl.BlockSpec((1, N_HEADS, t_scan, width), lambda bi, t: (bi, 0, t, 0))

    n_scan = t_scan // c
    return pl.pallas_call(
        functools.partial(_gdn_scan_kernel, n_chunks=n_scan, c=c), grid=(b, l // t_scan),
        in_specs=[all_heads(HEAD), all_heads(HEAD), all_heads(HEAD), all_heads(HEAD), all_heads(c),
                  pl.BlockSpec((1, N_HEADS, n_scan, 1, HEAD), lambda bi, t: (bi, 0, t, 0, 0)),
                  pl.BlockSpec((1, t_scan, BRANCH), lambda bi, t: (bi, t, 3)),
                  pl.BlockSpec((1, HEAD), lambda bi, t: (0, 0))],
        out_specs=pl.BlockSpec((1, t_scan, BRANCH), lambda bi, t: (bi, t, 0)),
        out_shape=jax.ShapeDtypeStruct((b, l, BRANCH), BF16),
        scratch_shapes=[pltpu.VMEM((N_HEADS, HEAD, HEAD), F32)],
        compiler_params=_params("parallel", "arbitrary"), name="gdn_scan",
    )(u, w, qg, kd, at, gl, proj, gain)


def _pad_lanes(vec):
    return jnp.zeros((1, HEAD), F32).at[0, :vec.shape[0]].set(vec.astype(F32))


def kernel(x, norm_pre, norm_post, ev_w_in, ev_w_out, hg_lb_logits, hg_norm, df_lambda, df_norm,
           od_w_in, od_w_out, gd_conv, gd_a_log, gd_dt_bias, gd_norm):
    p = dict(norm_pre=norm_pre, norm_post=norm_post, ev_w_in=ev_w_in, ev_w_out=ev_w_out,
             hg_lb_logits=hg_lb_logits, hg_norm=hg_norm, df_lambda=df_lambda, df_norm=df_norm,
             od_w_in=od_w_in, od_w_out=od_w_out, gd_conv=gd_conv, gd_a_log=gd_a_log,
             gd_dt_bias=gd_dt_bias, gd_norm=gd_norm)
    return _run_layers(x, p, range(norm_pre.shape[0]))


def _run_layers(x, p, layers):
    norm_pre, norm_post = p["norm_pre"], p["norm_post"]
    ev_w_in, ev_w_out, od_w_in, od_w_out = p["ev_w_in"], p["ev_w_out"], p["od_w_in"], p["od_w_out"]
    hg_lb_logits, hg_norm, df_lambda, df_norm = p["hg_lb_logits"], p["hg_norm"], p["df_lambda"], p["df_norm"]
    gd_conv, gd_a_log, gd_dt_bias, gd_norm = p["gd_conv"], p["gd_a_log"], p["gd_dt_bias"], p["gd_norm"]
    b, l, d = x.shape
    m = b * l
    tm_in = min(1024, m)
    tm_out = min(256, m)
    t_rec = min(REC_BLOCK, l)
    t_att = min(ATT_BLOCK, l)
    x2 = x.reshape(m, d)
    for layer in layers:
        j = layer // 2
        g_pre = norm_pre[layer].reshape(1, d)
        g_post = norm_post[layer].reshape(1, d)
        if layer % 2 == 0:
            proj = _inproj(x2, g_pre, ev_w_in[j].astype(BF16), tm=tm_in, tn=1024).reshape(b, l, -1)
            lam_init = 0.8 - 0.6 * math.exp(-0.3 * layer)
            mix_a = _hgrn(proj, hg_lb_logits.astype(F32), hg_norm[j].reshape(1, HEAD), j, t_blk=t_rec, c=CHUNK)
            mix_b = _diff_attn(proj, df_lambda[j].astype(F32), df_norm[j].reshape(1, HEAD), lam_init, tq=t_att)
            w_out = ev_w_out[j].astype(BF16)
        else:
            w = od_w_in[j]
            w_main = jnp.concatenate([w[:, :4 * BRANCH], w[:, 4 * BRANCH + 2 * N_HEADS:]], axis=1).astype(BF16)
            w_ab = jnp.pad(w[:, 4 * BRANCH:4 * BRANCH + 2 * N_HEADS], ((0, 0), (0, HEAD - 2 * N_HEADS))).astype(BF16)
            proj, ab = _inproj(x2, g_pre, w_main, w_ab, tm=tm_in, tn=1024)
            proj = proj.reshape(b, l, -1)
            mix_a = _gdn(proj, ab.reshape(b, l, HEAD), gd_conv[j].astype(F32), _pad_lanes(gd_a_log[j]),
                         _pad_lanes(gd_dt_bias[j]), gd_norm[j].reshape(1, HEAD), t_prep=min(GD_PREP_BLOCK, l), t_scan=t_rec, c=CHUNK)
            mix_b = _stick_breaking(proj, tq=t_att)
            w_out = od_w_out[j].astype(BF16)
        x2 = _outproj(mix_a.reshape(m, BRANCH), mix_b.reshape(m, BRANCH), w_out[:BRANCH], w_out[BRANCH:],
                      g_post, x2, tm=tm_out)
    return x2.reshape(b, l, d)
```

```python
import functools
import math

import numpy as np
import jax
import jax.numpy as jnp
from jax import lax
from jax.experimental import pallas as pl
from jax.experimental.pallas import tpu as pltpu

F32 = jnp.float32
BF16 = jnp.bfloat16
HIGHEST = lax.Precision.HIGHEST

D_MODEL = 2048
BRANCH = D_MODEL // 2
N_HEADS = 8
HEAD = BRANCH // N_HEADS
DF_DH = HEAD // 2
GD_CONV = 4
RMS_EPS = 1e-6
NEG = -1e30
GATE_FLOOR = 1e-20
SB_SKIP = 104.0

V7X_VMEM_LIMIT = 56 * 1024 * 1024

CHUNK = 64
HG_CHUNK = 256
REC_BLOCK = 256
ATT_BLOCK = 256
DIFF_Q_BLOCK = 512
DIFF_TRIP = 512
INV_BLOCK = 16
GD_PREP_BLOCK = 512
GD_HALO = 16


def _params(*sem):
    return pltpu.CompilerParams(dimension_semantics=sem, vmem_limit_bytes=V7X_VMEM_LIMIT)


def _sigmoid(x):
    return 0.5 * jnp.tanh(0.5 * x) + 0.5


def _silu(x):
    return x * _sigmoid(x)


def _softplus(x):
    return jnp.maximum(x, 0.0) + jnp.log(1.0 + jnp.exp(-jnp.abs(x)))


def _dot(a, b):
    return jnp.dot(a, b, preferred_element_type=F32)


def _dot_nt(a, b):
    return lax.dot_general(a, b, (((1,), (1,)), ((), ())), preferred_element_type=F32)


def _dot_exact(a, b):
    return jnp.dot(a, b, precision=HIGHEST, preferred_element_type=F32)


def _inproj_kernel(x_ref, g_ref, w_ref, o_ref, h_scr):
    @pl.when(pl.program_id(1) == 0)
    def _():
        x = x_ref[...]
        ms = jnp.mean(x * x, axis=-1, keepdims=True)
        h_scr[...] = (x * lax.rsqrt(ms + RMS_EPS) * g_ref[...]).astype(BF16)

    o_ref[...] = _dot(h_scr[...], w_ref[...]).astype(o_ref.dtype)


def _inproj_ab_kernel(x_ref, g_ref, w_ref, wab_ref, o_ref, oab_ref, h_scr):
    @pl.when(pl.program_id(1) == 0)
    def _():
        x = x_ref[...]
        ms = jnp.mean(x * x, axis=-1, keepdims=True)
        h_scr[...] = (x * lax.rsqrt(ms + RMS_EPS) * g_ref[...]).astype(BF16)
        oab_ref[...] = _dot(h_scr[...], wab_ref[...])

    o_ref[...] = _dot(h_scr[...], w_ref[...]).astype(o_ref.dtype)


def _inproj(x2, gain, w, w_ab=None, *, tm, tn):
    m, d = x2.shape
    n = w.shape[1]
    grid = (m // tm, n // tn)
    x_spec = pl.BlockSpec((tm, d), lambda i, j: (i, 0))
    g_spec = pl.BlockSpec((1, d), lambda i, j: (0, 0))
    w_spec = pl.BlockSpec((d, tn), lambda i, j: (0, j))
    o_spec = pl.BlockSpec((tm, tn), lambda i, j: (i, j))
    scratch = [pltpu.VMEM((tm, d), BF16)]
    if w_ab is None:
        return pl.pallas_call(
            _inproj_kernel, grid=grid, in_specs=[x_spec, g_spec, w_spec], out_specs=o_spec,
            out_shape=jax.ShapeDtypeStruct((m, n), BF16), scratch_shapes=scratch,
            compiler_params=_params("parallel", "arbitrary"), name="inproj",
        )(x2, gain, w)
    nab = w_ab.shape[1]
    return pl.pallas_call(
        _inproj_ab_kernel, grid=grid,
        in_specs=[x_spec, g_spec, w_spec, pl.BlockSpec((d, nab), lambda i, j: (0, 0))],
        out_specs=[o_spec, pl.BlockSpec((tm, nab), lambda i, j: (i, 0))],
        out_shape=[jax.ShapeDtypeStruct((m, n), BF16), jax.ShapeDtypeStruct((m, nab), F32)],
        scratch_shapes=scratch, compiler_params=_params("parallel", "arbitrary"), name="inproj_ab",
    )(x2, gain, w, w_ab)


def _outproj_kernel(ma_ref, mb_ref, wa_ref, wb_ref, g_ref, x_ref, o_ref):
    y = _dot(ma_ref[...], wa_ref[...]) + _dot(mb_ref[...], wb_ref[...])
    ms = jnp.mean(y * y, axis=-1, keepdims=True)
    o_ref[...] = x_ref[...] + y * lax.rsqrt(ms + RMS_EPS) * g_ref[...]


def _outproj(mix_a, mix_b, w_a, w_b, gain, x2, *, tm):
    m, d = x2.shape
    k = mix_a.shape[1]
    return pl.pallas_call(
        _outproj_kernel, grid=(m // tm,),
        in_specs=[pl.BlockSpec((tm, k), lambda i: (i, 0)), pl.BlockSpec((tm, k), lambda i: (i, 0)),
                  pl.BlockSpec((k, d), lambda i: (0, 0)), pl.BlockSpec((k, d), lambda i: (0, 0)),
                  pl.BlockSpec((1, d), lambda i: (0, 0)), pl.BlockSpec((tm, d), lambda i: (i, 0))],
        out_specs=pl.BlockSpec((tm, d), lambda i: (i, 0)),
        out_shape=jax.ShapeDtypeStruct((m, d), F32),
        compiler_params=_params("parallel"), name="outproj",
    )(mix_a, mix_b, w_a, w_b, gain, x2)


def _hgrn_constants(c):
    n = c // 2
    t = np.arange(n)
    masks = [np.eye(n, dtype=np.float32)]
    b = 1
    while b < n:
        mid = (t // (2 * b)) * (2 * b) + b
        same = (t[:, None] // (2 * b)) == (t[None, :] // (2 * b))
        masks.append((same & (t[:, None] >= mid[:, None]) & (t[None, :] < mid[:, None])).astype(np.float32))
        b *= 2
    rows = np.arange(c)
    tri = (rows[None, :] <= rows[:, None]).astype(np.float32)
    return tri, np.stack(masks, axis=0)


HG_PAD = 8


def _hgrn_kernel(q_ref, f_ref, i_ref, z_ref, lbl_ref, gain_ref, tri_ref, mask_ref, o_ref,
                 st_scr, a_scr, sh_scr, *, layer_j, c):
    n = c // 2

    @pl.when(pl.program_id(2) == 0)
    def _():
        st_scr[...] = jnp.zeros_like(st_scr)

    logits = lbl_ref[...]
    e = jnp.exp(logits - jnp.max(logits, axis=0, keepdims=True))
    p = e / jnp.sum(e, axis=0, keepdims=True)
    lb = jnp.sum(p[0:layer_j + 1], axis=0, keepdims=True) - p[0:1]

    q = _silu(q_ref[0].astype(F32))
    sig = _sigmoid(f_ref[0].astype(F32))
    g = jnp.log(jnp.maximum(lb + (1.0 - lb) * sig, GATE_FLOOR))
    k = (1.0 - lb) * (1.0 - sig)
    vb = i_ref[0]
    qb = q.astype(BF16)
    kb = k.astype(BF16)

    a = _dot_const3(tri_ref[...], g)
    zeros_pad = jnp.zeros((HG_PAD, HEAD), F32)
    a_scr[0:HG_PAD, :] = zeros_pad
    a_scr[HG_PAD:HG_PAD + c, :] = a

    def seg_rows(b, first):
        return jnp.concatenate([jnp.broadcast_to(a_scr[pl.ds(first + j * b, 1), :], (b, HEAD))
                                for j in range(c // b)], axis=0)

    def shifted(x, slot):
        sh_scr[slot, 0:HG_PAD, :] = zeros_pad
        sh_scr[slot, HG_PAD + c:HG_PAD + c + HG_PAD, :] = zeros_pad
        sh_scr[slot, HG_PAD:HG_PAD + c, :] = x
        return lambda d: sh_scr[slot, pl.ds(HG_PAD + d, c), :]

    row = lax.broadcasted_iota(jnp.int32, (c, HEAD), 0)
    g_at = shifted(g, 0)
    odd = (row & 1) == 1
    up = {1: g, 2: g + jnp.where(odd, g_at(-1), 0.0)}
    lo = {1: None, 2: jnp.where(odd, 0.0, g_at(1))}
    up2_at = shifted(up[2], 1)
    suffix2_at = shifted(lo[2] + g, 2)
    pos = row & 3
    up[4] = up[2] + jnp.where(pos == 2, up2_at(-1), jnp.where(pos == 3, up2_at(-2), 0.0))
    lo[4] = lo[2] + jnp.where(pos == 0, suffix2_at(2), jnp.where(pos == 1, suffix2_at(1), 0.0))
    b = 8
    while b <= c:
        up[b] = a - seg_rows(b, HG_PAD - 1)
        lo[b] = seg_rows(b, HG_PAD - 1 + b) - a
        b *= 2

    def decayed(b):
        qd = (q * jnp.exp(up[b])).astype(BF16)
        kd = kb if lo[b] is None else (k * jnp.exp(lo[b])).astype(BF16)
        return qd, kd

    halves = (slice(0, n), slice(n, c))
    scores = [mask_ref[0] * _dot_nt(qb[h], kb[h]) for h in halves]
    b, lv = 1, 1
    while b < n:
        qd, kd = decayed(b)
        scores = [s + mask_ref[lv] * _dot_nt(qd[h], kd[h]) for s, h in zip(scores, halves)]
        b, lv = 2 * b, lv + 1
    qd, kd = decayed(n)
    cross = _dot_nt(qd[halves[1]], kd[halves[0]])
    o_top = _dot(scores[0].astype(BF16), vb[halves[0]])
    o_bot = _dot(jnp.concatenate([cross, scores[1]], axis=1).astype(BF16), vb)
    qd, kd = decayed(c)
    st = st_scr[...]
    o = jnp.concatenate([o_top, o_bot], axis=0) + _dot_nt(qd, st.astype(BF16))
    st_scr[...] = jnp.exp(a[c - 1:c, :]) * st + lax.dot_general(vb, kd, (((0,), (0,)), ((), ())),
                                                               preferred_element_type=F32)

    ms = jnp.mean(o * o, axis=-1, keepdims=True)
    y = o * lax.rsqrt(ms + RMS_EPS) * gain_ref[...]
    o_ref[0] = (y * _silu(z_ref[0].astype(F32))).astype(o_ref.dtype)


def _hgrn(proj, lb_logits, gain, layer_j, *, c):
    b, l, _ = proj.shape
    tri, masks = _hgrn_constants(c)
    n_even = lb_logits.shape[0]

    def col(seg):
        return pl.BlockSpec((1, c, HEAD), lambda bi, h, t, seg=seg: (bi, t, seg * N_HEADS + h))

    kern = functools.partial(_hgrn_kernel, layer_j=layer_j, c=c)
    return pl.pallas_call(
        kern, grid=(b, N_HEADS, l // c),
        in_specs=[col(0), col(1), col(2), col(3),
                  pl.BlockSpec((n_even, HEAD), lambda bi, h, t: (0, h)),
                  pl.BlockSpec((1, HEAD), lambda bi, h, t: (0, 0)),
                  pl.BlockSpec(tri.shape, lambda bi, h, t: (0, 0)),
                  pl.BlockSpec(masks.shape, lambda bi, h, t: (0, 0, 0))],
        out_specs=pl.BlockSpec((1, c, HEAD), lambda bi, h, t: (bi, t, h)),
        out_shape=jax.ShapeDtypeStruct((b, l, BRANCH), BF16),
        scratch_shapes=[pltpu.VMEM((HEAD, HEAD), F32), pltpu.VMEM((HG_PAD + c, HEAD), F32),
                        pltpu.VMEM((3, c + 2 * HG_PAD, HEAD), F32)],
        compiler_params=_params("parallel", "parallel", "arbitrary"), name="hgrn2",
    )(proj, proj, proj, proj, lb_logits, gain, jnp.asarray(tri, BF16), jnp.asarray(masks))


def _diff_kernel(q_ref, k_ref, v_ref, z_ref, lamv_ref, gain_ref, o_ref, vt_scr, sa_scr, sb_scr, *, lam_init, tq, tk):
    qi = pl.program_id(2)
    n_trips = k_ref.shape[1] // tk

    @pl.when(qi == 0)
    def _():
        def transpose_block(i, carry):
            rows = pl.ds(pl.multiple_of(i * tk, tk), tk)
            vt_scr[i] = v_ref[0, rows, :].astype(F32).T.astype(BF16)
            return carry
        lax.fori_loop(0, n_trips, transpose_block, 0)

    lv = lamv_ref[...]
    d01 = jnp.sum(jnp.sum(lv[0:1] * lv[1:2], axis=-1, keepdims=True), axis=0, keepdims=True)
    d23 = jnp.sum(jnp.sum(lv[2:3] * lv[3:4], axis=-1, keepdims=True), axis=0, keepdims=True)
    lam = jnp.exp(d01) - jnp.exp(d23) + lam_init

    q = q_ref[0].astype(F32) * (DF_DH ** -0.5 * math.log2(math.e))
    lane = lax.broadcasted_iota(jnp.int32, q.shape, 1)
    zero = jnp.zeros_like(q)
    halves = (jnp.where(lane < DF_DH, q, zero).astype(BF16), jnp.where(lane >= DF_DH, q, zero).astype(BF16))

    def scores(t, s_buf):
        k = k_ref[0, pl.ds(pl.multiple_of(t * tk, tk), tk), :]
        tops = []
        for hf, qh in enumerate(halves):
            s = _dot_nt(k, qh)
            s_buf[hf] = s
            tops.append(jnp.max(s, axis=0, keepdims=True))
        return tuple(tops)

    def absorb(t, s_buf, state, tops, key_limit=None):
        vt = vt_scr[t]
        out = []
        for hf in range(2):
            m, l, acc = state[hf]
            s = s_buf[hf]
            top = tops[hf]
            if key_limit is not None:
                r = lax.broadcasted_iota(jnp.int32, (tk, tq), 0)
                cc = lax.broadcasted_iota(jnp.int32, (tk, tq), 1)
                s = jnp.where(r - cc <= key_limit, s, NEG)
                top = jnp.max(s, axis=0, keepdims=True)
            m_new = jnp.maximum(m, top)
            alpha = jnp.exp2(m - m_new)
            p = jnp.exp2(s - m_new)
            l = alpha * l + jnp.sum(p, axis=0, keepdims=True)
            acc = alpha * acc + _dot(vt, p.astype(BF16))
            out.append((m_new, l, acc))
        return tuple(out)

    n_full = (qi * tq) // tk
    state = tuple((jnp.full((1, tq), NEG, F32), jnp.zeros((1, tq), F32), jnp.zeros((HEAD, tq), F32))
                  for _ in halves)
    tops = scores(0, sa_scr)

    def pair(i, carry):
        state, tops_a = carry
        t = 2 * i
        tops_b = scores(t + 1, sb_scr)
        state = absorb(t, sa_scr, state, tops_a)
        tops_a = scores(t + 2, sa_scr)
        state = absorb(t + 1, sb_scr, state, tops_b)
        return state, tops_a

    state, tops = lax.fori_loop(0, n_full // 2, pair, (state, tops))
    key_limit = qi * tq - n_full * tk

    def finish(state):
        (_, l1, acc1), (_, l2, acc2) = state
        o = (acc1 / l1 - lam * (acc2 / l2)).T
        ms = jnp.mean(o * o, axis=-1, keepdims=True)
        y = o * lax.rsqrt(ms + RMS_EPS) * gain_ref[...]
        o_ref[0] = (y * (1.0 - lam_init) * _silu(z_ref[0].astype(F32))).astype(o_ref.dtype)

    @pl.when(n_full % 2 == 0)
    def _():
        finish(absorb(n_full, sa_scr, state, tops, key_limit=key_limit))

    @pl.when(n_full % 2 == 1)
    def _():
        tops_b = scores(n_full, sb_scr)
        mid = absorb(n_full - 1, sa_scr, state, tops)
        finish(absorb(n_full, sb_scr, mid, tops_b, key_limit=key_limit))


def _diff_attn(proj, lam_vec, gain, lam_init, *, tq):
    b, l, _ = proj.shape

    def qcol(seg):
        return pl.BlockSpec((1, tq, HEAD), lambda bi, h, t, seg=seg: (bi, t, seg * N_HEADS + h))

    def kvcol(seg):
        return pl.BlockSpec((1, l, HEAD), lambda bi, h, t, seg=seg: (bi, 0, seg * N_HEADS + h))

    tk = min(DIFF_TRIP, l)
    kern = functools.partial(_diff_kernel, lam_init=lam_init, tq=tq, tk=tk)
    return pl.pallas_call(
        kern, grid=(b, N_HEADS, l // tq),
        in_specs=[qcol(4), kvcol(5), kvcol(6), qcol(7),
                  pl.BlockSpec(lam_vec.shape, lambda bi, h, t: (0, 0)),
                  pl.BlockSpec((1, HEAD), lambda bi, h, t: (0, 0))],
        out_specs=pl.BlockSpec((1, tq, HEAD), lambda bi, h, t: (bi, t, h)),
        out_shape=jax.ShapeDtypeStruct((b, l, BRANCH), BF16),
        scratch_shapes=[pltpu.VMEM((l // tk, HEAD, tk), BF16), pltpu.VMEM((2, tk, tq), F32), pltpu.VMEM((2, tk, tq), F32)],
        compiler_params=_params("parallel", "parallel", "arbitrary"), name="diff_attn",
    )(proj, proj, proj, proj, lam_vec, gain)


def _sb_kernel(q_ref, k_ref, v_ref, z_ref, o_ref, *, tq):
    qi = pl.program_id(2)
    q = (q_ref[0].astype(F32) * (HEAD ** -0.5)).astype(BF16)
    r = lax.broadcasted_iota(jnp.int32, (tq, tq), 0)
    cc = lax.broadcasted_iota(jnp.int32, (tq, tq), 1)
    later = (r > cc).astype(BF16)

    def local(kj, masked):
        rows = pl.ds(pl.multiple_of(kj * tq, tq), tq)
        zz = _dot_nt(q, k_ref[0, rows, :])
        sp = _softplus(zz)
        log_rest = -sp
        if masked:
            log_rest = jnp.where(cc < r, log_rest, 0.0)
        hi = log_rest.astype(BF16)
        lo = (log_rest - hi.astype(F32)).astype(BF16)
        inner = zz - sp + (_dot(hi, later) + _dot(lo, later))
        return inner, jnp.sum(log_rest, axis=-1, keepdims=True), rows

    def weigh(part, run, masked):
        inner, total, rows = part
        w = jnp.exp(inner + run)
        if masked:
            w = jnp.where(cc < r, w, 0.0)
        return _dot(w.astype(BF16), v_ref[0, rows, :]), run + total

    diag = local(qi, True)
    prev = local(jnp.maximum(qi - 1, 0), False)
    acc, run = weigh(diag, jnp.zeros((tq, 1), F32), True)
    has_prev = (qi > 0).astype(F32)
    pv, run_prev = weigh(prev, run, False)
    acc = acc + has_prev * pv
    run = run + has_prev * (run_prev - run)

    def cond(carry):
        kj, go, _, _ = carry
        return jnp.logical_and(kj >= 0, go)

    def body(carry):
        kj, _, run, acc = carry
        pv, run = weigh(local(kj, False), run, False)
        return kj - 1, jnp.max(run) > -SB_SKIP, run, acc + pv

    _, _, _, acc = lax.while_loop(cond, body, (qi - 2, jnp.max(run) > -SB_SKIP, run, acc))
    o_ref[0] = (acc * _silu(z_ref[0].astype(F32))).astype(o_ref.dtype)


def _stick_breaking(proj, *, tq):
    b, l, _ = proj.shape

    def qcol(seg):
        return pl.BlockSpec((1, tq, HEAD), lambda bi, h, t, seg=seg: (bi, t, seg * N_HEADS + h))

    def kvcol(seg):
        return pl.BlockSpec((1, l, HEAD), lambda bi, h, t, seg=seg: (bi, 0, seg * N_HEADS + h))

    return pl.pallas_call(
        functools.partial(_sb_kernel, tq=tq), grid=(b, N_HEADS, l // tq),
        in_specs=[qcol(4), kvcol(5), kvcol(6), qcol(7)],
        out_specs=pl.BlockSpec((1, tq, HEAD), lambda bi, h, t: (bi, t, h)),
        out_shape=jax.ShapeDtypeStruct((b, l, BRANCH), BF16),
        compiler_params=_params("parallel", "parallel", "arbitrary"), name="stick_breaking",
    )(proj, proj, proj, proj)


def _split2(x):
    hi = x.astype(BF16)
    return hi, (x - hi.astype(F32)).astype(BF16)


def _dot3(a, b):
    a_hi, a_lo = _split2(a)
    b_hi, b_lo = _split2(b)
    return _dot(a_hi, b_hi) + (_dot(a_hi, b_lo) + _dot(a_lo, b_hi))


def _dot_const3(const_bf16, x):
    hi = x.astype(BF16)
    r1 = x - hi.astype(F32)
    mid = r1.astype(BF16)
    lo = (r1 - mid.astype(F32)).astype(BF16)
    return _dot(const_bf16, hi) + (_dot(const_bf16, mid) + _dot(const_bf16, lo))


def _each(fn, *lists):
    return [fn(*args) for args in zip(*lists)]


def _unit_lower_inverse(lws, eye, blockdiag):
    ld = [lw * blockdiag for lw in lws]
    lo = _each(lambda a, b: a - b, lws, ld)
    td = [eye - x for x in ld]
    p = _each(_dot3, ld, ld)
    n_factors = int(math.log2(INV_BLOCK)) - 1
    for i in range(n_factors):
        td = _each(lambda a, b: a + _dot3(a, b), td, p)
        if i < n_factors - 1:
            p = _each(_dot3, p, p)
    n = _each(_dot3, td, lo)
    n2 = _each(_dot3, n, n)
    y = [eye - x for x in n]
    y = _each(lambda a, b: a + _dot3(a, b), y, n2)
    return _each(_dot3, y, td)


def _gdn_prep_kernel(q_ref, k_ref, v_ref, qh_ref, kh_ref, vh_ref, ab_ref, wq_ref, wk_ref, wv_ref, alog_ref, dtb_ref,
                     u_ref, w_ref, qg_ref, kd_ref, at_ref, gl_ref, ext_scr, *, n_chunks, c):
    h = pl.program_id(1)
    t = pl.program_id(2)
    t_blk = n_chunks * c

    def conv(x_ref, halo_ref, w_ref, idx):
        x = x_ref[0].astype(F32)
        ext_scr[idx, 0:GD_HALO, :] = jnp.where(t > 0, halo_ref[0].astype(F32), 0.0)
        ext_scr[idx, GD_HALO:GD_HALO + t_blk, :] = x
        w = w_ref[...]
        y = w[GD_CONV - 1:GD_CONV] * x
        for tap in range(GD_CONV - 1):
            y = y + w[tap:tap + 1] * ext_scr[idx, pl.ds(GD_HALO - (GD_CONV - 1) + tap, t_blk), :]
        return _silu(y)

    def l2n(x):
        return x * lax.rsqrt(jnp.sum(x * x, axis=-1, keepdims=True) + RMS_EPS)

    q_all = l2n(conv(q_ref, qh_ref, wq_ref, 0)) * (HEAD ** -0.5)
    k_all = l2n(conv(k_ref, kh_ref, wk_ref, 1))
    v_all = conv(v_ref, vh_ref, wv_ref, 2)

    ab = ab_ref[0]
    lane = lax.broadcasted_iota(jnp.int32, ab.shape, 1)
    g_lanes = -jnp.exp(alog_ref[...]) * _softplus(ab + dtb_ref[...])
    g_all = jnp.sum(jnp.where(lane == h, g_lanes, 0.0), axis=-1, keepdims=True)
    beta_all = jnp.sum(jnp.where(lane == h + N_HEADS, _sigmoid(ab), 0.0), axis=-1, keepdims=True)

    r = lax.broadcasted_iota(jnp.int32, (c, c), 0)
    cc = lax.broadcasted_iota(jnp.int32, (c, c), 1)
    incl = cc <= r
    strict = cc < r
    tri = incl.astype(BF16)
    eye = (cc == r).astype(F32)
    blockdiag = ((r // INV_BLOCK) == (cc // INV_BLOCK)).astype(F32)
    r2 = lax.broadcasted_iota(jnp.int32, (c, HEAD), 0)
    c2 = lax.broadcasted_iota(jnp.int32, (c, HEAD), 1)
    strict_wide = jnp.logical_and(c2 < r2, c2 < c).astype(F32)

    chunks = [slice(ci * c, (ci + 1) * c) for ci in range(n_chunks)]
    q = [q_all[sl] for sl in chunks]
    k = [k_all[sl] for sl in chunks]
    v = [v_all[sl] for sl in chunks]
    beta = [beta_all[sl] for sl in chunks]
    g_wide = [jnp.broadcast_to(g_all[sl], (c, HEAD)) for sl in chunks]
    sums = [_dot_const3(tri, jnp.concatenate([gw, gw * strict_wide], axis=1)) for gw in g_wide]
    gc = [s[:, 0:HEAD] for s in sums]
    decay = [jnp.exp(jnp.where(incl, s[:, HEAD:HEAD + c], NEG)) for s in sums]
    kb = _each(lambda a, b: a * b, k, beta)
    kbf = [x.astype(BF16) for x in k]
    lw = _each(lambda a, b, d: jnp.where(strict, _dot_nt(a.astype(BF16), b) * d, 0.0), kb, kbf, decay)
    tinv = [x.astype(BF16) for x in _unit_lower_inverse(lw, eye, blockdiag)]
    egc = [jnp.exp(x) for x in gc]
    u = _each(lambda ti, a, b: _dot(ti, (a * b).astype(BF16)), tinv, v, beta)
    w = _each(lambda ti, a, e: _dot(ti, (a * e).astype(BF16)), tinv, kb, egc)
    attn = _each(lambda a, b, d: _dot_nt(a.astype(BF16), b) * d, q, kbf, decay)
    for ci, sl in enumerate(chunks):
        gc_last = gc[ci][c - 1:c, :]
        u_ref[0, 0, sl, :] = u[ci]
        w_ref[0, 0, sl, :] = w[ci].astype(BF16)
        qg_ref[0, 0, sl, :] = (q[ci] * egc[ci]).astype(BF16)
        kd_ref[0, 0, sl, :] = (k[ci] * jnp.exp(gc_last - gc[ci])).astype(BF16)
        at_ref[0, 0, sl, :] = attn[ci].astype(BF16)
        gl_ref[0, 0, ci] = jnp.exp(gc_last)


def _gdn_scan_kernel(u_ref, w_ref, qg_ref, kd_ref, at_ref, gl_ref, z_ref, gain_ref, o_ref, s_scr, *, n_chunks, c):
    @pl.when(pl.program_id(1) == 0)
    def _():
        s_scr[...] = jnp.zeros_like(s_scr)

    heads = range(N_HEADS)
    gain = gain_ref[...]
    s = [s_scr[h] for h in heads]
    for ci in range(n_chunks):
        rows = pl.ds(ci * c, c)
        sb = [x.astype(BF16) for x in s]
        v_new = [u_ref[0, h, rows, :] - _dot(w_ref[0, h, rows, :], sb[h]) for h in heads]
        vb = [x.astype(BF16) for x in v_new]
        o = [_dot(qg_ref[0, h, rows, :], sb[h]) + _dot(at_ref[0, h, rows, :], vb[h]) for h in heads]
        s = [gl_ref[0, h, ci] * s[h]
             + lax.dot_general(kd_ref[0, h, rows, :], vb[h], (((0,), (0,)), ((), ())), preferred_element_type=F32)
             for h in heads]
        for h in heads:
            cols = slice(h * HEAD, (h + 1) * HEAD)
            ms = jnp.mean(o[h] * o[h], axis=-1, keepdims=True)
            y = o[h] * lax.rsqrt(ms + RMS_EPS) * gain
            o_ref[0, rows, cols] = (y * _silu(z_ref[0, rows, cols].astype(F32))).astype(o_ref.dtype)
    for h in heads:
        s_scr[h] = s[h]


def _gdn(proj, ab, conv_w, a_log, dt_bias, gain, *, t_prep, t_scan, c):
    b, l, _ = proj.shape
    halo_blocks = t_prep // GD_HALO

    def col(seg):
        return pl.BlockSpec((1, t_prep, HEAD), lambda bi, h, t, seg=seg: (bi, t, seg * N_HEADS + h))

    def halo(seg):
        return pl.BlockSpec((1, GD_HALO, HEAD),
                            lambda bi, h, t, seg=seg: (bi, jnp.maximum(t * halo_blocks - 1, 0), seg * N_HEADS + h))

    def wcol(seg):
        return pl.BlockSpec((GD_CONV, HEAD), lambda bi, h, t, seg=seg: (0, seg * N_HEADS + h))

    one = pl.BlockSpec((1, HEAD), lambda bi, h, t: (0, 0))

    def per_head(width):
        return pl.BlockSpec((1, 1, t_prep, width), lambda bi, h, t: (bi, h, t, 0))

    n_prep = t_prep // c
    u, w, qg, kd, at, gl = pl.pallas_call(
        functools.partial(_gdn_prep_kernel, n_chunks=n_prep, c=c), grid=(b, N_HEADS, l // t_prep),
        in_specs=[col(0), col(1), col(2), halo(0), halo(1), halo(2),
                  pl.BlockSpec((1, t_prep, HEAD), lambda bi, h, t: (bi, t, 0)),
                  wcol(0), wcol(1), wcol(2), one, one],
        out_specs=[per_head(HEAD), per_head(HEAD), per_head(HEAD), per_head(HEAD), per_head(c),
                   pl.BlockSpec((1, 1, n_prep, 1, HEAD), lambda bi, h, t: (bi, h, t, 0, 0))],
        out_shape=[jax.ShapeDtypeStruct((b, N_HEADS, l, HEAD), F32),
                   jax.ShapeDtypeStruct((b, N_HEADS, l, HEAD), BF16),
                   jax.ShapeDtypeStruct((b, N_HEADS, l, HEAD), BF16),
                   jax.ShapeDtypeStruct((b, N_HEADS, l, HEAD), BF16),
                   jax.ShapeDtypeStruct((b, N_HEADS, l, c), BF16),
                   jax.ShapeDtypeStruct((b, N_HEADS, l // c, 1, HEAD), F32)],
        scratch_shapes=[pltpu.VMEM((3, t_prep + GD_HALO, HEAD), F32)],
        compiler_params=_params("parallel", "parallel", "parallel"), name="gdn_prep",
    )(proj, proj, proj, proj, proj, proj, ab, conv_w, conv_w, conv_w, a_log, dt_bias)

    def all_heads(width):
        return pl.BlockSpec((1, N_HEADS, t_scan, width), lambda bi, t: (bi, 0, t, 0))

    n_scan = t_scan // c
    return pl.pallas_call(
        functools.partial(_gdn_scan_kernel, n_chunks=n_scan, c=c), grid=(b, l // t_scan),
        in_specs=[all_heads(HEAD), all_heads(HEAD), all_heads(HEAD), all_heads(HEAD), all_heads(c),
                  pl.BlockSpec((1, N_HEADS, n_scan, 1, HEAD), lambda bi, t: (bi, 0, t, 0, 0)),
                  pl.BlockSpec((1, t_scan, BRANCH), lambda bi, t: (bi, t, 3)),
                  pl.BlockSpec((1, HEAD), lambda bi, t: (0, 0))],
        out_specs=pl.BlockSpec((1, t_scan, BRANCH), lambda bi, t: (bi, t, 0)),
        out_shape=jax.ShapeDtypeStruct((b, l, BRANCH), BF16),
        scratch_shapes=[pltpu.VMEM((N_HEADS, HEAD, HEAD), F32)],
        compiler_params=_params("parallel", "arbitrary"), name="gdn_scan",
    )(u, w, qg, kd, at, gl, proj, gain)


def _pad_lanes(vec):
    return jnp.zeros((1, HEAD), F32).at[0, :vec.shape[0]].set(vec.astype(F32))


def kernel(x, norm_pre, norm_post, ev_w_in, ev_w_out, hg_lb_logits, hg_norm, df_lambda, df_norm,
           od_w_in, od_w_out, gd_conv, gd_a_log, gd_dt_bias, gd_norm):
    p = dict(norm_pre=norm_pre, norm_post=norm_post, ev_w_in=ev_w_in, ev_w_out=ev_w_out,
             hg_lb_logits=hg_lb_logits, hg_norm=hg_norm, df_lambda=df_lambda, df_norm=df_norm,
             od_w_in=od_w_in, od_w_out=od_w_out, gd_conv=gd_conv, gd_a_log=gd_a_log,
             gd_dt_bias=gd_dt_bias, gd_norm=gd_norm)
    return _run_layers(x, p, range(norm_pre.shape[0]))


def _run_layers(x, p, layers):
    norm_pre, norm_post = p["norm_pre"], p["norm_post"]
    ev_w_in, ev_w_out, od_w_in, od_w_out = p["ev_w_in"], p["ev_w_out"], p["od_w_in"], p["od_w_out"]
    hg_lb_logits, hg_norm, df_lambda, df_norm = p["hg_lb_logits"], p["hg_norm"], p["df_lambda"], p["df_norm"]
    gd_conv, gd_a_log, gd_dt_bias, gd_norm = p["gd_conv"], p["gd_a_log"], p["gd_dt_bias"], p["gd_norm"]
    b, l, d = x.shape
    m = b * l
    tm_in = min(1024, m)
    tm_out = min(512, m)
    t_rec = min(REC_BLOCK, l)
    t_att = min(ATT_BLOCK, l)
    x2 = x.reshape(m, d)
    for layer in layers:
        j = layer // 2
        g_pre = norm_pre[layer].reshape(1, d)
        g_post = norm_post[layer].reshape(1, d)
        if layer % 2 == 0:
            proj = _inproj(x2, g_pre, ev_w_in[j].astype(BF16), tm=tm_in, tn=1024).reshape(b, l, -1)
            lam_init = 0.8 - 0.6 * math.exp(-0.3 * layer)
            mix_a = _hgrn(proj, hg_lb_logits.astype(F32), hg_norm[j].reshape(1, HEAD), j, c=min(HG_CHUNK, l))
            mix_b = _diff_attn(proj, df_lambda[j].astype(F32), df_norm[j].reshape(1, HEAD), lam_init,
                               tq=min(DIFF_Q_BLOCK, l))
            w_out = ev_w_out[j].astype(BF16)
        else:
            w = od_w_in[j]
            w_main = jnp.concatenate([w[:, :4 * BRANCH], w[:, 4 * BRANCH + 2 * N_HEADS:]], axis=1).astype(BF16)
            w_ab = jnp.pad(w[:, 4 * BRANCH:4 * BRANCH + 2 * N_HEADS], ((0, 0), (0, HEAD - 2 * N_HEADS))).astype(BF16)
            proj, ab = _inproj(x2, g_pre, w_main, w_ab, tm=tm_in, tn=1024)
            proj = proj.reshape(b, l, -1)
            mix_a = _gdn(proj, ab.reshape(b, l, HEAD), gd_conv[j].astype(F32), _pad_lanes(gd_a_log[j]),
                         _pad_lanes(gd_dt_bias[j]), gd_norm[j].reshape(1, HEAD), t_prep=min(GD_PREP_BLOCK, l), t_scan=t_rec, c=CHUNK)
            mix_b = _stick_breaking(proj, tq=t_att)
            w_out = od_w_out[j].astype(BF16)
        x2 = _outproj(mix_a.reshape(m, BRANCH), mix_b.reshape(m, BRANCH), w_out[:BRANCH], w_out[BRANCH:],
                      g_post, x2, tm=tm_out)
    return x2.reshape(b, l, d)
```

```python
import functools
import math

import numpy as np
import jax
import jax.numpy as jnp
from jax import lax
from jax.experimental import pallas as pl
from jax.experimental.pallas import tpu as pltpu

F32 = jnp.float32
BF16 = jnp.bfloat16
HIGHEST = lax.Precision.HIGHEST

D_MODEL = 2048
BRANCH = D_MODEL // 2
N_HEADS = 8
HEAD = BRANCH // N_HEADS
DF_DH = HEAD // 2
GD_CONV = 4
RMS_EPS = 1e-6
NEG = -1e30
GATE_FLOOR = 1e-20
SB_SKIP = 104.0

V7X_VMEM_LIMIT = 56 * 1024 * 1024

CHUNK = 64
HG_CHUNK = 256
HG_HEADS_PER_STEP = 4
REC_BLOCK = 256
ATT_BLOCK = 256
SB_HEADS_PER_STEP = 4
DIFF_Q_BLOCK = 512
DIFF_TRIP = 512
INV_BLOCK = 16
GD_PREP_BLOCK = 512
GD_HALO = 16


def _params(*sem):
    return pltpu.CompilerParams(dimension_semantics=sem, vmem_limit_bytes=V7X_VMEM_LIMIT)


def _sigmoid(x):
    return 0.5 * jnp.tanh(0.5 * x) + 0.5


def _silu(x):
    return x * _sigmoid(x)


def _softplus(x):
    return jnp.maximum(x, 0.0) + jnp.log(1.0 + jnp.exp(-jnp.abs(x)))


def _dot(a, b):
    return jnp.dot(a, b, preferred_element_type=F32)


def _dot_nt(a, b):
    return lax.dot_general(a, b, (((1,), (1,)), ((), ())), preferred_element_type=F32)


def _dot_exact(a, b):
    return jnp.dot(a, b, precision=HIGHEST, preferred_element_type=F32)


def _inproj_kernel(x_ref, g_ref, w_ref, o_ref, h_scr):
    @pl.when(pl.program_id(1) == 0)
    def _():
        x = x_ref[...]
        ms = jnp.mean(x * x, axis=-1, keepdims=True)
        h_scr[...] = (x * lax.rsqrt(ms + RMS_EPS) * g_ref[...]).astype(BF16)

    o_ref[...] = _dot(h_scr[...], w_ref[...]).astype(o_ref.dtype)


def _inproj_ab_kernel(x_ref, g_ref, w_ref, wab_ref, o_ref, oab_ref, h_scr):
    @pl.when(pl.program_id(1) == 0)
    def _():
        x = x_ref[...]
        ms = jnp.mean(x * x, axis=-1, keepdims=True)
        h_scr[...] = (x * lax.rsqrt(ms + RMS_EPS) * g_ref[...]).astype(BF16)
        oab_ref[...] = _dot(h_scr[...], wab_ref[...])

    o_ref[...] = _dot(h_scr[...], w_ref[...]).astype(o_ref.dtype)


def _inproj(x2, gain, w, w_ab=None, *, tm, tn):
    m, d = x2.shape
    n = w.shape[1]
    grid = (m // tm, n // tn)
    x_spec = pl.BlockSpec((tm, d), lambda i, j: (i, 0))
    g_spec = pl.BlockSpec((1, d), lambda i, j: (0, 0))
    w_spec = pl.BlockSpec((d, tn), lambda i, j: (0, j))
    o_spec = pl.BlockSpec((tm, tn), lambda i, j: (i, j))
    scratch = [pltpu.VMEM((tm, d), BF16)]
    if w_ab is None:
        return pl.pallas_call(
            _inproj_kernel, grid=grid, in_specs=[x_spec, g_spec, w_spec], out_specs=o_spec,
            out_shape=jax.ShapeDtypeStruct((m, n), BF16), scratch_shapes=scratch,
            compiler_params=_params("parallel", "arbitrary"), name="inproj",
        )(x2, gain, w)
    nab = w_ab.shape[1]
    return pl.pallas_call(
        _inproj_ab_kernel, grid=grid,
        in_specs=[x_spec, g_spec, w_spec, pl.BlockSpec((d, nab), lambda i, j: (0, 0))],
        out_specs=[o_spec, pl.BlockSpec((tm, nab), lambda i, j: (i, 0))],
        out_shape=[jax.ShapeDtypeStruct((m, n), BF16), jax.ShapeDtypeStruct((m, nab), F32)],
        scratch_shapes=scratch, compiler_params=_params("parallel", "arbitrary"), name="inproj_ab",
    )(x2, gain, w, w_ab)


def _outproj_kernel(ma_ref, mb_ref, wa_ref, wb_ref, g_ref, x_ref, o_ref):
    y = _dot(ma_ref[...], wa_ref[...]) + _dot(mb_ref[...], wb_ref[...])
    ms = jnp.mean(y * y, axis=-1, keepdims=True)
    o_ref[...] = x_ref[...] + y * lax.rsqrt(ms + RMS_EPS) * g_ref[...]


def _outproj(mix_a, mix_b, w_a, w_b, gain, x2, *, tm):
    m, d = x2.shape
    k = mix_a.shape[1]
    return pl.pallas_call(
        _outproj_kernel, grid=(m // tm,),
        in_specs=[pl.BlockSpec((tm, k), lambda i: (i, 0)), pl.BlockSpec((tm, k), lambda i: (i, 0)),
                  pl.BlockSpec((k, d), lambda i: (0, 0)), pl.BlockSpec((k, d), lambda i: (0, 0)),
                  pl.BlockSpec((1, d), lambda i: (0, 0)), pl.BlockSpec((tm, d), lambda i: (i, 0))],
        out_specs=pl.BlockSpec((tm, d), lambda i: (i, 0)),
        out_shape=jax.ShapeDtypeStruct((m, d), F32),
        compiler_params=_params("parallel"), name="outproj",
    )(mix_a, mix_b, w_a, w_b, gain, x2)


def _hgrn_constants(c):
    n = c // 2
    t = np.arange(n)
    masks = [np.eye(n, dtype=np.float32)]
    b = 1
    while b < n:
        mid = (t // (2 * b)) * (2 * b) + b
        same = (t[:, None] // (2 * b)) == (t[None, :] // (2 * b))
        masks.append((same & (t[:, None] >= mid[:, None]) & (t[None, :] < mid[:, None])).astype(np.float32))
        b *= 2
    rows = np.arange(c)
    tri = (rows[None, :] <= rows[:, None]).astype(np.float32)
    return tri, np.stack(masks, axis=0)


HG_PAD = 8


def _hgrn_kernel(q_ref, f_ref, i_ref, z_ref, lbl_ref, gain_ref, tri_ref, mask_ref, o_ref, *scratch,
                 layer_j, c, heads):
    @pl.when(pl.program_id(2) == 0)
    def _():
        for hd in range(heads):
            scratch[3 * hd][...] = jnp.zeros((HEAD, HEAD), F32)

    for hd in range(heads):
        _hgrn_head(slice(hd * HEAD, (hd + 1) * HEAD), q_ref, f_ref, i_ref, z_ref, lbl_ref, gain_ref, tri_ref,
                   mask_ref, o_ref, *scratch[3 * hd:3 * hd + 3], layer_j=layer_j, c=c)


def _hgrn_head(cols, q_ref, f_ref, i_ref, z_ref, lbl_ref, gain_ref, tri_ref, mask_ref, o_ref,
               st_scr, a_scr, sh_scr, *, layer_j, c):
    n = c // 2

    logits = lbl_ref[:, cols]
    e = jnp.exp(logits - jnp.max(logits, axis=0, keepdims=True))
    p = e / jnp.sum(e, axis=0, keepdims=True)
    lb = jnp.sum(p[0:layer_j + 1], axis=0, keepdims=True) - p[0:1]

    q = _silu(q_ref[0, :, cols].astype(F32))
    sig = _sigmoid(f_ref[0, :, cols].astype(F32))
    g = jnp.log(jnp.maximum(lb + (1.0 - lb) * sig, GATE_FLOOR))
    k = (1.0 - lb) * (1.0 - sig)
    vb = i_ref[0, :, cols]
    qb = q.astype(BF16)
    kb = k.astype(BF16)

    a = _dot_const3(tri_ref[...], g)
    zeros_pad = jnp.zeros((HG_PAD, HEAD), F32)
    a_scr[0:HG_PAD, :] = zeros_pad
    a_scr[HG_PAD:HG_PAD + c, :] = a

    def seg_rows(b, first):
        return jnp.concatenate([jnp.broadcast_to(a_scr[pl.ds(first + j * b, 1), :], (b, HEAD))
                                for j in range(c // b)], axis=0)

    def shifted(x, slot):
        sh_scr[slot, 0:HG_PAD, :] = zeros_pad
        sh_scr[slot, HG_PAD + c:HG_PAD + c + HG_PAD, :] = zeros_pad
        sh_scr[slot, HG_PAD:HG_PAD + c, :] = x
        return lambda d: sh_scr[slot, pl.ds(HG_PAD + d, c), :]

    row = lax.broadcasted_iota(jnp.int32, (c, HEAD), 0)
    g_at = shifted(g, 0)
    odd = (row & 1) == 1
    up = {1: g, 2: g + jnp.where(odd, g_at(-1), 0.0)}
    lo = {1: None, 2: jnp.where(odd, 0.0, g_at(1))}
    up2_at = shifted(up[2], 1)
    suffix2_at = shifted(lo[2] + g, 2)
    pos = row & 3
    up[4] = up[2] + jnp.where(pos == 2, up2_at(-1), jnp.where(pos == 3, up2_at(-2), 0.0))
    lo[4] = lo[2] + jnp.where(pos == 0, suffix2_at(2), jnp.where(pos == 1, suffix2_at(1), 0.0))
    b = 8
    while b <= c:
        up[b] = a - seg_rows(b, HG_PAD - 1)
        lo[b] = seg_rows(b, HG_PAD - 1 + b) - a
        b *= 2

    def decayed(b):
        qd = (q * jnp.exp(up[b])).astype(BF16)
        kd = kb if lo[b] is None else (k * jnp.exp(lo[b])).astype(BF16)
        return qd, kd

    halves = (slice(0, n), slice(n, c))
    scores = [mask_ref[0] * _dot_nt(qb[h], kb[h]) for h in halves]
    b, lv = 1, 1
    while b < n:
        qd, kd = decayed(b)
        scores = [s + mask_ref[lv] * _dot_nt(qd[h], kd[h]) for s, h in zip(scores, halves)]
        b, lv = 2 * b, lv + 1
    qd, kd = decayed(n)
    cross = _dot_nt(qd[halves[1]], kd[halves[0]])
    o_top = _dot(scores[0].astype(BF16), vb[halves[0]])
    o_bot = _dot(jnp.concatenate([cross, scores[1]], axis=1).astype(BF16), vb)
    qd, kd = decayed(c)
    st = st_scr[...]
    o = jnp.concatenate([o_top, o_bot], axis=0) + _dot_nt(qd, st.astype(BF16))
    st_scr[...] = jnp.exp(a[c - 1:c, :]) * st + lax.dot_general(vb, kd, (((0,), (0,)), ((), ())),
                                                               preferred_element_type=F32)

    ms = jnp.mean(o * o, axis=-1, keepdims=True)
    y = o * lax.rsqrt(ms + RMS_EPS) * gain_ref[...]
    o_ref[0, :, cols] = (y * _silu(z_ref[0, :, cols].astype(F32))).astype(o_ref.dtype)


def _hgrn(proj, lb_logits, gain, layer_j, *, c):
    b, l, _ = proj.shape
    tri, masks = _hgrn_constants(c)
    n_even = lb_logits.shape[0]
    heads = HG_HEADS_PER_STEP
    groups = N_HEADS // heads
    width = heads * HEAD

    def col(seg):
        return pl.BlockSpec((1, c, width), lambda bi, h, t, seg=seg: (bi, t, seg * groups + h))

    kern = functools.partial(_hgrn_kernel, layer_j=layer_j, c=c, heads=heads)
    per_head_scratch = [pltpu.VMEM((HEAD, HEAD), F32), pltpu.VMEM((HG_PAD + c, HEAD), F32),
                        pltpu.VMEM((3, c + 2 * HG_PAD, HEAD), F32)]
    return pl.pallas_call(
        kern, grid=(b, groups, l // c),
        in_specs=[col(0), col(1), col(2), col(3),
                  pl.BlockSpec((n_even, width), lambda bi, h, t: (0, h)),
                  pl.BlockSpec((1, HEAD), lambda bi, h, t: (0, 0)),
                  pl.BlockSpec(tri.shape, lambda bi, h, t: (0, 0)),
                  pl.BlockSpec(masks.shape, lambda bi, h, t: (0, 0, 0))],
        out_specs=pl.BlockSpec((1, c, width), lambda bi, h, t: (bi, t, h)),
        out_shape=jax.ShapeDtypeStruct((b, l, BRANCH), BF16),
        scratch_shapes=per_head_scratch * heads,
        compiler_params=_params("parallel", "parallel", "arbitrary"), name="hgrn2",
    )(proj, proj, proj, proj, lb_logits, gain, jnp.asarray(tri, BF16), jnp.asarray(masks))


def _diff_kernel(q_ref, k_ref, v_ref, z_ref, lamv_ref, gain_ref, o_ref, vt_scr, sa_scr, sb_scr, *, lam_init, tq, tk):
    qi = pl.program_id(2)
    n_trips = k_ref.shape[1] // tk

    @pl.when(qi == 0)
    def _():
        def transpose_block(i, carry):
            rows = pl.ds(pl.multiple_of(i * tk, tk), tk)
            vt_scr[i] = v_ref[0, rows, :].astype(F32).T.astype(BF16)
            return carry
        lax.fori_loop(0, n_trips, transpose_block, 0)

    lv = lamv_ref[...]
    d01 = jnp.sum(jnp.sum(lv[0:1] * lv[1:2], axis=-1, keepdims=True), axis=0, keepdims=True)
    d23 = jnp.sum(jnp.sum(lv[2:3] * lv[3:4], axis=-1, keepdims=True), axis=0, keepdims=True)
    lam = jnp.exp(d01) - jnp.exp(d23) + lam_init

    q = q_ref[0].astype(F32) * (DF_DH ** -0.5 * math.log2(math.e))
    lane = lax.broadcasted_iota(jnp.int32, q.shape, 1)
    zero = jnp.zeros_like(q)
    halves = (jnp.where(lane < DF_DH, q, zero).astype(BF16), jnp.where(lane >= DF_DH, q, zero).astype(BF16))

    def scores(t, s_buf, key_limit=None):
        k = k_ref[0, pl.ds(pl.multiple_of(t * tk, tk), tk), :]
        tops = []
        for hf, qh in enumerate(halves):
            s = _dot_nt(k, qh)
            if key_limit is not None:
                r = lax.broadcasted_iota(jnp.int32, (tk, tq), 0)
                cc = lax.broadcasted_iota(jnp.int32, (tk, tq), 1)
                s = jnp.where(r - cc <= key_limit, s, NEG)
            s_buf[hf] = s
            tops.append(jnp.max(s, axis=0, keepdims=True))
        return tuple(tops)

    def absorb(t, s_buf, state, tops):
        vt = vt_scr[t]
        out = []
        for hf in range(2):
            m, l, acc = state[hf]
            s = s_buf[hf]
            m_new = jnp.maximum(m, tops[hf])
            alpha = jnp.exp2(m - m_new)
            p = jnp.exp2(s - m_new)
            l = alpha * l + jnp.sum(p, axis=0, keepdims=True)
            acc = alpha * acc + _dot(vt, p.astype(BF16))
            out.append((m_new, l, acc))
        return tuple(out)

    n_full = (qi * tq) // tk
    state = tuple((jnp.full((1, tq), NEG, F32), jnp.zeros((1, tq), F32), jnp.zeros((HEAD, tq), F32))
                  for _ in halves)
    tops = scores(n_full, sa_scr, key_limit=qi * tq - n_full * tk)

    def visits(v, state, tops_a, n_pairs):
        for _ in range(n_pairs):
            tops_b = scores(n_full - (v + 1), sb_scr)
            state = absorb(n_full - v, sa_scr, state, tops_a)
            tops_a = scores(n_full - (v + 2), sa_scr)
            state = absorb(n_full - (v + 1), sb_scr, state, tops_b)
            v = v + 2
        return state, tops_a

    n_quads = n_full // 4
    state, tops = lax.fori_loop(0, n_quads, lambda i, c: visits(4 * i, c[0], c[1], 2), (state, tops))
    n_pairs = (n_full - 4 * n_quads) // 2
    state, tops = lax.fori_loop(0, n_pairs, lambda i, c: visits(4 * n_quads + 2 * i, c[0], c[1], 1),
                                (state, tops))

    def finish(state):
        (_, l1, acc1), (_, l2, acc2) = state
        o = (acc1 / l1 - lam * (acc2 / l2)).T
        ms = jnp.mean(o * o, axis=-1, keepdims=True)
        y = o * lax.rsqrt(ms + RMS_EPS) * gain_ref[...]
        o_ref[0] = (y * (1.0 - lam_init) * _silu(z_ref[0].astype(F32))).astype(o_ref.dtype)

    @pl.when(n_full % 2 == 0)
    def _():
        finish(absorb(0, sa_scr, state, tops))

    @pl.when(n_full % 2 == 1)
    def _():
        tops_b = scores(0, sb_scr)
        mid = absorb(1, sa_scr, state, tops)
        finish(absorb(0, sb_scr, mid, tops_b))


def _diff_attn(proj, lam_vec, gain, lam_init, *, tq):
    b, l, _ = proj.shape

    def qcol(seg):
        return pl.BlockSpec((1, tq, HEAD), lambda bi, h, t, seg=seg: (bi, t, seg * N_HEADS + h))

    def kvcol(seg):
        return pl.BlockSpec((1, l, HEAD), lambda bi, h, t, seg=seg: (bi, 0, seg * N_HEADS + h))

    tk = min(DIFF_TRIP, l)
    kern = functools.partial(_diff_kernel, lam_init=lam_init, tq=tq, tk=tk)
    return pl.pallas_call(
        kern, grid=(b, N_HEADS, l // tq),
        in_specs=[qcol(4), kvcol(5), kvcol(6), qcol(7),
                  pl.BlockSpec(lam_vec.shape, lambda bi, h, t: (0, 0)),
                  pl.BlockSpec((1, HEAD), lambda bi, h, t: (0, 0))],
        out_specs=pl.BlockSpec((1, tq, HEAD), lambda bi, h, t: (bi, t, h)),
        out_shape=jax.ShapeDtypeStruct((b, l, BRANCH), BF16),
        scratch_shapes=[pltpu.VMEM((l // tk, HEAD, tk), BF16), pltpu.VMEM((2, tk, tq), F32), pltpu.VMEM((2, tk, tq), F32)],
        compiler_params=_params("parallel", "parallel", "arbitrary"), name="diff_attn",
    )(proj, proj, proj, proj, lam_vec, gain)


def _sb_kernel(q_ref, k_ref, v_ref, z_ref, o_ref, *, tq, heads):
    qi = pl.program_id(2)
    cols = [slice(hd * HEAD, (hd + 1) * HEAD) for hd in range(heads)]
    q = [(q_ref[0, :, c].astype(F32) * (HEAD ** -0.5)).astype(BF16) for c in cols]
    r = lax.broadcasted_iota(jnp.int32, (tq, tq), 0)
    cc = lax.broadcasted_iota(jnp.int32, (tq, tq), 1)
    later = (r > cc).astype(BF16)

    def local(kj, masked):
        rows = pl.ds(pl.multiple_of(kj * tq, tq), tq)
        zz = [_dot_nt(qh, k_ref[0, rows, c]) for qh, c in zip(q, cols)]
        sp = [_softplus(x) for x in zz]
        log_rest = [-x for x in sp]
        if masked:
            log_rest = [jnp.where(cc < r, x, 0.0) for x in log_rest]
        hi = [x.astype(BF16) for x in log_rest]
        lo = [(x - h.astype(F32)).astype(BF16) for x, h in zip(log_rest, hi)]
        inner = [z - s + (_dot(h, later) + _dot(lw, later)) for z, s, h, lw in zip(zz, sp, hi, lo)]
        return inner, [jnp.sum(x, axis=-1, keepdims=True) for x in log_rest], rows

    def weigh(part, run, masked):
        inner, total, rows = part
        w = [jnp.exp(x + rn) for x, rn in zip(inner, run)]
        if masked:
            w = [jnp.where(cc < r, x, 0.0) for x in w]
        pv = [_dot(x.astype(BF16), v_ref[0, rows, c]) for x, c in zip(w, cols)]
        return pv, [rn + t for rn, t in zip(run, total)]

    def alive(run):
        top = run[0]
        for rn in run[1:]:
            top = jnp.maximum(top, rn)
        return jnp.max(top) > -SB_SKIP

    diag = local(qi, True)
    prev = local(jnp.maximum(qi - 1, 0), False)
    acc, run = weigh(diag, [jnp.zeros((tq, 1), F32)] * heads, True)
    has_prev = (qi > 0).astype(F32)
    pv, run_prev = weigh(prev, run, False)
    acc = [a + has_prev * p for a, p in zip(acc, pv)]
    run = [rn + has_prev * (rp - rn) for rn, rp in zip(run, run_prev)]

    def cond(carry):
        kj, go, _, _ = carry
        return jnp.logical_and(kj >= 0, go)

    def body(carry):
        kj, _, run, acc = carry
        pv, run = weigh(local(kj, False), run, False)
        return kj - 1, alive(run), run, [a + p for a, p in zip(acc, pv)]

    _, _, _, acc = lax.while_loop(cond, body, (qi - 2, alive(run), run, acc))
    for a, c in zip(acc, cols):
        o_ref[0, :, c] = (a * _silu(z_ref[0, :, c].astype(F32))).astype(o_ref.dtype)


def _stick_breaking(proj, *, tq):
    b, l, _ = proj.shape
    heads = SB_HEADS_PER_STEP
    groups = N_HEADS // heads
    width = heads * HEAD

    def qcol(seg):
        return pl.BlockSpec((1, tq, width), lambda bi, h, t, seg=seg: (bi, t, seg * groups + h))

    def kvcol(seg):
        return pl.BlockSpec((1, l, width), lambda bi, h, t, seg=seg: (bi, 0, seg * groups + h))

    return pl.pallas_call(
        functools.partial(_sb_kernel, tq=tq, heads=heads), grid=(b, groups, l // tq),
        in_specs=[qcol(4), kvcol(5), kvcol(6), qcol(7)],
        out_specs=pl.BlockSpec((1, tq, width), lambda bi, h, t: (bi, t, h)),
        out_shape=jax.ShapeDtypeStruct((b, l, BRANCH), BF16),
        compiler_params=_params("parallel", "parallel", "arbitrary"), name="stick_breaking",
    )(proj, proj, proj, proj)


def _split2(x):
    hi = x.astype(BF16)
    return hi, (x - hi.astype(F32)).astype(BF16)


def _dot3(a, b):
    a_hi, a_lo = _split2(a)
    b_hi, b_lo = _split2(b)
    return _dot(a_hi, b_hi) + (_dot(a_hi, b_lo) + _dot(a_lo, b_hi))


def _dot_const3(const_bf16, x):
    hi = x.astype(BF16)
    r1 = x - hi.astype(F32)
    mid = r1.astype(BF16)
    lo = (r1 - mid.astype(F32)).astype(BF16)
    return _dot(const_bf16, hi) + (_dot(const_bf16, mid) + _dot(const_bf16, lo))


def _each(fn, *lists):
    return [fn(*args) for args in zip(*lists)]


def _unit_lower_inverse(lws, eye, blockdiag):
    ld = [lw * blockdiag for lw in lws]
    lo = _each(lambda a, b: a - b, lws, ld)
    td = [eye - x for x in ld]
    p = _each(_dot3, ld, ld)
    n_factors = int(math.log2(INV_BLOCK)) - 1
    for i in range(n_factors):
        td = _each(lambda a, b: a + _dot3(a, b), td, p)
        if i < n_factors - 1:
            p = _each(_dot3, p, p)
    n = _each(_dot3, td, lo)
    n2 = _each(_dot3, n, n)
    y = [eye - x for x in n]
    y = _each(lambda a, b: a + _dot3(a, b), y, n2)
    return _each(_dot3, y, td)


def _gdn_prep_kernel(q_ref, k_ref, v_ref, qh_ref, kh_ref, vh_ref, ab_ref, wq_ref, wk_ref, wv_ref, alog_ref, dtb_ref,
                     u_ref, w_ref, qg_ref, kd_ref, at_ref, gl_ref, ext_scr, *, n_chunks, c):
    h = pl.program_id(1)
    t = pl.program_id(2)
    t_blk = n_chunks * c

    def conv(x_ref, halo_ref, w_ref, idx):
        x = x_ref[0].astype(F32)
        ext_scr[idx, 0:GD_HALO, :] = jnp.where(t > 0, halo_ref[0].astype(F32), 0.0)
        ext_scr[idx, GD_HALO:GD_HALO + t_blk, :] = x
        w = w_ref[...]
        y = w[GD_CONV - 1:GD_CONV] * x
        for tap in range(GD_CONV - 1):
            y = y + w[tap:tap + 1] * ext_scr[idx, pl.ds(GD_HALO - (GD_CONV - 1) + tap, t_blk), :]
        return _silu(y)

    def l2n(x):
        return x * lax.rsqrt(jnp.sum(x * x, axis=-1, keepdims=True) + RMS_EPS)

    q_all = l2n(conv(q_ref, qh_ref, wq_ref, 0)) * (HEAD ** -0.5)
    k_all = l2n(conv(k_ref, kh_ref, wk_ref, 1))
    v_all = conv(v_ref, vh_ref, wv_ref, 2)

    ab = ab_ref[0]
    lane = lax.broadcasted_iota(jnp.int32, ab.shape, 1)
    g_lanes = -jnp.exp(alog_ref[...]) * _softplus(ab + dtb_ref[...])
    g_all = jnp.sum(jnp.where(lane == h, g_lanes, 0.0), axis=-1, keepdims=True)
    beta_all = jnp.sum(jnp.where(lane == h + N_HEADS, _sigmoid(ab), 0.0), axis=-1, keepdims=True)

    r = lax.broadcasted_iota(jnp.int32, (c, c), 0)
    cc = lax.broadcasted_iota(jnp.int32, (c, c), 1)
    incl = cc <= r
    strict = cc < r
    tri = incl.astype(BF16)
    eye = (cc == r).astype(F32)
    blockdiag = ((r // INV_BLOCK) == (cc // INV_BLOCK)).astype(F32)
    r2 = lax.broadcasted_iota(jnp.int32, (c, HEAD), 0)
    c2 = lax.broadcasted_iota(jnp.int32, (c, HEAD), 1)
    strict_wide = jnp.logical_and(c2 < r2, c2 < c).astype(F32)

    chunks = [slice(ci * c, (ci + 1) * c) for ci in range(n_chunks)]
    q = [q_all[sl] for sl in chunks]
    k = [k_all[sl] for sl in chunks]
    v = [v_all[sl] for sl in chunks]
    beta = [beta_all[sl] for sl in chunks]
    g_wide = [jnp.broadcast_to(g_all[sl], (c, HEAD)) for sl in chunks]
    sums = [_dot_const3(tri, jnp.concatenate([gw, gw * strict_wide], axis=1)) for gw in g_wide]
    gc = [s[:, 0:HEAD] for s in sums]
    decay = [jnp.exp(jnp.where(incl, s[:, HEAD:HEAD + c], NEG)) for s in sums]
    kb = _each(lambda a, b: a * b, k, beta)
    kbf = [x.astype(BF16) for x in k]
    lw = _each(lambda a, b, d: jnp.where(strict, _dot_nt(a.astype(BF16), b) * d, 0.0), kb, kbf, decay)
    tinv = [x.astype(BF16) for x in _unit_lower_inverse(lw, eye, blockdiag)]
    egc = [jnp.exp(x) for x in gc]
    u = _each(lambda ti, a, b: _dot(ti, (a * b).astype(BF16)), tinv, v, beta)
    w = _each(lambda ti, a, e: _dot(ti, (a * e).astype(BF16)), tinv, kb, egc)
    attn = _each(lambda a, b, d: _dot_nt(a.astype(BF16), b) * d, q, kbf, decay)
    for ci, sl in enumerate(chunks):
        gc_last = gc[ci][c - 1:c, :]
        u_ref[0, 0, sl, :] = u[ci]
        w_ref[0, 0, sl, :] = w[ci].astype(BF16)
        qg_ref[0, 0, sl, :] = (q[ci] * egc[ci]).astype(BF16)
        kd_ref[0, 0, sl, :] = (k[ci] * jnp.exp(gc_last - gc[ci])).astype(BF16)
        at_ref[0, 0, sl, :] = attn[ci].astype(BF16)
        gl_ref[0, 0, ci] = jnp.exp(gc_last)


def _gdn_scan_kernel(u_ref, w_ref, qg_ref, kd_ref, at_ref, gl_ref, z_ref, gain_ref, o_ref, s_scr, *, n_chunks, c):
    @pl.when(pl.program_id(1) == 0)
    def _():
        s_scr[...] = jnp.zeros_like(s_scr)

    heads = range(N_HEADS)
    gain = gain_ref[...]
    s = [s_scr[h] for h in heads]
    for ci in range(n_chunks):
        rows = pl.ds(ci * c, c)
        sb = [x.astype(BF16) for x in s]
        v_new = [u_ref[0, h, rows, :] - _dot(w_ref[0, h, rows, :], sb[h]) for h in heads]
        vb = [x.astype(BF16) for x in v_new]
        o = [_dot(qg_ref[0, h, rows, :], sb[h]) + _dot(at_ref[0, h, rows, :], vb[h]) for h in heads]
        s = [gl_ref[0, h, ci] * s[h]
             + lax.dot_general(kd_ref[0, h, rows, :], vb[h], (((0,), (0,)), ((), ())), preferred_element_type=F32)
             for h in heads]
        for h in heads:
            cols = slice(h * HEAD, (h + 1) * HEAD)
            ms = jnp.mean(o[h] * o[h], axis=-1, keepdims=True)
            y = o[h] * lax.rsqrt(ms + RMS_EPS) * gain
            o_ref[0, rows, cols] = (y * _silu(z_ref[0, rows, cols].astype(F32))).astype(o_ref.dtype)
    for h in heads:
        s_scr[h] = s[h]


def _gdn(proj, ab, conv_w, a_log, dt_bias, gain, *, t_prep, t_scan, c):
    b, l, _ = proj.shape
    halo_blocks = t_prep // GD_HALO

    def col(seg):
        return pl.BlockSpec((1, t_prep, HEAD), lambda bi, h, t, seg=seg: (bi, t, seg * N_HEADS + h))

    def halo(seg):
        return pl.BlockSpec((1, GD_HALO, HEAD),
                            lambda bi, h, t, seg=seg: (bi, jnp.maximum(t * halo_blocks - 1, 0), seg * N_HEADS + h))

    def wcol(seg):
        return pl.BlockSpec((GD_CONV, HEAD), lambda bi, h, t, seg=seg: (0, seg * N_HEADS + h))

    one = pl.BlockSpec((1, HEAD), lambda bi, h, t: (0, 0))

    def per_head(width):
        return pl.BlockSpec((1, 1, t_prep, width), lambda bi, h, t: (bi, h, t, 0))

    n_prep = t_prep // c
    u, w, qg, kd, at, gl = pl.pallas_call(
        functools.partial(_gdn_prep_kernel, n_chunks=n_prep, c=c), grid=(b, N_HEADS, l // t_prep),
        in_specs=[col(0), col(1), col(2), halo(0), halo(1), halo(2),
                  pl.BlockSpec((1, t_prep, HEAD), lambda bi, h, t: (bi, t, 0)),
                  wcol(0), wcol(1), wcol(2), one, one],
        out_specs=[per_head(HEAD), per_head(HEAD), per_head(HEAD), per_head(HEAD), per_head(c),
                   pl.BlockSpec((1, 1, n_prep, 1, HEAD), lambda bi, h, t: (bi, h, t, 0, 0))],
        out_shape=[jax.ShapeDtypeStruct((b, N_HEADS, l, HEAD), F32),
                   jax.ShapeDtypeStruct((b, N_HEADS, l, HEAD), BF16),
                   jax.ShapeDtypeStruct((b, N_HEADS, l, HEAD), BF16),
                   jax.ShapeDtypeStruct((b, N_HEADS, l, HEAD), BF16),
                   jax.ShapeDtypeStruct((b, N_HEADS, l, c), BF16),
                   jax.ShapeDtypeStruct((b, N_HEADS, l // c, 1, HEAD), F32)],
        scratch_shapes=[pltpu.VMEM((3, t_prep + GD_HALO, HEAD), F32)],
        compiler_params=_params("parallel", "parallel", "parallel"), name="gdn_prep",
    )(proj, proj, proj, proj, proj, proj, ab, conv_w, conv_w, conv_w, a_log, dt_bias)

    def all_heads(width):
        return pl.BlockSpec((1, N_HEADS, t_scan, width), lambda bi, t: (bi, 0, t, 0))

    n_scan = t_scan // c
    return pl.pallas_call(
        functools.partial(_gdn_scan_kernel, n_chunks=n_scan, c=c), grid=(b, l // t_scan),
        in_specs=[all_heads(HEAD), all_heads(HEAD), all_heads(HEAD), all_heads(HEAD), all_heads(c),
                  pl.BlockSpec((1, N_HEADS, n_scan, 1, HEAD), lambda bi, t: (bi, 0, t, 0, 0)),
                  pl.BlockSpec((1, t_scan, BRANCH), lambda bi, t: (bi, t, 3)),
                  pl.BlockSpec((1, HEAD), lambda bi, t: (0, 0))],
        out_specs=pl.BlockSpec((1, t_scan, BRANCH), lambda bi, t: (bi, t, 0)),
        out_shape=jax.ShapeDtypeStruct((b, l, BRANCH), BF16),
        scratch_shapes=[pltpu.VMEM((N_HEADS, HEAD, HEAD), F32)],
        compiler_params=_params("parallel", "arbitrary"), name="gdn_scan",
    )(u, w, qg, kd, at, gl, proj, gain)


def _pad_lanes(vec):
    return jnp.zeros((1, HEAD), F32).at[0, :vec.shape[0]].set(vec.astype(F32))


def kernel(x, norm_pre, norm_post, ev_w_in, ev_w_out, hg_lb_logits, hg_norm, df_lambda, df_norm,
           od_w_in, od_w_out, gd_conv, gd_a_log, gd_dt_bias, gd_norm):
    p = dict(norm_pre=norm_pre, norm_post=norm_post, ev_w_in=ev_w_in, ev_w_out=ev_w_out,
             hg_lb_logits=hg_lb_logits, hg_norm=hg_norm, df_lambda=df_lambda, df_norm=df_norm,
             od_w_in=od_w_in, od_w_out=od_w_out, gd_conv=gd_conv, gd_a_log=gd_a_log,
             gd_dt_bias=gd_dt_bias, gd_norm=gd_norm)
    return _run_layers(x, p, range(norm_pre.shape[0]))


def _run_layers(x, p, layers):
    norm_pre, norm_post = p["norm_pre"], p["norm_post"]
    ev_w_in, ev_w_out, od_w_in, od_w_out = p["ev_w_in"], p["ev_w_out"], p["od_w_in"], p["od_w_out"]
    hg_lb_logits, hg_norm, df_lambda, df_norm = p["hg_lb_logits"], p["hg_norm"], p["df_lambda"], p["df_norm"]
    gd_conv, gd_a_log, gd_dt_bias, gd_norm = p["gd_conv"], p["gd_a_log"], p["gd_dt_bias"], p["gd_norm"]
    b, l, d = x.shape
    m = b * l
    tm_in = min(1024, m)
    tm_out = min(512, m)
    t_rec = min(REC_BLOCK, l)
    t_att = min(ATT_BLOCK, l)
    x2 = x.reshape(m, d)
    for layer in layers:
        j = layer // 2
        g_pre = norm_pre[layer].reshape(1, d)
        g_post = norm_post[layer].reshape(1, d)
        if layer % 2 == 0:
            proj = _inproj(x2, g_pre, ev_w_in[j].astype(BF16), tm=tm_in, tn=1024).reshape(b, l, -1)
            lam_init = 0.8 - 0.6 * math.exp(-0.3 * layer)
            mix_a = _hgrn(proj, hg_lb_logits.astype(F32), hg_norm[j].reshape(1, HEAD), j, c=min(HG_CHUNK, l))
            mix_b = _diff_attn(proj, df_lambda[j].astype(F32), df_norm[j].reshape(1, HEAD), lam_init,
                               tq=min(DIFF_Q_BLOCK, l))
            w_out = ev_w_out[j].astype(BF16)
        else:
            w = od_w_in[j]
            w_main = jnp.concatenate([w[:, :4 * BRANCH], w[:, 4 * BRANCH + 2 * N_HEADS:]], axis=1).astype(BF16)
            w_ab = jnp.pad(w[:, 4 * BRANCH:4 * BRANCH + 2 * N_HEADS], ((0, 0), (0, HEAD - 2 * N_HEADS))).astype(BF16)
            proj, ab = _inproj(x2, g_pre, w_main, w_ab, tm=tm_in, tn=1024)
            proj = proj.reshape(b, l, -1)
            mix_a = _gdn(proj, ab.reshape(b, l, HEAD), gd_conv[j].astype(F32), _pad_lanes(gd_a_log[j]),
                         _pad_lanes(gd_dt_bias[j]), gd_norm[j].reshape(1, HEAD), t_prep=min(GD_PREP_BLOCK, l), t_scan=t_rec, c=CHUNK)
            mix_b = _stick_breaking(proj, tq=t_att)
            w_out = od_w_out[j].astype(BF16)
        x2 = _outproj(mix_a.reshape(m, BRANCH), mix_b.reshape(m, BRANCH), w_out[:BRANCH], w_out[BRANCH:],
                      g_post, x2, tm=tm_out)
    return x2.reshape(b, l, d)
```

```python
import functools
import math

import numpy as np
import jax
import jax.numpy as jnp
from jax import lax
from jax.experimental import pallas as pl
from jax.experimental.pallas import tpu as pltpu

F32 = jnp.float32
BF16 = jnp.bfloat16
HIGHEST = lax.Precision.HIGHEST

D_MODEL = 2048
BRANCH = D_MODEL // 2
N_HEADS = 8
HEAD = BRANCH // N_HEADS
DF_DH = HEAD // 2
GD_CONV = 4
RMS_EPS = 1e-6
NEG = -1e30
GATE_FLOOR = 1e-20
SB_SKIP = 104.0

V7X_VMEM_LIMIT = 56 * 1024 * 1024

IN_TILE_N = 2048
CHUNK = 64
HG_CHUNK = 256
HG_HEADS_PER_STEP = 8
REC_BLOCK = 256
ATT_BLOCK = 256
SB_HEADS_PER_STEP = 4
DIFF_Q_BLOCK = 512
DIFF_TRIP = 512
INV_BLOCK = 16
GD_PREP_BLOCK = 1024
GD_HALO = 16
GD_GROUPS = 2
GD_STAGGER = 3


def _params(*sem):
    return pltpu.CompilerParams(dimension_semantics=sem, vmem_limit_bytes=V7X_VMEM_LIMIT)


def _sigmoid(x):
    return 0.5 * jnp.tanh(0.5 * x) + 0.5


def _silu(x):
    return x * _sigmoid(x)


def _softplus(x):
    return jnp.maximum(x, 0.0) + jnp.log(1.0 + jnp.exp(-jnp.abs(x)))


def _dot(a, b):
    return jnp.dot(a, b, preferred_element_type=F32)


def _dot_nt(a, b):
    return lax.dot_general(a, b, (((1,), (1,)), ((), ())), preferred_element_type=F32)


def _dot_exact(a, b):
    return jnp.dot(a, b, precision=HIGHEST, preferred_element_type=F32)


def _inproj_kernel(x_ref, g_ref, w_ref, o_ref, h_scr):
    @pl.when(pl.program_id(1) == 0)
    def _():
        x = x_ref[...]
        ms = jnp.mean(x * x, axis=-1, keepdims=True)
        h_scr[...] = (x * lax.rsqrt(ms + RMS_EPS) * g_ref[...]).astype(BF16)

    o_ref[...] = _dot(h_scr[...], w_ref[...]).astype(o_ref.dtype)


def _inproj_ab_kernel(x_ref, g_ref, w_ref, wab_ref, o_ref, oab_ref, h_scr):
    @pl.when(pl.program_id(1) == 0)
    def _():
        x = x_ref[...]
        ms = jnp.mean(x * x, axis=-1, keepdims=True)
        h_scr[...] = (x * lax.rsqrt(ms + RMS_EPS) * g_ref[...]).astype(BF16)
        oab_ref[...] = _dot(h_scr[...], wab_ref[...])

    o_ref[...] = _dot(h_scr[...], w_ref[...]).astype(o_ref.dtype)


def _inproj(x2, gain, w, w_ab=None, *, tm, tn):
    m, d = x2.shape
    n = w.shape[1]
    grid = (m // tm, n // tn)
    x_spec = pl.BlockSpec((tm, d), lambda i, j: (i, 0))
    g_spec = pl.BlockSpec((1, d), lambda i, j: (0, 0))
    w_spec = pl.BlockSpec((d, tn), lambda i, j: (0, j))
    o_spec = pl.BlockSpec((tm, tn), lambda i, j: (i, j))
    scratch = [pltpu.VMEM((tm, d), BF16)]
    if w_ab is None:
        return pl.pallas_call(
            _inproj_kernel, grid=grid, in_specs=[x_spec, g_spec, w_spec], out_specs=o_spec,
            out_shape=jax.ShapeDtypeStruct((m, n), BF16), scratch_shapes=scratch,
            compiler_params=_params("parallel", "arbitrary"), name="inproj",
        )(x2, gain, w)
    nab = w_ab.shape[1]
    return pl.pallas_call(
        _inproj_ab_kernel, grid=grid,
        in_specs=[x_spec, g_spec, w_spec, pl.BlockSpec((d, nab), lambda i, j: (0, 0))],
        out_specs=[o_spec, pl.BlockSpec((tm, nab), lambda i, j: (i, 0))],
        out_shape=[jax.ShapeDtypeStruct((m, n), BF16), jax.ShapeDtypeStruct((m, nab), F32)],
        scratch_shapes=scratch, compiler_params=_params("parallel", "arbitrary"), name="inproj_ab",
    )(x2, gain, w, w_ab)


def _outproj_kernel(ma_ref, mb_ref, wa_ref, wb_ref, g_ref, x_ref, o_ref):
    y = _dot(ma_ref[...], wa_ref[...]) + _dot(mb_ref[...], wb_ref[...])
    ms = jnp.mean(y * y, axis=-1, keepdims=True)
    o_ref[...] = x_ref[...] + y * lax.rsqrt(ms + RMS_EPS) * g_ref[...]


def _outproj(mix_a, mix_b, w_a, w_b, gain, x2, *, tm):
    m, d = x2.shape
    k = mix_a.shape[1]
    return pl.pallas_call(
        _outproj_kernel, grid=(m // tm,),
        in_specs=[pl.BlockSpec((tm, k), lambda i: (i, 0)), pl.BlockSpec((tm, k), lambda i: (i, 0)),
                  pl.BlockSpec((k, d), lambda i: (0, 0)), pl.BlockSpec((k, d), lambda i: (0, 0)),
                  pl.BlockSpec((1, d), lambda i: (0, 0)), pl.BlockSpec((tm, d), lambda i: (i, 0))],
        out_specs=pl.BlockSpec((tm, d), lambda i: (i, 0)),
        out_shape=jax.ShapeDtypeStruct((m, d), F32),
        compiler_params=_params("parallel"), name="outproj",
    )(mix_a, mix_b, w_a, w_b, gain, x2)


def _hgrn_constants(c):
    n = c // 2
    t = np.arange(n)
    masks = [np.eye(n, dtype=np.float32)]
    b = 1
    while b < n:
        mid = (t // (2 * b)) * (2 * b) + b
        same = (t[:, None] // (2 * b)) == (t[None, :] // (2 * b))
        masks.append((same & (t[:, None] >= mid[:, None]) & (t[None, :] < mid[:, None])).astype(np.float32))
        b *= 2
    rows = np.arange(c)
    tri = (rows[None, :] <= rows[:, None]).astype(np.float32)
    return tri, np.stack(masks, axis=0)


HG_PAD = 8


def _hgrn_kernel(q_ref, f_ref, i_ref, z_ref, lbl_ref, gain_ref, tri_ref, mask_ref, o_ref, *scratch,
                 layer_j, c, heads):
    @pl.when(pl.program_id(2) == 0)
    def _():
        for hd in range(heads):
            scratch[3 * hd][...] = jnp.zeros((HEAD, HEAD), F32)

    for hd in range(heads):
        _hgrn_head(slice(hd * HEAD, (hd + 1) * HEAD), q_ref, f_ref, i_ref, z_ref, lbl_ref, gain_ref, tri_ref,
                   mask_ref, o_ref, *scratch[3 * hd:3 * hd + 3], layer_j=layer_j, c=c)


def _hgrn_head(cols, q_ref, f_ref, i_ref, z_ref, lbl_ref, gain_ref, tri_ref, mask_ref, o_ref,
               st_scr, a_scr, sh_scr, *, layer_j, c):
    n = c // 2

    logits = lbl_ref[:, cols]
    e = jnp.exp(logits - jnp.max(logits, axis=0, keepdims=True))
    p = e / jnp.sum(e, axis=0, keepdims=True)
    lb = jnp.sum(p[0:layer_j + 1], axis=0, keepdims=True) - p[0:1]

    q = _silu(q_ref[0, :, cols].astype(F32))
    sig = _sigmoid(f_ref[0, :, cols].astype(F32))
    g = jnp.log(jnp.maximum(lb + (1.0 - lb) * sig, GATE_FLOOR))
    k = (1.0 - lb) * (1.0 - sig)
    vb = i_ref[0, :, cols]
    qb = q.astype(BF16)
    kb = k.astype(BF16)

    a = _dot_const3(tri_ref[...], g)
    zeros_pad = jnp.zeros((HG_PAD, HEAD), F32)
    a_scr[0:HG_PAD, :] = zeros_pad
    a_scr[HG_PAD:HG_PAD + c, :] = a

    def seg_rows(b, first):
        return jnp.concatenate([jnp.broadcast_to(a_scr[pl.ds(first + j * b, 1), :], (b, HEAD))
                                for j in range(c // b)], axis=0)

    def shifted(x, slot):
        sh_scr[slot, 0:HG_PAD, :] = zeros_pad
        sh_scr[slot, HG_PAD + c:HG_PAD + c + HG_PAD, :] = zeros_pad
        sh_scr[slot, HG_PAD:HG_PAD + c, :] = x
        return lambda d: sh_scr[slot, pl.ds(HG_PAD + d, c), :]

    row = lax.broadcasted_iota(jnp.int32, (c, HEAD), 0)
    g_at = shifted(g, 0)
    odd = (row & 1) == 1
    up = {1: g, 2: g + jnp.where(odd, g_at(-1), 0.0)}
    lo = {1: None, 2: jnp.where(odd, 0.0, g_at(1))}
    up2_at = shifted(up[2], 1)
    suffix2_at = shifted(lo[2] + g, 2)
    pos = row & 3
    up[4] = up[2] + jnp.where(pos == 2, up2_at(-1), jnp.where(pos == 3, up2_at(-2), 0.0))
    lo[4] = lo[2] + jnp.where(pos == 0, suffix2_at(2), jnp.where(pos == 1, suffix2_at(1), 0.0))
    b = 8
    while b <= c:
        up[b] = a - seg_rows(b, HG_PAD - 1)
        lo[b] = seg_rows(b, HG_PAD - 1 + b) - a
        b *= 2

    def decayed(b):
        qd = (q * jnp.exp(up[b])).astype(BF16)
        kd = kb if lo[b] is None else (k * jnp.exp(lo[b])).astype(BF16)
        return qd, kd

    halves = (slice(0, n), slice(n, c))
    scores = [mask_ref[0] * _dot_nt(qb[h], kb[h]) for h in halves]
    b, lv = 1, 1
    while b < n:
        qd, kd = decayed(b)
        scores = [s + mask_ref[lv] * _dot_nt(qd[h], kd[h]) for s, h in zip(scores, halves)]
        b, lv = 2 * b, lv + 1
    qd, kd = decayed(n)
    cross = _dot_nt(qd[halves[1]], kd[halves[0]])
    o_top = _dot(scores[0].astype(BF16), vb[halves[0]])
    o_bot = _dot(jnp.concatenate([cross, scores[1]], axis=1).astype(BF16), vb)
    qd, kd = decayed(c)
    st = st_scr[...]
    o = jnp.concatenate([o_top, o_bot], axis=0) + _dot_nt(qd, st.astype(BF16))
    st_scr[...] = jnp.exp(a[c - 1:c, :]) * st + lax.dot_general(vb, kd, (((0,), (0,)), ((), ())),
                                                               preferred_element_type=F32)

    ms = jnp.mean(o * o, axis=-1, keepdims=True)
    y = o * lax.rsqrt(ms + RMS_EPS) * gain_ref[...]
    o_ref[0, :, cols] = (y * _silu(z_ref[0, :, cols].astype(F32))).astype(o_ref.dtype)


def _hgrn(proj, lb_logits, gain, layer_j, *, c):
    b, l, _ = proj.shape
    tri, masks = _hgrn_constants(c)
    n_even = lb_logits.shape[0]
    heads = HG_HEADS_PER_STEP
    groups = N_HEADS // heads
    width = heads * HEAD

    def col(seg):
        return pl.BlockSpec((1, c, width), lambda bi, h, t, seg=seg: (bi, t, seg * groups + h))

    kern = functools.partial(_hgrn_kernel, layer_j=layer_j, c=c, heads=heads)
    per_head_scratch = [pltpu.VMEM((HEAD, HEAD), F32), pltpu.VMEM((HG_PAD + c, HEAD), F32),
                        pltpu.VMEM((3, c + 2 * HG_PAD, HEAD), F32)]
    return pl.pallas_call(
        kern, grid=(b, groups, l // c),
        in_specs=[col(0), col(1), col(2), col(3),
                  pl.BlockSpec((n_even, width), lambda bi, h, t: (0, h)),
                  pl.BlockSpec((1, HEAD), lambda bi, h, t: (0, 0)),
                  pl.BlockSpec(tri.shape, lambda bi, h, t: (0, 0)),
                  pl.BlockSpec(masks.shape, lambda bi, h, t: (0, 0, 0))],
        out_specs=pl.BlockSpec((1, c, width), lambda bi, h, t: (bi, t, h)),
        out_shape=jax.ShapeDtypeStruct((b, l, BRANCH), BF16),
        scratch_shapes=per_head_scratch * heads,
        compiler_params=_params("parallel", "parallel", "arbitrary"), name="hgrn2",
    )(proj, proj, proj, proj, lb_logits, gain, jnp.asarray(tri, BF16), jnp.asarray(masks))


def _diff_kernel(q_ref, k_ref, v_ref, z_ref, lamv_ref, gain_ref, o_ref, vt_scr, sa_scr, sb_scr, *, lam_init, tq, tk):
    qi = pl.program_id(2)
    n_trips = k_ref.shape[1] // tk

    @pl.when(qi == 0)
    def _():
        def transpose_block(i, carry):
            rows = pl.ds(pl.multiple_of(i * tk, tk), tk)
            vt_scr[i] = v_ref[0, rows, :].astype(F32).T.astype(BF16)
            return carry
        lax.fori_loop(0, n_trips, transpose_block, 0)

    lv = lamv_ref[...]
    d01 = jnp.sum(jnp.sum(lv[0:1] * lv[1:2], axis=-1, keepdims=True), axis=0, keepdims=True)
    d23 = jnp.sum(jnp.sum(lv[2:3] * lv[3:4], axis=-1, keepdims=True), axis=0, keepdims=True)
    lam = jnp.exp(d01) - jnp.exp(d23) + lam_init

    q = q_ref[0].astype(F32) * (DF_DH ** -0.5 * math.log2(math.e))
    lane = lax.broadcasted_iota(jnp.int32, q.shape, 1)
    zero = jnp.zeros_like(q)
    halves = (jnp.where(lane < DF_DH, q, zero).astype(BF16), jnp.where(lane >= DF_DH, q, zero).astype(BF16))

    def scores(t, s_buf, key_limit=None):
        k = k_ref[0, pl.ds(pl.multiple_of(t * tk, tk), tk), :]
        tops = []
        for hf, qh in enumerate(halves):
            s = _dot_nt(k, qh)
            if key_limit is not None:
                r = lax.broadcasted_iota(jnp.int32, (tk, tq), 0)
                cc = lax.broadcasted_iota(jnp.int32, (tk, tq), 1)
                s = jnp.where(r - cc <= key_limit, s, NEG)
            s_buf[hf] = s
            tops.append(jnp.max(s, axis=0, keepdims=True))
        return tuple(tops)

    def absorb(t, s_buf, state, tops):
        vt = vt_scr[t]
        out = []
        for hf in range(2):
            m, l, acc = state[hf]
            s = s_buf[hf]
            m_new = jnp.maximum(m, tops[hf])
            alpha = jnp.exp2(m - m_new)
            p = jnp.exp2(s - m_new)
            l = alpha * l + jnp.sum(p, axis=0, keepdims=True)
            acc = alpha * acc + _dot(vt, p.astype(BF16))
            out.append((m_new, l, acc))
        return tuple(out)

    n_full = (qi * tq) // tk
    state = tuple((jnp.full((1, tq), NEG, F32), jnp.zeros((1, tq), F32), jnp.zeros((HEAD, tq), F32))
                  for _ in halves)
    tops = scores(n_full, sa_scr, key_limit=qi * tq - n_full * tk)

    def visits(v, state, tops_a, n_pairs):
        for _ in range(n_pairs):
            tops_b = scores(n_full - (v + 1), sb_scr)
            state = absorb(n_full - v, sa_scr, state, tops_a)
            tops_a = scores(n_full - (v + 2), sa_scr)
            state = absorb(n_full - (v + 1), sb_scr, state, tops_b)
            v = v + 2
        return state, tops_a

    n_quads = n_full // 4
    state, tops = lax.fori_loop(0, n_quads, lambda i, c: visits(4 * i, c[0], c[1], 2), (state, tops))
    n_pairs = (n_full - 4 * n_quads) // 2
    state, tops = lax.fori_loop(0, n_pairs, lambda i, c: visits(4 * n_quads + 2 * i, c[0], c[1], 1),
                                (state, tops))

    def finish(state):
        (_, l1, acc1), (_, l2, acc2) = state
        o = (acc1 / l1 - lam * (acc2 / l2)).T
        ms = jnp.mean(o * o, axis=-1, keepdims=True)
        y = o * lax.rsqrt(ms + RMS_EPS) * gain_ref[...]
        o_ref[0] = (y * (1.0 - lam_init) * _silu(z_ref[0].astype(F32))).astype(o_ref.dtype)

    @pl.when(n_full % 2 == 0)
    def _():
        finish(absorb(0, sa_scr, state, tops))

    @pl.when(n_full % 2 == 1)
    def _():
        tops_b = scores(0, sb_scr)
        mid = absorb(1, sa_scr, state, tops)
        finish(absorb(0, sb_scr, mid, tops_b))


def _diff_attn(proj, lam_vec, gain, lam_init, *, tq):
    b, l, _ = proj.shape

    def qcol(seg):
        return pl.BlockSpec((1, tq, HEAD), lambda bi, h, t, seg=seg: (bi, t, seg * N_HEADS + h))

    def kvcol(seg):
        return pl.BlockSpec((1, l, HEAD), lambda bi, h, t, seg=seg: (bi, 0, seg * N_HEADS + h))

    tk = min(DIFF_TRIP, l)
    kern = functools.partial(_diff_kernel, lam_init=lam_init, tq=tq, tk=tk)
    return pl.pallas_call(
        kern, grid=(b, N_HEADS, l // tq),
        in_specs=[qcol(4), kvcol(5), kvcol(6), qcol(7),
                  pl.BlockSpec(lam_vec.shape, lambda bi, h, t: (0, 0)),
                  pl.BlockSpec((1, HEAD), lambda bi, h, t: (0, 0))],
        out_specs=pl.BlockSpec((1, tq, HEAD), lambda bi, h, t: (bi, t, h)),
        out_shape=jax.ShapeDtypeStruct((b, l, BRANCH), BF16),
        scratch_shapes=[pltpu.VMEM((l // tk, HEAD, tk), BF16), pltpu.VMEM((2, tk, tq), F32), pltpu.VMEM((2, tk, tq), F32)],
        compiler_params=_params("parallel", "parallel", "arbitrary"), name="diff_attn",
    )(proj, proj, proj, proj, lam_vec, gain)


def _sb_kernel(q_ref, k_ref, v_ref, z_ref, o_ref, *, tq, heads):
    qi = pl.program_id(2)
    cols = [slice(hd * HEAD, (hd + 1) * HEAD) for hd in range(heads)]
    q = [(q_ref[0, :, c].astype(F32) * (HEAD ** -0.5)).astype(BF16) for c in cols]
    r = lax.broadcasted_iota(jnp.int32, (tq, tq), 0)
    cc = lax.broadcasted_iota(jnp.int32, (tq, tq), 1)
    later = (r > cc).astype(BF16)

    def local(kj, masked):
        rows = pl.ds(pl.multiple_of(kj * tq, tq), tq)
        zz = [_dot_nt(qh, k_ref[0, rows, c]) for qh, c in zip(q, cols)]
        sp = [_softplus(x) for x in zz]
        log_rest = [-x for x in sp]
        if masked:
            log_rest = [jnp.where(cc < r, x, 0.0) for x in log_rest]
        hi = [x.astype(BF16) for x in log_rest]
        lo = [(x - h.astype(F32)).astype(BF16) for x, h in zip(log_rest, hi)]
        inner = [z - s + (_dot(h, later) + _dot(lw, later)) for z, s, h, lw in zip(zz, sp, hi, lo)]
        return inner, [jnp.sum(x, axis=-1, keepdims=True) for x in log_rest], rows

    def weigh(part, run, masked):
        inner, total, rows = part
        w = [jnp.exp(x + rn) for x, rn in zip(inner, run)]
        if masked:
            w = [jnp.where(cc < r, x, 0.0) for x in w]
        pv = [_dot(x.astype(BF16), v_ref[0, rows, c]) for x, c in zip(w, cols)]
        return pv, [rn + t for rn, t in zip(run, total)]

    def alive(run):
        top = run[0]
        for rn in run[1:]:
            top = jnp.maximum(top, rn)
        return jnp.max(top) > -SB_SKIP

    diag = local(qi, True)
    prev = local(jnp.maximum(qi - 1, 0), False)
    acc, run = weigh(diag, [jnp.zeros((tq, 1), F32)] * heads, True)
    has_prev = (qi > 0).astype(F32)
    pv, run_prev = weigh(prev, run, False)
    acc = [a + has_prev * p for a, p in zip(acc, pv)]
    run = [rn + has_prev * (rp - rn) for rn, rp in zip(run, run_prev)]

    def cond(carry):
        kj, go, _, _ = carry
        return jnp.logical_and(kj >= 0, go)

    def body(carry):
        kj, _, run, acc = carry
        pv, run = weigh(local(kj, False), run, False)
        return kj - 1, alive(run), run, [a + p for a, p in zip(acc, pv)]

    _, _, _, acc = lax.while_loop(cond, body, (qi - 2, alive(run), run, acc))
    for a, c in zip(acc, cols):
        o_ref[0, :, c] = (a * _silu(z_ref[0, :, c].astype(F32))).astype(o_ref.dtype)


def _stick_breaking(proj, *, tq):
    b, l, _ = proj.shape
    heads = SB_HEADS_PER_STEP
    groups = N_HEADS // heads
    width = heads * HEAD

    def qcol(seg):
        return pl.BlockSpec((1, tq, width), lambda bi, h, t, seg=seg: (bi, t, seg * groups + h))

    def kvcol(seg):
        return pl.BlockSpec((1, l, width), lambda bi, h, t, seg=seg: (bi, 0, seg * groups + h))

    return pl.pallas_call(
        functools.partial(_sb_kernel, tq=tq, heads=heads), grid=(b, groups, l // tq),
        in_specs=[qcol(4), kvcol(5), kvcol(6), qcol(7)],
        out_specs=pl.BlockSpec((1, tq, width), lambda bi, h, t: (bi, t, h)),
        out_shape=jax.ShapeDtypeStruct((b, l, BRANCH), BF16),
        compiler_params=_params("parallel", "parallel", "arbitrary"), name="stick_breaking",
    )(proj, proj, proj, proj)


def _split2(x):
    hi = x.astype(BF16)
    return hi, (x - hi.astype(F32)).astype(BF16)


def _dot3(a, b):
    a_hi, a_lo = _split2(a)
    b_hi, b_lo = _split2(b)
    return _dot(a_hi, b_hi) + (_dot(a_hi, b_lo) + _dot(a_lo, b_hi))


def _dot_const3(const_bf16, x):
    hi = x.astype(BF16)
    r1 = x - hi.astype(F32)
    mid = r1.astype(BF16)
    lo = (r1 - mid.astype(F32)).astype(BF16)
    return _dot(const_bf16, hi) + (_dot(const_bf16, mid) + _dot(const_bf16, lo))


def _each(fn, *lists):
    return [fn(*args) for args in zip(*lists)]


def _unit_lower_inverse_steps(lws, eye, blockdiag, out):
    ld = [lw * blockdiag for lw in lws]
    lo = _each(lambda a, b: a - b, lws, ld)
    td = [eye - x for x in ld]
    p = _each(_dot3, ld, ld)
    yield
    n_factors = int(math.log2(INV_BLOCK)) - 1
    for i in range(n_factors):
        td = _each(lambda a, b: a + _dot3(a, b), td, p)
        if i < n_factors - 1:
            p = _each(_dot3, p, p)
        yield
    n = _each(_dot3, td, lo)
    yield
    n2 = _each(_dot3, n, n)
    yield
    y = [eye - x for x in n]
    y = _each(lambda a, b: a + _dot3(a, b), y, n2)
    yield
    out.extend(_each(_dot3, y, td))
    yield


_DONE = object()


def _run_staggered(chains, lag):
    live = list(chains)
    for ahead, chain in enumerate(reversed(live)):
        for _ in range(ahead * lag):
            next(chain, None)
    while live:
        live = [chain for chain in live if next(chain, _DONE) is not _DONE]


def _gdn_prep_kernel(q_ref, k_ref, v_ref, qh_ref, kh_ref, vh_ref, ab_ref, wq_ref, wk_ref, wv_ref, alog_ref, dtb_ref,
                     u_ref, w_ref, qg_ref, kd_ref, at_ref, gl_ref, ext_scr, *, n_chunks, c):
    h = pl.program_id(1)
    t = pl.program_id(2)
    t_blk = n_chunks * c

    for idx, (x_ref, halo_ref) in enumerate(((q_ref, qh_ref), (k_ref, kh_ref), (v_ref, vh_ref))):
        ext_scr[idx, 0:GD_HALO, :] = jnp.where(t > 0, halo_ref[0].astype(F32), 0.0)
        ext_scr[idx, GD_HALO:GD_HALO + t_blk, :] = x_ref[0].astype(F32)

    def conv(w_ref, idx, row0, n_rows):
        w = w_ref[...]
        y = w[GD_CONV - 1:GD_CONV] * ext_scr[idx, pl.ds(GD_HALO + row0, n_rows), :]
        for tap in range(GD_CONV - 1):
            y = y + w[tap:tap + 1] * ext_scr[idx, pl.ds(GD_HALO - (GD_CONV - 1) + tap + row0, n_rows), :]
        return _silu(y)

    def l2n(x):
        return x * lax.rsqrt(jnp.sum(x * x, axis=-1, keepdims=True) + RMS_EPS)

    r = lax.broadcasted_iota(jnp.int32, (c, c), 0)
    cc = lax.broadcasted_iota(jnp.int32, (c, c), 1)
    incl = cc <= r
    strict = cc < r
    tri = incl.astype(BF16)
    eye = (cc == r).astype(F32)
    blockdiag = ((r // INV_BLOCK) == (cc // INV_BLOCK)).astype(F32)
    r2 = lax.broadcasted_iota(jnp.int32, (c, HEAD), 0)
    c2 = lax.broadcasted_iota(jnp.int32, (c, HEAD), 1)
    strict_wide = jnp.logical_and(c2 < r2, c2 < c).astype(F32)

    def chain(ids):
        row0, n_rows = ids[0] * c, len(ids) * c
        local = [slice(i * c, (i + 1) * c) for i in range(len(ids))]
        chunks = [slice(ci * c, (ci + 1) * c) for ci in ids]
        q_rows = l2n(conv(wq_ref, 0, row0, n_rows)) * (HEAD ** -0.5)
        yield
        k_rows = l2n(conv(wk_ref, 1, row0, n_rows))
        yield
        v_rows = conv(wv_ref, 2, row0, n_rows)
        ab = ab_ref[0, row0:row0 + n_rows, :]
        lane = lax.broadcasted_iota(jnp.int32, ab.shape, 1)
        g_lanes = -jnp.exp(alog_ref[...]) * _softplus(ab + dtb_ref[...])
        g_rows = jnp.sum(jnp.where(lane == h, g_lanes, 0.0), axis=-1, keepdims=True)
        beta_rows = jnp.sum(jnp.where(lane == h + N_HEADS, _sigmoid(ab), 0.0), axis=-1, keepdims=True)
        yield
        q = [q_rows[sl] for sl in local]
        k = [k_rows[sl] for sl in local]
        v = [v_rows[sl] for sl in local]
        beta = [beta_rows[sl] for sl in local]
        g_wide = [jnp.broadcast_to(g_rows[sl], (c, HEAD)) for sl in local]
        sums = [_dot_const3(tri, jnp.concatenate([gw, gw * strict_wide], axis=1)) for gw in g_wide]
        yield
        gc = [s[:, 0:HEAD] for s in sums]
        decay = [jnp.exp(jnp.where(incl, s[:, HEAD:HEAD + c], NEG)) for s in sums]
        kb = _each(lambda a, b: a * b, k, beta)
        kbf = [x.astype(BF16) for x in k]
        lw = _each(lambda a, b, d: jnp.where(strict, _dot_nt(a.astype(BF16), b) * d, 0.0), kb, kbf, decay)
        attn = _each(lambda a, b, d: _dot_nt(a.astype(BF16), b) * d, q, kbf, decay)
        yield
        inverse = []
        yield from _unit_lower_inverse_steps(lw, eye, blockdiag, inverse)
        tinv = [x.astype(BF16) for x in inverse]
        egc = [jnp.exp(x) for x in gc]
        u = _each(lambda ti, a, b: _dot(ti, (a * b).astype(BF16)), tinv, v, beta)
        w = _each(lambda ti, a, e: _dot(ti, (a * e).astype(BF16)), tinv, kb, egc)
        for i, (ci, sl) in enumerate(zip(ids, chunks)):
            gc_last = gc[i][c - 1:c, :]
            u_ref[0, 0, sl, :] = u[i]
            w_ref[0, 0, sl, :] = w[i].astype(BF16)
            qg_ref[0, 0, sl, :] = (q[i] * egc[i]).astype(BF16)
            kd_ref[0, 0, sl, :] = (k[i] * jnp.exp(gc_last - gc[i])).astype(BF16)
            at_ref[0, 0, sl, :] = attn[i].astype(BF16)
            gl_ref[0, 0, ci] = jnp.exp(gc_last)
        yield

    n_groups = min(GD_GROUPS, n_chunks)
    per_group = n_chunks // n_groups
    _run_staggered([chain(range(gi * per_group, (gi + 1) * per_group)) for gi in range(n_groups)], GD_STAGGER)


def _gdn_scan_kernel(u_ref, w_ref, qg_ref, kd_ref, at_ref, gl_ref, z_ref, gain_ref, o_ref, s_scr, *, n_chunks, c):
    @pl.when(pl.program_id(1) == 0)
    def _():
        s_scr[...] = jnp.zeros_like(s_scr)

    heads = range(N_HEADS)
    gain = gain_ref[...]
    s = [s_scr[h] for h in heads]
    for ci in range(n_chunks):
        rows = pl.ds(ci * c, c)
        sb = [x.astype(BF16) for x in s]
        v_new = [u_ref[0, h, rows, :] - _dot(w_ref[0, h, rows, :], sb[h]) for h in heads]
        vb = [x.astype(BF16) for x in v_new]
        o = [_dot(qg_ref[0, h, rows, :], sb[h]) + _dot(at_ref[0, h, rows, :], vb[h]) for h in heads]
        s = [gl_ref[0, h, ci] * s[h]
             + lax.dot_general(kd_ref[0, h, rows, :], vb[h], (((0,), (0,)), ((), ())), preferred_element_type=F32)
             for h in heads]
        for h in heads:
            cols = slice(h * HEAD, (h + 1) * HEAD)
            ms = jnp.mean(o[h] * o[h], axis=-1, keepdims=True)
            y = o[h] * lax.rsqrt(ms + RMS_EPS) * gain
            o_ref[0, rows, cols] = (y * _silu(z_ref[0, rows, cols].astype(F32))).astype(o_ref.dtype)
    for h in heads:
        s_scr[h] = s[h]


def _gdn(proj, ab, conv_w, a_log, dt_bias, gain, *, t_prep, t_scan, c):
    b, l, _ = proj.shape
    halo_blocks = t_prep // GD_HALO

    def col(seg):
        return pl.BlockSpec((1, t_prep, HEAD), lambda bi, h, t, seg=seg: (bi, t, seg * N_HEADS + h))

    def halo(seg):
        return pl.BlockSpec((1, GD_HALO, HEAD),
                            lambda bi, h, t, seg=seg: (bi, jnp.maximum(t * halo_blocks - 1, 0), seg * N_HEADS + h))

    def wcol(seg):
        return pl.BlockSpec((GD_CONV, HEAD), lambda bi, h, t, seg=seg: (0, seg * N_HEADS + h))

    one = pl.BlockSpec((1, HEAD), lambda bi, h, t: (0, 0))

    def per_head(width):
        return pl.BlockSpec((1, 1, t_prep, width), lambda bi, h, t: (bi, h, t, 0))

    n_prep = t_prep // c
    u, w, qg, kd, at, gl = pl.pallas_call(
        functools.partial(_gdn_prep_kernel, n_chunks=n_prep, c=c), grid=(b, N_HEADS, l // t_prep),
        in_specs=[col(0), col(1), col(2), halo(0), halo(1), halo(2),
                  pl.BlockSpec((1, t_prep, HEAD), lambda bi, h, t: (bi, t, 0)),
                  wcol(0), wcol(1), wcol(2), one, one],
        out_specs=[per_head(HEAD), per_head(HEAD), per_head(HEAD), per_head(HEAD), per_head(c),
                   pl.BlockSpec((1, 1, n_prep, 1, HEAD), lambda bi, h, t: (bi, h, t, 0, 0))],
        out_shape=[jax.ShapeDtypeStruct((b, N_HEADS, l, HEAD), F32),
                   jax.ShapeDtypeStruct((b, N_HEADS, l, HEAD), BF16),
                   jax.ShapeDtypeStruct((b, N_HEADS, l, HEAD), BF16),
                   jax.ShapeDtypeStruct((b, N_HEADS, l, HEAD), BF16),
                   jax.ShapeDtypeStruct((b, N_HEADS, l, c), BF16),
                   jax.ShapeDtypeStruct((b, N_HEADS, l // c, 1, HEAD), F32)],
        scratch_shapes=[pltpu.VMEM((3, t_prep + GD_HALO, HEAD), F32)],
        compiler_params=_params("parallel", "parallel", "parallel"), name="gdn_prep",
    )(proj, proj, proj, proj, proj, proj, ab, conv_w, conv_w, conv_w, a_log, dt_bias)

    def all_heads(width):
        return pl.BlockSpec((1, N_HEADS, t_scan, width), lambda bi, t: (bi, 0, t, 0))

    n_scan = t_scan // c
    return pl.pallas_call(
        functools.partial(_gdn_scan_kernel, n_chunks=n_scan, c=c), grid=(b, l // t_scan),
        in_specs=[all_heads(HEAD), all_heads(HEAD), all_heads(HEAD), all_heads(HEAD), all_heads(c),
                  pl.BlockSpec((1, N_HEADS, n_scan, 1, HEAD), lambda bi, t: (bi, 0, t, 0, 0)),
                  pl.BlockSpec((1, t_scan, BRANCH), lambda bi, t: (bi, t, 3)),
                  pl.BlockSpec((1, HEAD), lambda bi, t: (0, 0))],
        out_specs=pl.BlockSpec((1, t_scan, BRANCH), lambda bi, t: (bi, t, 0)),
        out_shape=jax.ShapeDtypeStruct((b, l, BRANCH), BF16),
        scratch_shapes=[pltpu.VMEM((N_HEADS, HEAD, HEAD), F32)],
        compiler_params=_params("parallel", "arbitrary"), name="gdn_scan",
    )(u, w, qg, kd, at, gl, proj, gain)


def _pad_lanes(vec):
    return jnp.zeros((1, HEAD), F32).at[0, :vec.shape[0]].set(vec.astype(F32))


def kernel(x, norm_pre, norm_post, ev_w_in, ev_w_out, hg_lb_logits, hg_norm, df_lambda, df_norm,
           od_w_in, od_w_out, gd_conv, gd_a_log, gd_dt_bias, gd_norm):
    p = dict(norm_pre=norm_pre, norm_post=norm_post, ev_w_in=ev_w_in, ev_w_out=ev_w_out,
             hg_lb_logits=hg_lb_logits, hg_norm=hg_norm, df_lambda=df_lambda, df_norm=df_norm,
             od_w_in=od_w_in, od_w_out=od_w_out, gd_conv=gd_conv, gd_a_log=gd_a_log,
             gd_dt_bias=gd_dt_bias, gd_norm=gd_norm)
    return _run_layers(x, p, range(norm_pre.shape[0]))


def _run_layers(x, p, layers):
    norm_pre, norm_post = p["norm_pre"], p["norm_post"]
    ev_w_in, ev_w_out, od_w_in, od_w_out = p["ev_w_in"], p["ev_w_out"], p["od_w_in"], p["od_w_out"]
    hg_lb_logits, hg_norm, df_lambda, df_norm = p["hg_lb_logits"], p["hg_norm"], p["df_lambda"], p["df_norm"]
    gd_conv, gd_a_log, gd_dt_bias, gd_norm = p["gd_conv"], p["gd_a_log"], p["gd_dt_bias"], p["gd_norm"]
    b, l, d = x.shape
    m = b * l
    tm_in = min(1024, m)
    tm_out = min(512, m)
    t_rec = min(REC_BLOCK, l)
    t_att = min(ATT_BLOCK, l)
    x2 = x.reshape(m, d)
    for layer in layers:
        j = layer // 2
        g_pre = norm_pre[layer].reshape(1, d)
        g_post = norm_post[layer].reshape(1, d)
        if layer % 2 == 0:
            proj = _inproj(x2, g_pre, ev_w_in[j].astype(BF16), tm=tm_in, tn=IN_TILE_N).reshape(b, l, -1)
            lam_init = 0.8 - 0.6 * math.exp(-0.3 * layer)
            mix_a = _hgrn(proj, hg_lb_logits.astype(F32), hg_norm[j].reshape(1, HEAD), j, c=min(HG_CHUNK, l))
            mix_b = _diff_attn(proj, df_lambda[j].astype(F32), df_norm[j].reshape(1, HEAD), lam_init,
                               tq=min(DIFF_Q_BLOCK, l))
            w_out = ev_w_out[j].astype(BF16)
        else:
            w = od_w_in[j]
            w_main = jnp.concatenate([w[:, :4 * BRANCH], w[:, 4 * BRANCH + 2 * N_HEADS:]], axis=1).astype(BF16)
            w_ab = jnp.pad(w[:, 4 * BRANCH:4 * BRANCH + 2 * N_HEADS], ((0, 0), (0, HEAD - 2 * N_HEADS))).astype(BF16)
            proj, ab = _inproj(x2, g_pre, w_main, w_ab, tm=tm_in, tn=IN_TILE_N)
            proj = proj.reshape(b, l, -1)
            mix_a = _gdn(proj, ab.reshape(b, l, HEAD), gd_conv[j].astype(F32), _pad_lanes(gd_a_log[j]),
                         _pad_lanes(gd_dt_bias[j]), gd_norm[j].reshape(1, HEAD), t_prep=min(GD_PREP_BLOCK, l), t_scan=t_rec, c=CHUNK)
            mix_b = _stick_breaking(proj, tq=t_att)
            w_out = od_w_out[j].astype(BF16)
        x2 = _outproj(mix_a.reshape(m, BRANCH), mix_b.reshape(m, BRANCH), w_out[:BRANCH], w_out[BRANCH:],
                      g_post, x2, tm=tm_out)
    return x2.reshape(b, l, d)
```

```python
import functools
import math

import numpy as np
import jax
import jax.numpy as jnp
from jax import lax
from jax.experimental import pallas as pl
from jax.experimental.pallas import tpu as pltpu

F32 = jnp.float32
BF16 = jnp.bfloat16
HIGHEST = lax.Precision.HIGHEST

D_MODEL = 2048
BRANCH = D_MODEL // 2
N_HEADS = 8
HEAD = BRANCH // N_HEADS
DF_DH = HEAD // 2
GD_CONV = 4
RMS_EPS = 1e-6
NEG = -1e30
GATE_FLOOR = 1e-20
SB_SKIP = 104.0

V7X_VMEM_LIMIT = 56 * 1024 * 1024

IN_TILE_N = 2048
CHUNK = 64
HG_CHUNK = 256
HG_HEADS_PER_STEP = 8
REC_BLOCK = 256
ATT_BLOCK = 256
SB_HEADS_PER_STEP = 4
DIFF_Q_BLOCK = 512
DIFF_TRIP = 512
DIFF_KEY_PARTS = 2
INV_BLOCK = 16
GD_PREP_BLOCK = 1024
GD_HALO = 16
GD_GROUPS = 2
GD_STAGGER = 3


def _params(*sem):
    return pltpu.CompilerParams(dimension_semantics=sem, vmem_limit_bytes=V7X_VMEM_LIMIT)


def _sigmoid(x):
    return 0.5 * jnp.tanh(0.5 * x) + 0.5


def _silu(x):
    return x * _sigmoid(x)


def _softplus(x):
    return jnp.maximum(x, 0.0) + jnp.log(1.0 + jnp.exp(-jnp.abs(x)))


def _dot(a, b):
    return jnp.dot(a, b, preferred_element_type=F32)


def _dot_nt(a, b):
    return lax.dot_general(a, b, (((1,), (1,)), ((), ())), preferred_element_type=F32)


def _dot_exact(a, b):
    return jnp.dot(a, b, precision=HIGHEST, preferred_element_type=F32)


def _inproj_kernel(x_ref, g_ref, w_ref, o_ref, h_scr):
    @pl.when(pl.program_id(1) == 0)
    def _():
        x = x_ref[...]
        ms = jnp.mean(x * x, axis=-1, keepdims=True)
        h_scr[...] = (x * lax.rsqrt(ms + RMS_EPS) * g_ref[...]).astype(BF16)

    o_ref[...] = _dot(h_scr[...], w_ref[...]).astype(o_ref.dtype)


def _inproj_ab_kernel(x_ref, g_ref, w_ref, wab_ref, o_ref, oab_ref, h_scr):
    @pl.when(pl.program_id(1) == 0)
    def _():
        x = x_ref[...]
        ms = jnp.mean(x * x, axis=-1, keepdims=True)
        h_scr[...] = (x * lax.rsqrt(ms + RMS_EPS) * g_ref[...]).astype(BF16)
        oab_ref[...] = _dot(h_scr[...], wab_ref[...])

    o_ref[...] = _dot(h_scr[...], w_ref[...]).astype(o_ref.dtype)


def _inproj(x2, gain, w, w_ab=None, *, tm, tn):
    m, d = x2.shape
    n = w.shape[1]
    grid = (m // tm, n // tn)
    x_spec = pl.BlockSpec((tm, d), lambda i, j: (i, 0))
    g_spec = pl.BlockSpec((1, d), lambda i, j: (0, 0))
    w_spec = pl.BlockSpec((d, tn), lambda i, j: (0, j))
    o_spec = pl.BlockSpec((tm, tn), lambda i, j: (i, j))
    scratch = [pltpu.VMEM((tm, d), BF16)]
    if w_ab is None:
        return pl.pallas_call(
            _inproj_kernel, grid=grid, in_specs=[x_spec, g_spec, w_spec], out_specs=o_spec,
            out_shape=jax.ShapeDtypeStruct((m, n), BF16), scratch_shapes=scratch,
            compiler_params=_params("parallel", "arbitrary"), name="inproj",
        )(x2, gain, w)
    nab = w_ab.shape[1]
    return pl.pallas_call(
        _inproj_ab_kernel, grid=grid,
        in_specs=[x_spec, g_spec, w_spec, pl.BlockSpec((d, nab), lambda i, j: (0, 0))],
        out_specs=[o_spec, pl.BlockSpec((tm, nab), lambda i, j: (i, 0))],
        out_shape=[jax.ShapeDtypeStruct((m, n), BF16), jax.ShapeDtypeStruct((m, nab), F32)],
        scratch_shapes=scratch, compiler_params=_params("parallel", "arbitrary"), name="inproj_ab",
    )(x2, gain, w, w_ab)


def _outproj_kernel(ma_ref, mb_ref, wa_ref, wb_ref, g_ref, x_ref, o_ref):
    y = _dot(ma_ref[...], wa_ref[...]) + _dot(mb_ref[...], wb_ref[...])
    ms = jnp.mean(y * y, axis=-1, keepdims=True)
    o_ref[...] = x_ref[...] + y * lax.rsqrt(ms + RMS_EPS) * g_ref[...]


def _outproj(mix_a, mix_b, w_a, w_b, gain, x2, *, tm):
    m, d = x2.shape
    k = mix_a.shape[1]
    return pl.pallas_call(
        _outproj_kernel, grid=(m // tm,),
        in_specs=[pl.BlockSpec((tm, k), lambda i: (i, 0)), pl.BlockSpec((tm, k), lambda i: (i, 0)),
                  pl.BlockSpec((k, d), lambda i: (0, 0)), pl.BlockSpec((k, d), lambda i: (0, 0)),
                  pl.BlockSpec((1, d), lambda i: (0, 0)), pl.BlockSpec((tm, d), lambda i: (i, 0))],
        out_specs=pl.BlockSpec((tm, d), lambda i: (i, 0)),
        out_shape=jax.ShapeDtypeStruct((m, d), F32),
        compiler_params=_params("parallel"), name="outproj",
    )(mix_a, mix_b, w_a, w_b, gain, x2)


def _hgrn_constants(c):
    n = c // 2
    t = np.arange(n)
    masks = [np.eye(n, dtype=np.float32)]
    b = 1
    while b < n:
        mid = (t // (2 * b)) * (2 * b) + b
        same = (t[:, None] // (2 * b)) == (t[None, :] // (2 * b))
        masks.append((same & (t[:, None] >= mid[:, None]) & (t[None, :] < mid[:, None])).astype(np.float32))
        b *= 2
    rows = np.arange(c)
    tri = (rows[None, :] <= rows[:, None]).astype(np.float32)
    return tri, np.stack(masks, axis=0)


HG_PAD = 8


def _hgrn_kernel(q_ref, f_ref, i_ref, z_ref, lbl_ref, gain_ref, tri_ref, mask_ref, o_ref, *scratch,
                 layer_j, c, heads):
    @pl.when(pl.program_id(2) == 0)
    def _():
        for hd in range(heads):
            scratch[3 * hd][...] = jnp.zeros((HEAD, HEAD), F32)

    for hd in range(heads):
        _hgrn_head(slice(hd * HEAD, (hd + 1) * HEAD), q_ref, f_ref, i_ref, z_ref, lbl_ref, gain_ref, tri_ref,
                   mask_ref, o_ref, *scratch[3 * hd:3 * hd + 3], layer_j=layer_j, c=c)


def _hgrn_head(cols, q_ref, f_ref, i_ref, z_ref, lbl_ref, gain_ref, tri_ref, mask_ref, o_ref,
               st_scr, a_scr, sh_scr, *, layer_j, c):
    n = c // 2

    logits = lbl_ref[:, cols]
    e = jnp.exp(logits - jnp.max(logits, axis=0, keepdims=True))
    p = e / jnp.sum(e, axis=0, keepdims=True)
    lb = jnp.sum(p[0:layer_j + 1], axis=0, keepdims=True) - p[0:1]

    q = _silu(q_ref[0, :, cols].astype(F32))
    sig = _sigmoid(f_ref[0, :, cols].astype(F32))
    g = jnp.log(jnp.maximum(lb + (1.0 - lb) * sig, GATE_FLOOR))
    k = (1.0 - lb) * (1.0 - sig)
    vb = i_ref[0, :, cols]
    qb = q.astype(BF16)
    kb = k.astype(BF16)

    a = _dot_const3(tri_ref[...], g)
    zeros_pad = jnp.zeros((HG_PAD, HEAD), F32)
    a_scr[0:HG_PAD, :] = zeros_pad
    a_scr[HG_PAD:HG_PAD + c, :] = a

    def seg_rows(b, first):
        return jnp.concatenate([jnp.broadcast_to(a_scr[pl.ds(first + j * b, 1), :], (b, HEAD))
                                for j in range(c // b)], axis=0)

    def shifted(x, slot):
        sh_scr[slot, 0:HG_PAD, :] = zeros_pad
        sh_scr[slot, HG_PAD + c:HG_PAD + c + HG_PAD, :] = zeros_pad
        sh_scr[slot, HG_PAD:HG_PAD + c, :] = x
        return lambda d: sh_scr[slot, pl.ds(HG_PAD + d, c), :]

    row = lax.broadcasted_iota(jnp.int32, (c, HEAD), 0)
    g_at = shifted(g, 0)
    odd = (row & 1) == 1
    up = {1: g, 2: g + jnp.where(odd, g_at(-1), 0.0)}
    lo = {1: None, 2: jnp.where(odd, 0.0, g_at(1))}
    up2_at = shifted(up[2], 1)
    suffix2_at = shifted(lo[2] + g, 2)
    pos = row & 3
    up[4] = up[2] + jnp.where(pos == 2, up2_at(-1), jnp.where(pos == 3, up2_at(-2), 0.0))
    lo[4] = lo[2] + jnp.where(pos == 0, suffix2_at(2), jnp.where(pos == 1, suffix2_at(1), 0.0))
    b = 8
    while b <= c:
        up[b] = a - seg_rows(b, HG_PAD - 1)
        lo[b] = seg_rows(b, HG_PAD - 1 + b) - a
        b *= 2

    def decayed(b):
        qd = (q * jnp.exp(up[b])).astype(BF16)
        kd = kb if lo[b] is None else (k * jnp.exp(lo[b])).astype(BF16)
        return qd, kd

    halves = (slice(0, n), slice(n, c))
    scores = [mask_ref[0] * _dot_nt(qb[h], kb[h]) for h in halves]
    b, lv = 1, 1
    while b < n:
        qd, kd = decayed(b)
        scores = [s + mask_ref[lv] * _dot_nt(qd[h], kd[h]) for s, h in zip(scores, halves)]
        b, lv = 2 * b, lv + 1
    qd, kd = decayed(n)
    cross = _dot_nt(qd[halves[1]], kd[halves[0]])
    o_top = _dot(scores[0].astype(BF16), vb[halves[0]])
    o_bot = _dot(jnp.concatenate([cross, scores[1]], axis=1).astype(BF16), vb)
    qd, kd = decayed(c)
    st = st_scr[...]
    o = jnp.concatenate([o_top, o_bot], axis=0) + _dot_nt(qd, st.astype(BF16))
    st_scr[...] = jnp.exp(a[c - 1:c, :]) * st + lax.dot_general(vb, kd, (((0,), (0,)), ((), ())),
                                                               preferred_element_type=F32)

    ms = jnp.mean(o * o, axis=-1, keepdims=True)
    y = o * lax.rsqrt(ms + RMS_EPS) * gain_ref[...]
    o_ref[0, :, cols] = (y * _silu(z_ref[0, :, cols].astype(F32))).astype(o_ref.dtype)


def _hgrn(proj, lb_logits, gain, layer_j, *, c):
    b, l, _ = proj.shape
    tri, masks = _hgrn_constants(c)
    n_even = lb_logits.shape[0]
    heads = HG_HEADS_PER_STEP
    groups = N_HEADS // heads
    width = heads * HEAD

    def col(seg):
        return pl.BlockSpec((1, c, width), lambda bi, h, t, seg=seg: (bi, t, seg * groups + h))

    kern = functools.partial(_hgrn_kernel, layer_j=layer_j, c=c, heads=heads)
    per_head_scratch = [pltpu.VMEM((HEAD, HEAD), F32), pltpu.VMEM((HG_PAD + c, HEAD), F32),
                        pltpu.VMEM((3, c + 2 * HG_PAD, HEAD), F32)]
    return pl.pallas_call(
        kern, grid=(b, groups, l // c),
        in_specs=[col(0), col(1), col(2), col(3),
                  pl.BlockSpec((n_even, width), lambda bi, h, t: (0, h)),
                  pl.BlockSpec((1, HEAD), lambda bi, h, t: (0, 0)),
                  pl.BlockSpec(tri.shape, lambda bi, h, t: (0, 0)),
                  pl.BlockSpec(masks.shape, lambda bi, h, t: (0, 0, 0))],
        out_specs=pl.BlockSpec((1, c, width), lambda bi, h, t: (bi, t, h)),
        out_shape=jax.ShapeDtypeStruct((b, l, BRANCH), BF16),
        scratch_shapes=per_head_scratch * heads,
        compiler_params=_params("parallel", "parallel", "arbitrary"), name="hgrn2",
    )(proj, proj, proj, proj, lb_logits, gain, jnp.asarray(tri, BF16), jnp.asarray(masks))


def _diff_kernel(q_ref, k_ref, v_ref, z_ref, lamv_ref, gain_ref, o_ref, vt_scr, sa_scr, sb_scr, *, lam_init, tq, tk):
    qi = pl.program_id(2)
    n_trips = k_ref.shape[1] // tk

    @pl.when(qi == 0)
    def _():
        def transpose_block(i, carry):
            rows = pl.ds(pl.multiple_of(i * tk, tk), tk)
            vt_scr[i] = v_ref[0, rows, :].astype(F32).T.astype(BF16)
            return carry
        lax.fori_loop(0, n_trips, transpose_block, 0)

    lv = lamv_ref[...]
    d01 = jnp.sum(jnp.sum(lv[0:1] * lv[1:2], axis=-1, keepdims=True), axis=0, keepdims=True)
    d23 = jnp.sum(jnp.sum(lv[2:3] * lv[3:4], axis=-1, keepdims=True), axis=0, keepdims=True)
    lam = jnp.exp(d01) - jnp.exp(d23) + lam_init

    q = q_ref[0].astype(F32) * (DF_DH ** -0.5 * math.log2(math.e))
    lane = lax.broadcasted_iota(jnp.int32, q.shape, 1)
    zero = jnp.zeros_like(q)
    halves = (jnp.where(lane < DF_DH, q, zero).astype(BF16), jnp.where(lane >= DF_DH, q, zero).astype(BF16))

    parts = DIFF_KEY_PARTS
    rows_per_part = tk // parts

    def scores_part(t, s_buf, hf, kp, key_limit=None):
        start = pl.multiple_of(t * tk + kp * rows_per_part, rows_per_part)
        s = _dot_nt(k_ref[0, pl.ds(start, rows_per_part), :], halves[hf])
        if key_limit is not None:
            r = lax.broadcasted_iota(jnp.int32, (rows_per_part, tq), 0) + kp * rows_per_part
            cc = lax.broadcasted_iota(jnp.int32, (rows_per_part, tq), 1)
            s = jnp.where(r - cc <= key_limit, s, NEG)
        s_buf[hf, kp * rows_per_part:(kp + 1) * rows_per_part, :] = s
        return jnp.max(s, axis=0, keepdims=True)

    def rescale(st, top):
        m, l, acc = st
        m_new = jnp.maximum(m, top)
        alpha = jnp.exp2(m - m_new)
        return m_new, alpha * l, alpha * acc

    def absorb_part(t, s_buf, hf, kp, st):
        m, l, acc = st
        lanes = slice(kp * rows_per_part, (kp + 1) * rows_per_part)
        p = jnp.exp2(s_buf[hf, lanes, :] - m)
        return m, l + jnp.sum(p, axis=0, keepdims=True), acc + _dot(vt_scr[t, :, lanes], p.astype(BF16))

    def scores(t, s_buf, key_limit=None):
        return tuple(functools.reduce(jnp.maximum, [scores_part(t, s_buf, hf, kp, key_limit) for kp in range(parts)])
                     for hf in range(2))

    def absorb(t, s_buf, state, tops):
        out = []
        for hf in range(2):
            st = rescale(state[hf], tops[hf])
            for kp in range(parts):
                st = absorb_part(t, s_buf, hf, kp, st)
            out.append(st)
        return tuple(out)

    def step(t_next, buf_next, t_cur, buf_cur, state, tops_cur):
        tops_next, out = [], []
        for hf in range(2):
            st = rescale(state[hf], tops_cur[hf])
            top = None
            for kp in range(parts):
                piece = scores_part(t_next, buf_next, hf, kp)
                top = piece if top is None else jnp.maximum(top, piece)
                st = absorb_part(t_cur, buf_cur, hf, kp, st)
            tops_next.append(top)
            out.append(st)
        return tuple(out), tuple(tops_next)

    n_full = (qi * tq) // tk
    state = tuple((jnp.full((1, tq), NEG, F32), jnp.zeros((1, tq), F32), jnp.zeros((HEAD, tq), F32))
                  for _ in halves)
    tops = scores(n_full, sa_scr, key_limit=qi * tq - n_full * tk)

    def visits(v, state, tops_a, n_pairs):
        for _ in range(n_pairs):
            state, tops_b = step(n_full - (v + 1), sb_scr, n_full - v, sa_scr, state, tops_a)
            state, tops_a = step(n_full - (v + 2), sa_scr, n_full - (v + 1), sb_scr, state, tops_b)
            v = v + 2
        return state, tops_a

    n_quads = n_full // 4
    state, tops = lax.fori_loop(0, n_quads, lambda i, c: visits(4 * i, c[0], c[1], 2), (state, tops))
    n_pairs = (n_full - 4 * n_quads) // 2
    state, tops = lax.fori_loop(0, n_pairs, lambda i, c: visits(4 * n_quads + 2 * i, c[0], c[1], 1),
                                (state, tops))

    def finish(state):
        (_, l1, acc1), (_, l2, acc2) = state
        o = (acc1 / l1 - lam * (acc2 / l2)).T
        ms = jnp.mean(o * o, axis=-1, keepdims=True)
        y = o * lax.rsqrt(ms + RMS_EPS) * gain_ref[...]
        o_ref[0] = (y * (1.0 - lam_init) * _silu(z_ref[0].astype(F32))).astype(o_ref.dtype)

    @pl.when(n_full % 2 == 0)
    def _():
        finish(absorb(0, sa_scr, state, tops))

    @pl.when(n_full % 2 == 1)
    def _():
        tops_b = scores(0, sb_scr)
        mid = absorb(1, sa_scr, state, tops)
        finish(absorb(0, sb_scr, mid, tops_b))


def _diff_attn(proj, lam_vec, gain, lam_init, *, tq):
    b, l, _ = proj.shape

    def qcol(seg):
        return pl.BlockSpec((1, tq, HEAD), lambda bi, h, t, seg=seg: (bi, t, seg * N_HEADS + h))

    def kvcol(seg):
        return pl.BlockSpec((1, l, HEAD), lambda bi, h, t, seg=seg: (bi, 0, seg * N_HEADS + h))

    tk = min(DIFF_TRIP, l)
    kern = functools.partial(_diff_kernel, lam_init=lam_init, tq=tq, tk=tk)
    return pl.pallas_call(
        kern, grid=(b, N_HEADS, l // tq),
        in_specs=[qcol(4), kvcol(5), kvcol(6), qcol(7),
                  pl.BlockSpec(lam_vec.shape, lambda bi, h, t: (0, 0)),
                  pl.BlockSpec((1, HEAD), lambda bi, h, t: (0, 0))],
        out_specs=pl.BlockSpec((1, tq, HEAD), lambda bi, h, t: (bi, t, h)),
        out_shape=jax.ShapeDtypeStruct((b, l, BRANCH), BF16),
        scratch_shapes=[pltpu.VMEM((l // tk, HEAD, tk), BF16), pltpu.VMEM((2, tk, tq), F32), pltpu.VMEM((2, tk, tq), F32)],
        compiler_params=_params("parallel", "parallel", "arbitrary"), name="diff_attn",
    )(proj, proj, proj, proj, lam_vec, gain)


def _sb_kernel(q_ref, k_ref, v_ref, z_ref, o_ref, *, tq, heads):
    qi = pl.program_id(2)
    cols = [slice(hd * HEAD, (hd + 1) * HEAD) for hd in range(heads)]
    q = [(q_ref[0, :, c].astype(F32) * (HEAD ** -0.5)).astype(BF16) for c in cols]
    r = lax.broadcasted_iota(jnp.int32, (tq, tq), 0)
    cc = lax.broadcasted_iota(jnp.int32, (tq, tq), 1)
    later = (r > cc).astype(BF16)

    def local(kj, masked):
        rows = pl.ds(pl.multiple_of(kj * tq, tq), tq)
        zz = [_dot_nt(qh, k_ref[0, rows, c]) for qh, c in zip(q, cols)]
        sp = [_softplus(x) for x in zz]
        log_rest = [-x for x in sp]
        if masked:
            log_rest = [jnp.where(cc < r, x, 0.0) for x in log_rest]
        hi = [x.astype(BF16) for x in log_rest]
        lo = [(x - h.astype(F32)).astype(BF16) for x, h in zip(log_rest, hi)]
        inner = [z - s + (_dot(h, later) + _dot(lw, later)) for z, s, h, lw in zip(zz, sp, hi, lo)]
        return inner, [jnp.sum(x, axis=-1, keepdims=True) for x in log_rest], rows

    def weigh(part, run, masked):
        inner, total, rows = part
        w = [jnp.exp(x + rn) for x, rn in zip(inner, run)]
        if masked:
            w = [jnp.where(cc < r, x, 0.0) for x in w]
        pv = [_dot(x.astype(BF16), v_ref[0, rows, c]) for x, c in zip(w, cols)]
        return pv, [rn + t for rn, t in zip(run, total)]

    def alive(run):
        top = run[0]
        for rn in run[1:]:
            top = jnp.maximum(top, rn)
        return jnp.max(top) > -SB_SKIP

    diag = local(qi, True)
    prev = local(jnp.maximum(qi - 1, 0), False)
    acc, run = weigh(diag, [jnp.zeros((tq, 1), F32)] * heads, True)
    has_prev = (qi > 0).astype(F32)
    pv, run_prev = weigh(prev, run, False)
    acc = [a + has_prev * p for a, p in zip(acc, pv)]
    run = [rn + has_prev * (rp - rn) for rn, rp in zip(run, run_prev)]

    def cond(carry):
        kj, go, _, _ = carry
        return jnp.logical_and(kj >= 0, go)

    def body(carry):
        kj, _, run, acc = carry
        pv, run = weigh(local(kj, False), run, False)
        return kj - 1, alive(run), run, [a + p for a, p in zip(acc, pv)]

    _, _, _, acc = lax.while_loop(cond, body, (qi - 2, alive(run), run, acc))
    for a, c in zip(acc, cols):
        o_ref[0, :, c] = (a * _silu(z_ref[0, :, c].astype(F32))).astype(o_ref.dtype)


def _stick_breaking(proj, *, tq):
    b, l, _ = proj.shape
    heads = SB_HEADS_PER_STEP
    groups = N_HEADS // heads
    width = heads * HEAD

    def qcol(seg):
        return pl.BlockSpec((1, tq, width), lambda bi, h, t, seg=seg: (bi, t, seg * groups + h))

    def kvcol(seg):
        return pl.BlockSpec((1, l, width), lambda bi, h, t, seg=seg: (bi, 0, seg * groups + h))

    return pl.pallas_call(
        functools.partial(_sb_kernel, tq=tq, heads=heads), grid=(b, groups, l // tq),
        in_specs=[qcol(4), kvcol(5), kvcol(6), qcol(7)],
        out_specs=pl.BlockSpec((1, tq, width), lambda bi, h, t: (bi, t, h)),
        out_shape=jax.ShapeDtypeStruct((b, l, BRANCH), BF16),
        compiler_params=_params("parallel", "parallel", "arbitrary"), name="stick_breaking",
    )(proj, proj, proj, proj)


def _split2(x):
    hi = x.astype(BF16)
    return hi, (x - hi.astype(F32)).astype(BF16)


def _dot3(a, b):
    a_hi, a_lo = _split2(a)
    b_hi, b_lo = _split2(b)
    return _dot(a_hi, b_hi) + (_dot(a_hi, b_lo) + _dot(a_lo, b_hi))


def _dot_const3(const_bf16, x):
    hi = x.astype(BF16)
    r1 = x - hi.astype(F32)
    mid = r1.astype(BF16)
    lo = (r1 - mid.astype(F32)).astype(BF16)
    return _dot(const_bf16, hi) + (_dot(const_bf16, mid) + _dot(const_bf16, lo))


def _each(fn, *lists):
    return [fn(*args) for args in zip(*lists)]


def _unit_lower_inverse_steps(lws, eye, blockdiag, out):
    ld = [lw * blockdiag for lw in lws]
    lo = _each(lambda a, b: a - b, lws, ld)
    td = [eye - x for x in ld]
    p = _each(_dot3, ld, ld)
    yield
    n_factors = int(math.log2(INV_BLOCK)) - 1
    for i in range(n_factors):
        td = _each(lambda a, b: a + _dot3(a, b), td, p)
        if i < n_factors - 1:
            p = _each(_dot3, p, p)
        yield
    n = _each(_dot3, td, lo)
    yield
    n2 = _each(_dot3, n, n)
    yield
    y = [eye - x for x in n]
    y = _each(lambda a, b: a + _dot3(a, b), y, n2)
    yield
    out.extend(_each(_dot3, y, td))
    yield


_DONE = object()


def _run_staggered(chains, lag):
    live = list(chains)
    for ahead, chain in enumerate(reversed(live)):
        for _ in range(ahead * lag):
            next(chain, None)
    while live:
        live = [chain for chain in live if next(chain, _DONE) is not _DONE]


def _gdn_prep_kernel(q_ref, k_ref, v_ref, qh_ref, kh_ref, vh_ref, ab_ref, wq_ref, wk_ref, wv_ref, alog_ref, dtb_ref,
                     u_ref, w_ref, qg_ref, kd_ref, at_ref, gl_ref, ext_scr, *, n_chunks, c):
    h = pl.program_id(1)
    t = pl.program_id(2)
    t_blk = n_chunks * c

    for idx, (x_ref, halo_ref) in enumerate(((q_ref, qh_ref), (k_ref, kh_ref), (v_ref, vh_ref))):
        ext_scr[idx, 0:GD_HALO, :] = jnp.where(t > 0, halo_ref[0].astype(F32), 0.0)
        ext_scr[idx, GD_HALO:GD_HALO + t_blk, :] = x_ref[0].astype(F32)

    def conv(w_ref, idx, row0, n_rows):
        w = w_ref[...]
        y = w[GD_CONV - 1:GD_CONV] * ext_scr[idx, pl.ds(GD_HALO + row0, n_rows), :]
        for tap in range(GD_CONV - 1):
            y = y + w[tap:tap + 1] * ext_scr[idx, pl.ds(GD_HALO - (GD_CONV - 1) + tap + row0, n_rows), :]
        return _silu(y)

    def l2n(x):
        return x * lax.rsqrt(jnp.sum(x * x, axis=-1, keepdims=True) + RMS_EPS)

    r = lax.broadcasted_iota(jnp.int32, (c, c), 0)
    cc = lax.broadcasted_iota(jnp.int32, (c, c), 1)
    incl = cc <= r
    strict = cc < r
    tri = incl.astype(BF16)
    eye = (cc == r).astype(F32)
    blockdiag = ((r // INV_BLOCK) == (cc // INV_BLOCK)).astype(F32)
    r2 = lax.broadcasted_iota(jnp.int32, (c, HEAD), 0)
    c2 = lax.broadcasted_iota(jnp.int32, (c, HEAD), 1)
    strict_wide = jnp.logical_and(c2 < r2, c2 < c).astype(F32)

    def chain(ids):
        row0, n_rows = ids[0] * c, len(ids) * c
        local = [slice(i * c, (i + 1) * c) for i in range(len(ids))]
        chunks = [slice(ci * c, (ci + 1) * c) for ci in ids]
        q_rows = l2n(conv(wq_ref, 0, row0, n_rows)) * (HEAD ** -0.5)
        yield
        k_rows = l2n(conv(wk_ref, 1, row0, n_rows))
        yield
        v_rows = conv(wv_ref, 2, row0, n_rows)
        ab = ab_ref[0, row0:row0 + n_rows, :]
        lane = lax.broadcasted_iota(jnp.int32, ab.shape, 1)
        g_lanes = -jnp.exp(alog_ref[...]) * _softplus(ab + dtb_ref[...])
        g_rows = jnp.sum(jnp.where(lane == h, g_lanes, 0.0), axis=-1, keepdims=True)
        beta_rows = jnp.sum(jnp.where(lane == h + N_HEADS, _sigmoid(ab), 0.0), axis=-1, keepdims=True)
        yield
        q = [q_rows[sl] for sl in local]
        k = [k_rows[sl] for sl in local]
        v = [v_rows[sl] for sl in local]
        beta = [beta_rows[sl] for sl in local]
        g_wide = [jnp.broadcast_to(g_rows[sl], (c, HEAD)) for sl in local]
        sums = [_dot_const3(tri, jnp.concatenate([gw, gw * strict_wide], axis=1)) for gw in g_wide]
        yield
        gc = [s[:, 0:HEAD] for s in sums]
        decay = [jnp.exp(jnp.where(incl, s[:, HEAD:HEAD + c], NEG)) for s in sums]
        kb = _each(lambda a, b: a * b, k, beta)
        kbf = [x.astype(BF16) for x in k]
        lw = _each(lambda a, b, d: jnp.where(strict, _dot_nt(a.astype(BF16), b) * d, 0.0), kb, kbf, decay)
        attn = _each(lambda a, b, d: _dot_nt(a.astype(BF16), b) * d, q, kbf, decay)
        yield
        inverse = []
        yield from _unit_lower_inverse_steps(lw, eye, blockdiag, inverse)
        tinv = [x.astype(BF16) for x in inverse]
        egc = [jnp.exp(x) for x in gc]
        u = _each(lambda ti, a, b: _dot(ti, (a * b).astype(BF16)), tinv, v, beta)
        w = _each(lambda ti, a, e: _dot(ti, (a * e).astype(BF16)), tinv, kb, egc)
        for i, (ci, sl) in enumerate(zip(ids, chunks)):
            gc_last = gc[i][c - 1:c, :]
            u_ref[0, 0, sl, :] = u[i]
            w_ref[0, 0, sl, :] = w[i].astype(BF16)
            qg_ref[0, 0, sl, :] = (q[i] * egc[i]).astype(BF16)
            kd_ref[0, 0, sl, :] = (k[i] * jnp.exp(gc_last - gc[i])).astype(BF16)
            at_ref[0, 0, sl, :] = attn[i].astype(BF16)
            gl_ref[0, 0, ci] = jnp.exp(gc_last)
        yield

    n_groups = min(GD_GROUPS, n_chunks)
    per_group = n_chunks // n_groups
    _run_staggered([chain(range(gi * per_group, (gi + 1) * per_group)) for gi in range(n_groups)], GD_STAGGER)


def _gdn_scan_kernel(u_ref, w_ref, qg_ref, kd_ref, at_ref, gl_ref, z_ref, gain_ref, o_ref, s_scr, *, n_chunks, c):
    @pl.when(pl.program_id(1) == 0)
    def _():
        s_scr[...] = jnp.zeros_like(s_scr)

    heads = range(N_HEADS)
    gain = gain_ref[...]
    s = [s_scr[h] for h in heads]
    for ci in range(n_chunks):
        rows = pl.ds(ci * c, c)
        sb = [x.astype(BF16) for x in s]
        v_new = [u_ref[0, h, rows, :] - _dot(w_ref[0, h, rows, :], sb[h]) for h in heads]
        vb = [x.astype(BF16) for x in v_new]
        o = [_dot(qg_ref[0, h, rows, :], sb[h]) + _dot(at_ref[0, h, rows, :], vb[h]) for h in heads]
        s = [gl_ref[0, h, ci] * s[h]
             + lax.dot_general(kd_ref[0, h, rows, :], vb[h], (((0,), (0,)), ((), ())), preferred_element_type=F32)
             for h in heads]
        for h in heads:
            cols = slice(h * HEAD, (h + 1) * HEAD)
            ms = jnp.mean(o[h] * o[h], axis=-1, keepdims=True)
            y = o[h] * lax.rsqrt(ms + RMS_EPS) * gain
            o_ref[0, rows, cols] = (y * _silu(z_ref[0, rows, cols].astype(F32))).astype(o_ref.dtype)
    for h in heads:
        s_scr[h] = s[h]


def _gdn(proj, ab, conv_w, a_log, dt_bias, gain, *, t_prep, t_scan, c):
    b, l, _ = proj.shape
    halo_blocks = t_prep // GD_HALO

    def col(seg):
        return pl.BlockSpec((1, t_prep, HEAD), lambda bi, h, t, seg=seg: (bi, t, seg * N_HEADS + h))

    def halo(seg):
        return pl.BlockSpec((1, GD_HALO, HEAD),
                            lambda bi, h, t, seg=seg: (bi, jnp.maximum(t * halo_blocks - 1, 0), seg * N_HEADS + h))

    def wcol(seg):
        return pl.BlockSpec((GD_CONV, HEAD), lambda bi, h, t, seg=seg: (0, seg * N_HEADS + h))

    one = pl.BlockSpec((1, HEAD), lambda bi, h, t: (0, 0))

    def per_head(width):
        return pl.BlockSpec((1, 1, t_prep, width), lambda bi, h, t: (bi, h, t, 0))

    n_prep = t_prep // c
    u, w, qg, kd, at, gl = pl.pallas_call(
        functools.partial(_gdn_prep_kernel, n_chunks=n_prep, c=c), grid=(b, N_HEADS, l // t_prep),
        in_specs=[col(0), col(1), col(2), halo(0), halo(1), halo(2),
                  pl.BlockSpec((1, t_prep, HEAD), lambda bi, h, t: (bi, t, 0)),
                  wcol(0), wcol(1), wcol(2), one, one],
        out_specs=[per_head(HEAD), per_head(HEAD), per_head(HEAD), per_head(HEAD), per_head(c),
                   pl.BlockSpec((1, 1, n_prep, 1, HEAD), lambda bi, h, t: (bi, h, t, 0, 0))],
        out_shape=[jax.ShapeDtypeStruct((b, N_HEADS, l, HEAD), F32),
                   jax.ShapeDtypeStruct((b, N_HEADS, l, HEAD), BF16),
                   jax.ShapeDtypeStruct((b, N_HEADS, l, HEAD), BF16),
                   jax.ShapeDtypeStruct((b, N_HEADS, l, HEAD), BF16),
                   jax.ShapeDtypeStruct((b, N_HEADS, l, c), BF16),
                   jax.ShapeDtypeStruct((b, N_HEADS, l // c, 1, HEAD), F32)],
        scratch_shapes=[pltpu.VMEM((3, t_prep + GD_HALO, HEAD), F32)],
        compiler_params=_params("parallel", "parallel", "parallel"), name="gdn_prep",
    )(proj, proj, proj, proj, proj, proj, ab, conv_w, conv_w, conv_w, a_log, dt_bias)

    def all_heads(width):
        return pl.BlockSpec((1, N_HEADS, t_scan, width), lambda bi, t: (bi, 0, t, 0))

    n_scan = t_scan // c
    return pl.pallas_call(
        functools.partial(_gdn_scan_kernel, n_chunks=n_scan, c=c), grid=(b, l // t_scan),
        in_specs=[all_heads(HEAD), all_heads(HEAD), all_heads(HEAD), all_heads(HEAD), all_heads(c),
                  pl.BlockSpec((1, N_HEADS, n_scan, 1, HEAD), lambda bi, t: (bi, 0, t, 0, 0)),
                  pl.BlockSpec((1, t_scan, BRANCH), lambda bi, t: (bi, t, 3)),
                  pl.BlockSpec((1, HEAD), lambda bi, t: (0, 0))],
        out_specs=pl.BlockSpec((1, t_scan, BRANCH), lambda bi, t: (bi, t, 0)),
        out_shape=jax.ShapeDtypeStruct((b, l, BRANCH), BF16),
        scratch_shapes=[pltpu.VMEM((N_HEADS, HEAD, HEAD), F32)],
        compiler_params=_params("parallel", "arbitrary"), name="gdn_scan",
    )(u, w, qg, kd, at, gl, proj, gain)


def _pad_lanes(vec):
    return jnp.zeros((1, HEAD), F32).at[0, :vec.shape[0]].set(vec.astype(F32))


def kernel(x, norm_pre, norm_post, ev_w_in, ev_w_out, hg_lb_logits, hg_norm, df_lambda, df_norm,
           od_w_in, od_w_out, gd_conv, gd_a_log, gd_dt_bias, gd_norm):
    p = dict(norm_pre=norm_pre, norm_post=norm_post, ev_w_in=ev_w_in, ev_w_out=ev_w_out,
             hg_lb_logits=hg_lb_logits, hg_norm=hg_norm, df_lambda=df_lambda, df_norm=df_norm,
             od_w_in=od_w_in, od_w_out=od_w_out, gd_conv=gd_conv, gd_a_log=gd_a_log,
             gd_dt_bias=gd_dt_bias, gd_norm=gd_norm)
    return _run_layers(x, p, range(norm_pre.shape[0]))


def _run_layers(x, p, layers):
    norm_pre, norm_post = p["norm_pre"], p["norm_post"]
    ev_w_in, ev_w_out, od_w_in, od_w_out = p["ev_w_in"], p["ev_w_out"], p["od_w_in"], p["od_w_out"]
    hg_lb_logits, hg_norm, df_lambda, df_norm = p["hg_lb_logits"], p["hg_norm"], p["df_lambda"], p["df_norm"]
    gd_conv, gd_a_log, gd_dt_bias, gd_norm = p["gd_conv"], p["gd_a_log"], p["gd_dt_bias"], p["gd_norm"]
    b, l, d = x.shape
    m = b * l
    tm_in = min(1024, m)
    tm_out = min(512, m)
    t_rec = min(REC_BLOCK, l)
    t_att = min(ATT_BLOCK, l)
    x2 = x.reshape(m, d)
    for layer in layers:
        j = layer // 2
        g_pre = norm_pre[layer].reshape(1, d)
        g_post = norm_post[layer].reshape(1, d)
        if layer % 2 == 0:
            proj = _inproj(x2, g_pre, ev_w_in[j].astype(BF16), tm=tm_in, tn=IN_TILE_N).reshape(b, l, -1)
            lam_init = 0.8 - 0.6 * math.exp(-0.3 * layer)
            mix_a = _hgrn(proj, hg_lb_logits.astype(F32), hg_norm[j].reshape(1, HEAD), j, c=min(HG_CHUNK, l))
            mix_b = _diff_attn(proj, df_lambda[j].astype(F32), df_norm[j].reshape(1, HEAD), lam_init,
                               tq=min(DIFF_Q_BLOCK, l))
            w_out = ev_w_out[j].astype(BF16)
        else:
            w = od_w_in[j]
            w_main = jnp.concatenate([w[:, :4 * BRANCH], w[:, 4 * BRANCH + 2 * N_HEADS:]], axis=1).astype(BF16)
            w_ab = jnp.pad(w[:, 4 * BRANCH:4 * BRANCH + 2 * N_HEADS], ((0, 0), (0, HEAD - 2 * N_HEADS))).astype(BF16)
            proj, ab = _inproj(x2, g_pre, w_main, w_ab, tm=tm_in, tn=IN_TILE_N)
            proj = proj.reshape(b, l, -1)
            mix_a = _gdn(proj, ab.reshape(b, l, HEAD), gd_conv[j].astype(F32), _pad_lanes(gd_a_log[j]),
                         _pad_lanes(gd_dt_bias[j]), gd_norm[j].reshape(1, HEAD), t_prep=min(GD_PREP_BLOCK, l), t_scan=t_rec, c=CHUNK)
            mix_b = _stick_breaking(proj, tq=t_att)
            w_out = od_w_out[j].astype(BF16)
        x2 = _outproj(mix_a.reshape(m, BRANCH), mix_b.reshape(m, BRANCH), w_out[:BRANCH], w_out[BRANCH:],
                      g_post, x2, tm=tm_out)
    return x2.reshape(b, l, d)
```

```python
import functools
import math

import numpy as np
import jax
import jax.numpy as jnp
from jax import lax
from jax.experimental import pallas as pl
from jax.experimental.pallas import tpu as pltpu

F32 = jnp.float32
BF16 = jnp.bfloat16
HIGHEST = lax.Precision.HIGHEST

D_MODEL = 2048
BRANCH = D_MODEL // 2
N_HEADS = 8
HEAD = BRANCH // N_HEADS
DF_DH = HEAD // 2
GD_CONV = 4
RMS_EPS = 1e-6
NEG = -1e30
GATE_FLOOR = 1e-20
SB_SKIP = 104.0

V7X_VMEM_LIMIT = 56 * 1024 * 1024

IN_TILE_N = 2048
CHUNK = 64
HG_CHUNK = 256
HG_HEADS_PER_STEP = 8
HG_STAGGER = 5
REC_BLOCK = 256
ATT_BLOCK = 256
SB_HEADS_PER_STEP = 4
DIFF_Q_BLOCK = 512
DIFF_KEY_PARTS = 2
INV_BLOCK = 16
GD_PREP_BLOCK = 1024
GD_HALO = 16
GD_GROUPS = 2
GD_STAGGER = 3


def _params(*sem):
    return pltpu.CompilerParams(dimension_semantics=sem, vmem_limit_bytes=V7X_VMEM_LIMIT)


def _sigmoid(x):
    return 0.5 * jnp.tanh(0.5 * x) + 0.5


def _silu(x):
    return x * _sigmoid(x)


def _softplus(x):
    return jnp.maximum(x, 0.0) + jnp.log(1.0 + jnp.exp(-jnp.abs(x)))


def _dot(a, b):
    return jnp.dot(a, b, preferred_element_type=F32)


def _dot_nt(a, b):
    return lax.dot_general(a, b, (((1,), (1,)), ((), ())), preferred_element_type=F32)


def _dot_exact(a, b):
    return jnp.dot(a, b, precision=HIGHEST, preferred_element_type=F32)


def _inproj_kernel(x_ref, g_ref, w_ref, o_ref, h_scr):
    @pl.when(pl.program_id(1) == 0)
    def _():
        x = x_ref[...]
        ms = jnp.mean(x * x, axis=-1, keepdims=True)
        h_scr[...] = (x * lax.rsqrt(ms + RMS_EPS) * g_ref[...]).astype(BF16)

    o_ref[...] = _dot(h_scr[...], w_ref[...]).astype(o_ref.dtype)


def _inproj_ab_kernel(x_ref, g_ref, w_ref, wab_ref, o_ref, oab_ref, h_scr):
    @pl.when(pl.program_id(1) == 0)
    def _():
        x = x_ref[...]
        ms = jnp.mean(x * x, axis=-1, keepdims=True)
        h_scr[...] = (x * lax.rsqrt(ms + RMS_EPS) * g_ref[...]).astype(BF16)
        oab_ref[...] = _dot(h_scr[...], wab_ref[...])

    o_ref[...] = _dot(h_scr[...], w_ref[...]).astype(o_ref.dtype)


def _inproj(x2, gain, w, w_ab=None, *, tm, tn):
    m, d = x2.shape
    n = w.shape[1]
    grid = (m // tm, n // tn)
    x_spec = pl.BlockSpec((tm, d), lambda i, j: (i, 0))
    g_spec = pl.BlockSpec((1, d), lambda i, j: (0, 0))
    w_spec = pl.BlockSpec((d, tn), lambda i, j: (0, j))
    o_spec = pl.BlockSpec((tm, tn), lambda i, j: (i, j))
    scratch = [pltpu.VMEM((tm, d), BF16)]
    if w_ab is None:
        return pl.pallas_call(
            _inproj_kernel, grid=grid, in_specs=[x_spec, g_spec, w_spec], out_specs=o_spec,
            out_shape=jax.ShapeDtypeStruct((m, n), BF16), scratch_shapes=scratch,
            compiler_params=_params("parallel", "arbitrary"), name="inproj",
        )(x2, gain, w)
    nab = w_ab.shape[1]
    return pl.pallas_call(
        _inproj_ab_kernel, grid=grid,
        in_specs=[x_spec, g_spec, w_spec, pl.BlockSpec((d, nab), lambda i, j: (0, 0))],
        out_specs=[o_spec, pl.BlockSpec((tm, nab), lambda i, j: (i, 0))],
        out_shape=[jax.ShapeDtypeStruct((m, n), BF16), jax.ShapeDtypeStruct((m, nab), F32)],
        scratch_shapes=scratch, compiler_params=_params("parallel", "arbitrary"), name="inproj_ab",
    )(x2, gain, w, w_ab)


def _outproj_kernel(ma_ref, mb_ref, wa_ref, wb_ref, g_ref, x_ref, o_ref):
    y = _dot(ma_ref[...], wa_ref[...]) + _dot(mb_ref[...], wb_ref[...])
    ms = jnp.mean(y * y, axis=-1, keepdims=True)
    o_ref[...] = x_ref[...] + y * lax.rsqrt(ms + RMS_EPS) * g_ref[...]


def _outproj(mix_a, mix_b, w_a, w_b, gain, x2, *, tm):
    m, d = x2.shape
    k = mix_a.shape[1]
    return pl.pallas_call(
        _outproj_kernel, grid=(m // tm,),
        in_specs=[pl.BlockSpec((tm, k), lambda i: (i, 0)), pl.BlockSpec((tm, k), lambda i: (i, 0)),
                  pl.BlockSpec((k, d), lambda i: (0, 0)), pl.BlockSpec((k, d), lambda i: (0, 0)),
                  pl.BlockSpec((1, d), lambda i: (0, 0)), pl.BlockSpec((tm, d), lambda i: (i, 0))],
        out_specs=pl.BlockSpec((tm, d), lambda i: (i, 0)),
        out_shape=jax.ShapeDtypeStruct((m, d), F32),
        compiler_params=_params("parallel"), name="outproj",
    )(mix_a, mix_b, w_a, w_b, gain, x2)


def _hgrn_constants(c):
    n = c // 2
    t = np.arange(n)
    masks = [np.eye(n, dtype=np.float32)]
    b = 1
    while b < n:
        mid = (t // (2 * b)) * (2 * b) + b
        same = (t[:, None] // (2 * b)) == (t[None, :] // (2 * b))
        masks.append((same & (t[:, None] >= mid[:, None]) & (t[None, :] < mid[:, None])).astype(np.float32))
        b *= 2
    rows = np.arange(c)
    tri = (rows[None, :] <= rows[:, None]).astype(np.float32)
    return tri, np.stack(masks, axis=0)


HG_PAD = 8


def _hgrn_kernel(q_ref, f_ref, i_ref, z_ref, lbl_ref, gain_ref, tri_ref, mask_ref, o_ref, *scratch,
                 layer_j, c, heads):
    @pl.when(pl.program_id(2) == 0)
    def _():
        for hd in range(heads):
            scratch[3 * hd][...] = jnp.zeros((HEAD, HEAD), F32)

    _run_staggered([_hgrn_head(slice(hd * HEAD, (hd + 1) * HEAD), q_ref, f_ref, i_ref, z_ref, lbl_ref, gain_ref,
                               tri_ref, mask_ref, o_ref, *scratch[3 * hd:3 * hd + 3], layer_j=layer_j, c=c)
                    for hd in range(heads)], HG_STAGGER)


def _hgrn_head(cols, q_ref, f_ref, i_ref, z_ref, lbl_ref, gain_ref, tri_ref, mask_ref, o_ref,
               st_scr, a_scr, sh_scr, *, layer_j, c):
    n = c // 2

    logits = lbl_ref[:, cols]
    e = jnp.exp(logits - jnp.max(logits, axis=0, keepdims=True))
    p = e / jnp.sum(e, axis=0, keepdims=True)
    lb = jnp.sum(p[0:layer_j + 1], axis=0, keepdims=True) - p[0:1]

    q = _silu(q_ref[0, :, cols].astype(F32))
    sig = _sigmoid(f_ref[0, :, cols].astype(F32))
    g = jnp.log(jnp.maximum(lb + (1.0 - lb) * sig, GATE_FLOOR))
    k = (1.0 - lb) * (1.0 - sig)
    vb = i_ref[0, :, cols]
    qb = q.astype(BF16)
    kb = k.astype(BF16)
    yield

    a = _dot_const3(tri_ref[...], g)
    zeros_pad = jnp.zeros((HG_PAD, HEAD), F32)
    a_scr[0:HG_PAD, :] = zeros_pad
    a_scr[HG_PAD:HG_PAD + c, :] = a
    yield

    def seg_rows(b, first):
        return jnp.concatenate([jnp.broadcast_to(a_scr[pl.ds(first + j * b, 1), :], (b, HEAD))
                                for j in range(c // b)], axis=0)

    def shifted(x, slot):
        sh_scr[slot, 0:HG_PAD, :] = zeros_pad
        sh_scr[slot, HG_PAD + c:HG_PAD + c + HG_PAD, :] = zeros_pad
        sh_scr[slot, HG_PAD:HG_PAD + c, :] = x
        return lambda d: sh_scr[slot, pl.ds(HG_PAD + d, c), :]

    row = lax.broadcasted_iota(jnp.int32, (c, HEAD), 0)
    g_at = shifted(g, 0)
    odd = (row & 1) == 1
    up = {1: g, 2: g + jnp.where(odd, g_at(-1), 0.0)}
    lo = {1: None, 2: jnp.where(odd, 0.0, g_at(1))}
    up2_at = shifted(up[2], 1)
    suffix2_at = shifted(lo[2] + g, 2)
    pos = row & 3
    up[4] = up[2] + jnp.where(pos == 2, up2_at(-1), jnp.where(pos == 3, up2_at(-2), 0.0))
    lo[4] = lo[2] + jnp.where(pos == 0, suffix2_at(2), jnp.where(pos == 1, suffix2_at(1), 0.0))
    b = 8
    while b <= c:
        up[b] = a - seg_rows(b, HG_PAD - 1)
        lo[b] = seg_rows(b, HG_PAD - 1 + b) - a
        b *= 2
    yield

    def decayed(b):
        qd = (q * jnp.exp(up[b])).astype(BF16)
        kd = kb if lo[b] is None else (k * jnp.exp(lo[b])).astype(BF16)
        return qd, kd

    halves = (slice(0, n), slice(n, c))
    scores = [mask_ref[0] * _dot_nt(qb[h], kb[h]) for h in halves]
    b, lv = 1, 1
    while b < n:
        qd, kd = decayed(b)
        scores = [s + mask_ref[lv] * _dot_nt(qd[h], kd[h]) for s, h in zip(scores, halves)]
        b, lv = 2 * b, lv + 1
        yield
    qd, kd = decayed(n)
    cross = _dot_nt(qd[halves[1]], kd[halves[0]])
    o_top = _dot(scores[0].astype(BF16), vb[halves[0]])
    o_bot = _dot(jnp.concatenate([cross, scores[1]], axis=1).astype(BF16), vb)
    yield
    qd, kd = decayed(c)
    st = st_scr[...]
    o = jnp.concatenate([o_top, o_bot], axis=0) + _dot_nt(qd, st.astype(BF16))
    st_scr[...] = jnp.exp(a[c - 1:c, :]) * st + lax.dot_general(vb, kd, (((0,), (0,)), ((), ())),
                                                               preferred_element_type=F32)

    ms = jnp.mean(o * o, axis=-1, keepdims=True)
    y = o * lax.rsqrt(ms + RMS_EPS) * gain_ref[...]
    o_ref[0, :, cols] = (y * _silu(z_ref[0, :, cols].astype(F32))).astype(o_ref.dtype)
    yield


def _hgrn(proj, lb_logits, gain, layer_j, *, c):
    b, l, _ = proj.shape
    tri, masks = _hgrn_constants(c)
    n_even = lb_logits.shape[0]
    heads = HG_HEADS_PER_STEP
    groups = N_HEADS // heads
    width = heads * HEAD

    def col(seg):
        return pl.BlockSpec((1, c, width), lambda bi, h, t, seg=seg: (bi, t, seg * groups + h))

    kern = functools.partial(_hgrn_kernel, layer_j=layer_j, c=c, heads=heads)
    per_head_scratch = [pltpu.VMEM((HEAD, HEAD), F32), pltpu.VMEM((HG_PAD + c, HEAD), F32),
                        pltpu.VMEM((3, c + 2 * HG_PAD, HEAD), F32)]
    return pl.pallas_call(
        kern, grid=(b, groups, l // c),
        in_specs=[col(0), col(1), col(2), col(3),
                  pl.BlockSpec((n_even, width), lambda bi, h, t: (0, h)),
                  pl.BlockSpec((1, HEAD), lambda bi, h, t: (0, 0)),
                  pl.BlockSpec(tri.shape, lambda bi, h, t: (0, 0)),
                  pl.BlockSpec(masks.shape, lambda bi, h, t: (0, 0, 0))],
        out_specs=pl.BlockSpec((1, c, width), lambda bi, h, t: (bi, t, h)),
        out_shape=jax.ShapeDtypeStruct((b, l, BRANCH), BF16),
        scratch_shapes=per_head_scratch * heads,
        compiler_params=_params("parallel", "parallel", "arbitrary"), name="hgrn2",
    )(proj, proj, proj, proj, lb_logits, gain, jnp.asarray(tri, BF16), jnp.asarray(masks))


def _diff_kernel(q_ref, k_ref, v_ref, z_ref, lamv_ref, gain_ref, causal_ref, o_ref, vt_scr, sa_scr, sb_scr,
                 *, lam_init, tq, tk):
    qi = pl.program_id(2)
    n_trips = k_ref.shape[1] // tk

    @pl.when(qi == 0)
    def _():
        def transpose_block(i, carry):
            rows = pl.ds(pl.multiple_of(i * tk, tk), tk)
            vt_scr[i] = v_ref[0, rows, :].astype(F32).T.astype(BF16)
            return carry
        lax.fori_loop(0, n_trips, transpose_block, 0)

    lv = lamv_ref[...]
    d01 = jnp.sum(jnp.sum(lv[0:1] * lv[1:2], axis=-1, keepdims=True), axis=0, keepdims=True)
    d23 = jnp.sum(jnp.sum(lv[2:3] * lv[3:4], axis=-1, keepdims=True), axis=0, keepdims=True)
    lam = jnp.exp(d01) - jnp.exp(d23) + lam_init

    q = q_ref[0].astype(F32) * (DF_DH ** -0.5 * math.log2(math.e))
    lane = lax.broadcasted_iota(jnp.int32, q.shape, 1)
    zero = jnp.zeros_like(q)
    halves = (jnp.where(lane < DF_DH, q, zero).astype(BF16), jnp.where(lane >= DF_DH, q, zero).astype(BF16))

    parts = DIFF_KEY_PARTS
    rows_per_part = tk // parts

    def scores_part(t, s_buf, hf, kp, diagonal=False):
        start = pl.multiple_of(t * tk + kp * rows_per_part, rows_per_part)
        s = _dot_nt(k_ref[0, pl.ds(start, rows_per_part), :], halves[hf])
        if diagonal:
            s = s + causal_ref[kp * rows_per_part:(kp + 1) * rows_per_part, :]
        s_buf[hf, kp * rows_per_part:(kp + 1) * rows_per_part, :] = s
        return jnp.max(s, axis=0, keepdims=True)

    def rescale(st, top):
        m, l, acc = st
        m_new = jnp.maximum(m, top)
        alpha = jnp.exp2(m - m_new)
        return m_new, alpha * l, alpha * acc

    def absorb_part(t, s_buf, hf, kp, st):
        m, l, acc = st
        lanes = slice(kp * rows_per_part, (kp + 1) * rows_per_part)
        p = jnp.exp2(s_buf[hf, lanes, :] - m)
        return m, l + jnp.sum(p, axis=0, keepdims=True), acc + _dot(vt_scr[t, :, lanes], p.astype(BF16))

    def scores(t, s_buf, diagonal=False):
        return tuple(functools.reduce(jnp.maximum, [scores_part(t, s_buf, hf, kp, diagonal) for kp in range(parts)])
                     for hf in range(2))

    def absorb(t, s_buf, state, tops):
        out = []
        for hf in range(2):
            st = rescale(state[hf], tops[hf])
            for kp in range(parts):
                st = absorb_part(t, s_buf, hf, kp, st)
            out.append(st)
        return tuple(out)

    def step(t_next, buf_next, t_cur, buf_cur, state, tops_cur):
        tops_next, out = [], []
        for hf in range(2):
            st = rescale(state[hf], tops_cur[hf])
            top = None
            for kp in range(parts):
                piece = scores_part(t_next, buf_next, hf, kp)
                top = piece if top is None else jnp.maximum(top, piece)
                st = absorb_part(t_cur, buf_cur, hf, kp, st)
            tops_next.append(top)
            out.append(st)
        return tuple(out), tuple(tops_next)

    n_full = qi
    state = tuple((jnp.full((1, tq), NEG, F32), jnp.zeros((1, tq), F32), jnp.zeros((HEAD, tq), F32))
                  for _ in halves)
    tops = scores(n_full, sa_scr, diagonal=True)

    def visits(v, state, tops_a, n_pairs):
        for _ in range(n_pairs):
            state, tops_b = step(n_full - (v + 1), sb_scr, n_full - v, sa_scr, state, tops_a)
            state, tops_a = step(n_full - (v + 2), sa_scr, n_full - (v + 1), sb_scr, state, tops_b)
            v = v + 2
        return state, tops_a

    n_quads = n_full // 4
    state, tops = lax.fori_loop(0, n_quads, lambda i, c: visits(4 * i, c[0], c[1], 2), (state, tops))
    n_pairs = (n_full - 4 * n_quads) // 2
    state, tops = lax.fori_loop(0, n_pairs, lambda i, c: visits(4 * n_quads + 2 * i, c[0], c[1], 1),
                                (state, tops))

    def finish(state):
        (_, l1, acc1), (_, l2, acc2) = state
        o = (acc1 / l1 - lam * (acc2 / l2)).T
        ms = jnp.mean(o * o, axis=-1, keepdims=True)
        y = o * lax.rsqrt(ms + RMS_EPS) * gain_ref[...]
        o_ref[0] = (y * (1.0 - lam_init) * _silu(z_ref[0].astype(F32))).astype(o_ref.dtype)

    @pl.when(n_full % 2 == 0)
    def _():
        finish(absorb(0, sa_scr, state, tops))

    @pl.when(n_full % 2 == 1)
    def _():
        mid, tops_b = step(0, sb_scr, 1, sa_scr, state, tops)
        finish(absorb(0, sb_scr, mid, tops_b))


def _diff_attn(proj, lam_vec, gain, lam_init, *, tq):
    b, l, _ = proj.shape

    def qcol(seg):
        return pl.BlockSpec((1, tq, HEAD), lambda bi, h, t, seg=seg: (bi, t, seg * N_HEADS + h))

    def kvcol(seg):
        return pl.BlockSpec((1, l, HEAD), lambda bi, h, t, seg=seg: (bi, 0, seg * N_HEADS + h))

    tk = tq
    key = np.arange(tk)[:, None]
    causal = np.where(key <= np.arange(tq)[None, :], 0.0, NEG).astype(np.float32)
    kern = functools.partial(_diff_kernel, lam_init=lam_init, tq=tq, tk=tk)
    return pl.pallas_call(
        kern, grid=(b, N_HEADS, l // tq),
        in_specs=[qcol(4), kvcol(5), kvcol(6), qcol(7),
                  pl.BlockSpec(lam_vec.shape, lambda bi, h, t: (0, 0)),
                  pl.BlockSpec((1, HEAD), lambda bi, h, t: (0, 0)),
                  pl.BlockSpec((tk, tq), lambda bi, h, t: (0, 0))],
        out_specs=pl.BlockSpec((1, tq, HEAD), lambda bi, h, t: (bi, t, h)),
        out_shape=jax.ShapeDtypeStruct((b, l, BRANCH), BF16),
        scratch_shapes=[pltpu.VMEM((l // tk, HEAD, tk), BF16), pltpu.VMEM((2, tk, tq), F32), pltpu.VMEM((2, tk, tq), F32)],
        compiler_params=_params("parallel", "parallel", "arbitrary"), name="diff_attn",
    )(proj, proj, proj, proj, lam_vec, gain, jnp.asarray(causal))


def _sb_kernel(q_ref, k_ref, v_ref, z_ref, o_ref, *, tq, heads):
    qi = pl.program_id(2)
    cols = [slice(hd * HEAD, (hd + 1) * HEAD) for hd in range(heads)]
    q = [(q_ref[0, :, c].astype(F32) * (HEAD ** -0.5)).astype(BF16) for c in cols]
    r = lax.broadcasted_iota(jnp.int32, (tq, tq), 0)
    cc = lax.broadcasted_iota(jnp.int32, (tq, tq), 1)
    later = (r > cc).astype(BF16)

    def local(kj, masked):
        rows = pl.ds(pl.multiple_of(kj * tq, tq), tq)
        zz = [_dot_nt(qh, k_ref[0, rows, c]) for qh, c in zip(q, cols)]
        sp = [_softplus(x) for x in zz]
        log_rest = [-x for x in sp]
        if masked:
            log_rest = [jnp.where(cc < r, x, 0.0) for x in log_rest]
        hi = [x.astype(BF16) for x in log_rest]
        lo = [(x - h.astype(F32)).astype(BF16) for x, h in zip(log_rest, hi)]
        inner = [z - s + (_dot(h, later) + _dot(lw, later)) for z, s, h, lw in zip(zz, sp, hi, lo)]
        return inner, [jnp.sum(x, axis=-1, keepdims=True) for x in log_rest], rows

    def weigh(part, run, masked):
        inner, total, rows = part
        w = [jnp.exp(x + rn) for x, rn in zip(inner, run)]
        if masked:
            w = [jnp.where(cc < r, x, 0.0) for x in w]
        pv = [_dot(x.astype(BF16), v_ref[0, rows, c]) for x, c in zip(w, cols)]
        return pv, [rn + t for rn, t in zip(run, total)]

    def alive(run):
        top = run[0]
        for rn in run[1:]:
            top = jnp.maximum(top, rn)
        return jnp.max(top) > -SB_SKIP

    diag = local(qi, True)
    prev = local(jnp.maximum(qi - 1, 0), False)
    acc, run = weigh(diag, [jnp.zeros((tq, 1), F32)] * heads, True)
    has_prev = (qi > 0).astype(F32)
    pv, run_prev = weigh(prev, run, False)
    acc = [a + has_prev * p for a, p in zip(acc, pv)]
    run = [rn + has_prev * (rp - rn) for rn, rp in zip(run, run_prev)]

    def cond(carry):
        kj, go, _, _ = carry
        return jnp.logical_and(kj >= 0, go)

    def body(carry):
        kj, _, run, acc = carry
        pv, run = weigh(local(kj, False), run, False)
        return kj - 1, alive(run), run, [a + p for a, p in zip(acc, pv)]

    _, _, _, acc = lax.while_loop(cond, body, (qi - 2, alive(run), run, acc))
    for a, c in zip(acc, cols):
        o_ref[0, :, c] = (a * _silu(z_ref[0, :, c].astype(F32))).astype(o_ref.dtype)


def _stick_breaking(proj, *, tq):
    b, l, _ = proj.shape
    heads = SB_HEADS_PER_STEP
    groups = N_HEADS // heads
    width = heads * HEAD

    def qcol(seg):
        return pl.BlockSpec((1, tq, width), lambda bi, h, t, seg=seg: (bi, t, seg * groups + h))

    def kvcol(seg):
        return pl.BlockSpec((1, l, width), lambda bi, h, t, seg=seg: (bi, 0, seg * groups + h))

    return pl.pallas_call(
        functools.partial(_sb_kernel, tq=tq, heads=heads), grid=(b, groups, l // tq),
        in_specs=[qcol(4), kvcol(5), kvcol(6), qcol(7)],
        out_specs=pl.BlockSpec((1, tq, width), lambda bi, h, t: (bi, t, h)),
        out_shape=jax.ShapeDtypeStruct((b, l, BRANCH), BF16),
        compiler_params=_params("parallel", "parallel", "arbitrary"), name="stick_breaking",
    )(proj, proj, proj, proj)


def _split2(x):
    hi = x.astype(BF16)
    return hi, (x - hi.astype(F32)).astype(BF16)


def _dot3(a, b):
    a_hi, a_lo = _split2(a)
    b_hi, b_lo = _split2(b)
    return _dot(a_hi, b_hi) + (_dot(a_hi, b_lo) + _dot(a_lo, b_hi))


def _dot_const3(const_bf16, x):
    hi = x.astype(BF16)
    r1 = x - hi.astype(F32)
    mid = r1.astype(BF16)
    lo = (r1 - mid.astype(F32)).astype(BF16)
    return _dot(const_bf16, hi) + (_dot(const_bf16, mid) + _dot(const_bf16, lo))


def _each(fn, *lists):
    return [fn(*args) for args in zip(*lists)]


def _unit_lower_inverse_steps(lws, eye, blockdiag, out):
    ld = [lw * blockdiag for lw in lws]
    lo = _each(lambda a, b: a - b, lws, ld)
    td = [eye - x for x in ld]
    p = _each(_dot3, ld, ld)
    yield
    n_factors = int(math.log2(INV_BLOCK)) - 1
    for i in range(n_factors):
        td = _each(lambda a, b: a + _dot3(a, b), td, p)
        if i < n_factors - 1:
            p = _each(_dot3, p, p)
        yield
    n = _each(_dot3, td, lo)
    yield
    n2 = _each(_dot3, n, n)
    yield
    y = [eye - x for x in n]
    y = _each(lambda a, b: a + _dot3(a, b), y, n2)
    yield
    out.extend(_each(_dot3, y, td))
    yield


_DONE = object()


def _run_staggered(chains, lag):
    live = list(chains)
    for ahead, chain in enumerate(reversed(live)):
        for _ in range(ahead * lag):
            next(chain, None)
    while live:
        live = [chain for chain in live if next(chain, _DONE) is not _DONE]


def _gdn_prep_kernel(q_ref, k_ref, v_ref, qh_ref, kh_ref, vh_ref, ab_ref, wq_ref, wk_ref, wv_ref, alog_ref, dtb_ref,
                     u_ref, w_ref, qg_ref, kd_ref, at_ref, gl_ref, ext_scr, *, n_chunks, c):
    h = pl.program_id(1)
    t = pl.program_id(2)
    t_blk = n_chunks * c

    for idx, (x_ref, halo_ref) in enumerate(((q_ref, qh_ref), (k_ref, kh_ref), (v_ref, vh_ref))):
        ext_scr[idx, 0:GD_HALO, :] = jnp.where(t > 0, halo_ref[0].astype(F32), 0.0)
        ext_scr[idx, GD_HALO:GD_HALO + t_blk, :] = x_ref[0].astype(F32)

    def conv(w_ref, idx, row0, n_rows):
        w = w_ref[...]
        y = w[GD_CONV - 1:GD_CONV] * ext_scr[idx, pl.ds(GD_HALO + row0, n_rows), :]
        for tap in range(GD_CONV - 1):
            y = y + w[tap:tap + 1] * ext_scr[idx, pl.ds(GD_HALO - (GD_CONV - 1) + tap + row0, n_rows), :]
        return _silu(y)

    def l2n(x):
        return x * lax.rsqrt(jnp.sum(x * x, axis=-1, keepdims=True) + RMS_EPS)

    r = lax.broadcasted_iota(jnp.int32, (c, c), 0)
    cc = lax.broadcasted_iota(jnp.int32, (c, c), 1)
    incl = cc <= r
    strict = cc < r
    tri = incl.astype(BF16)
    eye = (cc == r).astype(F32)
    blockdiag = ((r // INV_BLOCK) == (cc // INV_BLOCK)).astype(F32)
    r2 = lax.broadcasted_iota(jnp.int32, (c, HEAD), 0)
    c2 = lax.broadcasted_iota(jnp.int32, (c, HEAD), 1)
    strict_wide = jnp.logical_and(c2 < r2, c2 < c).astype(F32)

    def chain(ids):
        row0, n_rows = ids[0] * c, len(ids) * c
        local = [slice(i * c, (i + 1) * c) for i in range(len(ids))]
        chunks = [slice(ci * c, (ci + 1) * c) for ci in ids]
        q_rows = l2n(conv(wq_ref, 0, row0, n_rows)) * (HEAD ** -0.5)
        yield
        k_rows = l2n(conv(wk_ref, 1, row0, n_rows))
        yield
        v_rows = conv(wv_ref, 2, row0, n_rows)
        ab = ab_ref[0, row0:row0 + n_rows, :]
        lane = lax.broadcasted_iota(jnp.int32, ab.shape, 1)
        g_lanes = -jnp.exp(alog_ref[...]) * _softplus(ab + dtb_ref[...])
        g_rows = jnp.sum(jnp.where(lane == h, g_lanes, 0.0), axis=-1, keepdims=True)
        beta_rows = jnp.sum(jnp.where(lane == h + N_HEADS, _sigmoid(ab), 0.0), axis=-1, keepdims=True)
        yield
        q = [q_rows[sl] for sl in local]
        k = [k_rows[sl] for sl in local]
        v = [v_rows[sl] for sl in local]
        beta = [beta_rows[sl] for sl in local]
        g_wide = [jnp.broadcast_to(g_rows[sl], (c, HEAD)) for sl in local]
        sums = [_dot_const3(tri, jnp.concatenate([gw, gw * strict_wide], axis=1)) for gw in g_wide]
        yield
        gc = [s[:, 0:HEAD] for s in sums]
        decay = [jnp.exp(jnp.where(incl, s[:, HEAD:HEAD + c], NEG)) for s in sums]
        kb = _each(lambda a, b: a * b, k, beta)
        kbf = [x.astype(BF16) for x in k]
        lw = _each(lambda a, b, d: jnp.where(strict, _dot_nt(a.astype(BF16), b) * d, 0.0), kb, kbf, decay)
        attn = _each(lambda a, b, d: _dot_nt(a.astype(BF16), b) * d, q, kbf, decay)
        yield
        inverse = []
        yield from _unit_lower_inverse_steps(lw, eye, blockdiag, inverse)
        tinv = [x.astype(BF16) for x in inverse]
        egc = [jnp.exp(x) for x in gc]
        u = _each(lambda ti, a, b: _dot(ti, (a * b).astype(BF16)), tinv, v, beta)
        w = _each(lambda ti, a, e: _dot(ti, (a * e).astype(BF16)), tinv, kb, egc)
        for i, (ci, sl) in enumerate(zip(ids, chunks)):
            gc_last = gc[i][c - 1:c, :]
            u_ref[0, 0, sl, :] = u[i]
            w_ref[0, 0, sl, :] = w[i].astype(BF16)
            qg_ref[0, 0, sl, :] = (q[i] * egc[i]).astype(BF16)
            kd_ref[0, 0, sl, :] = (k[i] * jnp.exp(gc_last - gc[i])).astype(BF16)
            at_ref[0, 0, sl, :] = attn[i].astype(BF16)
            gl_ref[0, 0, ci] = jnp.exp(gc_last)
        yield

    n_groups = min(GD_GROUPS, n_chunks)
    per_group = n_chunks // n_groups
    _run_staggered([chain(range(gi * per_group, (gi + 1) * per_group)) for gi in range(n_groups)], GD_STAGGER)


def _gdn_scan_kernel(u_ref, w_ref, qg_ref, kd_ref, at_ref, gl_ref, z_ref, gain_ref, o_ref, s_scr, *, n_chunks, c):
    @pl.when(pl.program_id(1) == 0)
    def _():
        s_scr[...] = jnp.zeros_like(s_scr)

    heads = range(N_HEADS)
    gain = gain_ref[...]
    s = [s_scr[h] for h in heads]
    for ci in range(n_chunks):
        rows = pl.ds(ci * c, c)
        sb = [x.astype(BF16) for x in s]
        v_new = [u_ref[0, h, rows, :] - _dot(w_ref[0, h, rows, :], sb[h]) for h in heads]
        vb = [x.astype(BF16) for x in v_new]
        o = [_dot(qg_ref[0, h, rows, :], sb[h]) + _dot(at_ref[0, h, rows, :], vb[h]) for h in heads]
        s = [gl_ref[0, h, ci] * s[h]
             + lax.dot_general(kd_ref[0, h, rows, :], vb[h], (((0,), (0,)), ((), ())), preferred_element_type=F32)
             for h in heads]
        for h in heads:
            cols = slice(h * HEAD, (h + 1) * HEAD)
            ms = jnp.mean(o[h] * o[h], axis=-1, keepdims=True)
            y = o[h] * lax.rsqrt(ms + RMS_EPS) * gain
            o_ref[0, rows, cols] = (y * _silu(z_ref[0, rows, cols].astype(F32))).astype(o_ref.dtype)
    for h in heads:
        s_scr[h] = s[h]


def _gdn(proj, ab, conv_w, a_log, dt_bias, gain, *, t_prep, t_scan, c):
    b, l, _ = proj.shape
    halo_blocks = t_prep // GD_HALO

    def col(seg):
        return pl.BlockSpec((1, t_prep, HEAD), lambda bi, h, t, seg=seg: (bi, t, seg * N_HEADS + h))

    def halo(seg):
        return pl.BlockSpec((1, GD_HALO, HEAD),
                            lambda bi, h, t, seg=seg: (bi, jnp.maximum(t * halo_blocks - 1, 0), seg * N_HEADS + h))

    def wcol(seg):
        return pl.BlockSpec((GD_CONV, HEAD), lambda bi, h, t, seg=seg: (0, seg * N_HEADS + h))

    one = pl.BlockSpec((1, HEAD), lambda bi, h, t: (0, 0))

    def per_head(width):
        return pl.BlockSpec((1, 1, t_prep, width), lambda bi, h, t: (bi, h, t, 0))

    n_prep = t_prep // c
    u, w, qg, kd, at, gl = pl.pallas_call(
        functools.partial(_gdn_prep_kernel, n_chunks=n_prep, c=c), grid=(b, N_HEADS, l // t_prep),
        in_specs=[col(0), col(1), col(2), halo(0), halo(1), halo(2),
                  pl.BlockSpec((1, t_prep, HEAD), lambda bi, h, t: (bi, t, 0)),
                  wcol(0), wcol(1), wcol(2), one, one],
        out_specs=[per_head(HEAD), per_head(HEAD), per_head(HEAD), per_head(HEAD), per_head(c),
                   pl.BlockSpec((1, 1, n_prep, 1, HEAD), lambda bi, h, t: (bi, h, t, 0, 0))],
        out_shape=[jax.ShapeDtypeStruct((b, N_HEADS, l, HEAD), F32),
                   jax.ShapeDtypeStruct((b, N_HEADS, l, HEAD), BF16),
                   jax.ShapeDtypeStruct((b, N_HEADS, l, HEAD), BF16),
                   jax.ShapeDtypeStruct((b, N_HEADS, l, HEAD), BF16),
                   jax.ShapeDtypeStruct((b, N_HEADS, l, c), BF16),
                   jax.ShapeDtypeStruct((b, N_HEADS, l // c, 1, HEAD), F32)],
        scratch_shapes=[pltpu.VMEM((3, t_prep + GD_HALO, HEAD), F32)],
        compiler_params=_params("parallel", "parallel", "parallel"), name="gdn_prep",
    )(proj, proj, proj, proj, proj, proj, ab, conv_w, conv_w, conv_w, a_log, dt_bias)

    def all_heads(width):
        return pl.BlockSpec((1, N_HEADS, t_scan, width), lambda bi, t: (bi, 0, t, 0))

    n_scan = t_scan // c
    return pl.pallas_call(
        functools.partial(_gdn_scan_kernel, n_chunks=n_scan, c=c), grid=(b, l // t_scan),
        in_specs=[all_heads(HEAD), all_heads(HEAD), all_heads(HEAD), all_heads(HEAD), all_heads(c),
                  pl.BlockSpec((1, N_HEADS, n_scan, 1, HEAD), lambda bi, t: (bi, 0, t, 0, 0)),
                  pl.BlockSpec((1, t_scan, BRANCH), lambda bi, t: (bi, t, 3)),
                  pl.BlockSpec((1, HEAD), lambda bi, t: (0, 0))],
        out_specs=pl.BlockSpec((1, t_scan, BRANCH), lambda bi, t: (bi, t, 0)),
        out_shape=jax.ShapeDtypeStruct((b, l, BRANCH), BF16),
        scratch_shapes=[pltpu.VMEM((N_HEADS, HEAD, HEAD), F32)],
        compiler_params=_params("parallel", "arbitrary"), name="gdn_scan",
    )(u, w, qg, kd, at, gl, proj, gain)


def _pad_lanes(vec):
    return jnp.zeros((1, HEAD), F32).at[0, :vec.shape[0]].set(vec.astype(F32))


def kernel(x, norm_pre, norm_post, ev_w_in, ev_w_out, hg_lb_logits, hg_norm, df_lambda, df_norm,
           od_w_in, od_w_out, gd_conv, gd_a_log, gd_dt_bias, gd_norm):
    p = dict(norm_pre=norm_pre, norm_post=norm_post, ev_w_in=ev_w_in, ev_w_out=ev_w_out,
             hg_lb_logits=hg_lb_logits, hg_norm=hg_norm, df_lambda=df_lambda, df_norm=df_norm,
             od_w_in=od_w_in, od_w_out=od_w_out, gd_conv=gd_conv, gd_a_log=gd_a_log,
             gd_dt_bias=gd_dt_bias, gd_norm=gd_norm)
    return _run_layers(x, p, range(norm_pre.shape[0]))


def _run_layers(x, p, layers):
    norm_pre, norm_post = p["norm_pre"], p["norm_post"]
    ev_w_in, ev_w_out, od_w_in, od_w_out = p["ev_w_in"], p["ev_w_out"], p["od_w_in"], p["od_w_out"]
    hg_lb_logits, hg_norm, df_lambda, df_norm = p["hg_lb_logits"], p["hg_norm"], p["df_lambda"], p["df_norm"]
    gd_conv, gd_a_log, gd_dt_bias, gd_norm = p["gd_conv"], p["gd_a_log"], p["gd_dt_bias"], p["gd_norm"]
    b, l, d = x.shape
    m = b * l
    tm_in = min(1024, m)
    tm_out = min(512, m)
    t_rec = min(REC_BLOCK, l)
    t_att = min(ATT_BLOCK, l)
    x2 = x.reshape(m, d)
    for layer in layers:
        j = layer // 2
        g_pre = norm_pre[layer].reshape(1, d)
        g_post = norm_post[layer].reshape(1, d)
        if layer % 2 == 0:
            proj = _inproj(x2, g_pre, ev_w_in[j].astype(BF16), tm=tm_in, tn=IN_TILE_N).reshape(b, l, -1)
            lam_init = 0.8 - 0.6 * math.exp(-0.3 * layer)
            mix_a = _hgrn(proj, hg_lb_logits.astype(F32), hg_norm[j].reshape(1, HEAD), j, c=min(HG_CHUNK, l))
            mix_b = _diff_attn(proj, df_lambda[j].astype(F32), df_norm[j].reshape(1, HEAD), lam_init,
                               tq=min(DIFF_Q_BLOCK, l))
            w_out = ev_w_out[j].astype(BF16)
        else:
            w = od_w_in[j]
            w_main = jnp.concatenate([w[:, :4 * BRANCH], w[:, 4 * BRANCH + 2 * N_HEADS:]], axis=1).astype(BF16)
            w_ab = jnp.pad(w[:, 4 * BRANCH:4 * BRANCH + 2 * N_HEADS], ((0, 0), (0, HEAD - 2 * N_HEADS))).astype(BF16)
            proj, ab = _inproj(x2, g_pre, w_main, w_ab, tm=tm_in, tn=IN_TILE_N)
            proj = proj.reshape(b, l, -1)
            mix_a = _gdn(proj, ab.reshape(b, l, HEAD), gd_conv[j].astype(F32), _pad_lanes(gd_a_log[j]),
                         _pad_lanes(gd_dt_bias[j]), gd_norm[j].reshape(1, HEAD), t_prep=min(GD_PREP_BLOCK, l), t_scan=t_rec, c=CHUNK)
            mix_b = _stick_breaking(proj, tq=t_att)
            w_out = od_w_out[j].astype(BF16)
        x2 = _outproj(mix_a.reshape(m, BRANCH), mix_b.reshape(m, BRANCH), w_out[:BRANCH], w_out[BRANCH:],
                      g_post, x2, tm=tm_out)
    return x2.reshape(b, l, d)
```

```python
import functools
import math

import numpy as np
import jax
import jax.numpy as jnp
from jax import lax
from jax.experimental import pallas as pl
from jax.experimental.pallas import tpu as pltpu

F32 = jnp.float32
BF16 = jnp.bfloat16
HIGHEST = lax.Precision.HIGHEST

D_MODEL = 2048
BRANCH = D_MODEL // 2
N_HEADS = 8
HEAD = BRANCH // N_HEADS
DF_DH = HEAD // 2
GD_CONV = 4
RMS_EPS = 1e-6
NEG = -1e30
GATE_FLOOR = 1e-20
SB_SKIP = 104.0

V7X_VMEM_LIMIT = 56 * 1024 * 1024

IN_TILE_N = 2048
CHUNK = 64
HG_CHUNK = 256
HG_HEADS_PER_STEP = 8
HG_STAGGER = 5
REC_BLOCK = 256
ATT_BLOCK = 256
SB_HEADS_PER_STEP = 4
DIFF_Q_BLOCK = 512
DIFF_KEY_PARTS = 2
INV_BLOCK = 16
GD_PREP_BLOCK = 2048
GD_HALO = 16
GD_SCAN_BATCH_ROWS = 2
GD_GROUPS = 4
GD_STAGGER = 2


def _params(*sem):
    return pltpu.CompilerParams(dimension_semantics=sem, vmem_limit_bytes=V7X_VMEM_LIMIT)


def _sigmoid(x):
    return 0.5 * jnp.tanh(0.5 * x) + 0.5


def _silu(x):
    return x * _sigmoid(x)


def _softplus(x):
    return jnp.maximum(x, 0.0) + jnp.log(1.0 + jnp.exp(-jnp.abs(x)))


def _dot(a, b):
    return jnp.dot(a, b, preferred_element_type=F32)


def _dot_nt(a, b):
    return lax.dot_general(a, b, (((1,), (1,)), ((), ())), preferred_element_type=F32)


def _dot_exact(a, b):
    return jnp.dot(a, b, precision=HIGHEST, preferred_element_type=F32)


def _inproj_kernel(x_ref, g_ref, w_ref, o_ref, h_scr):
    @pl.when(pl.program_id(1) == 0)
    def _():
        x = x_ref[...]
        ms = jnp.mean(x * x, axis=-1, keepdims=True)
        h_scr[...] = (x * lax.rsqrt(ms + RMS_EPS) * g_ref[...]).astype(BF16)

    o_ref[...] = _dot(h_scr[...], w_ref[...]).astype(o_ref.dtype)


def _inproj_ab_kernel(x_ref, g_ref, w_ref, wab_ref, o_ref, oab_ref, h_scr):
    @pl.when(pl.program_id(1) == 0)
    def _():
        x = x_ref[...]
        ms = jnp.mean(x * x, axis=-1, keepdims=True)
        h_scr[...] = (x * lax.rsqrt(ms + RMS_EPS) * g_ref[...]).astype(BF16)
        oab_ref[...] = _dot(h_scr[...], wab_ref[...])

    o_ref[...] = _dot(h_scr[...], w_ref[...]).astype(o_ref.dtype)


def _inproj(x2, gain, w, w_ab=None, *, tm, tn):
    m, d = x2.shape
    n = w.shape[1]
    grid = (m // tm, n // tn)
    x_spec = pl.BlockSpec((tm, d), lambda i, j: (i, 0))
    g_spec = pl.BlockSpec((1, d), lambda i, j: (0, 0))
    w_spec = pl.BlockSpec((d, tn), lambda i, j: (0, j))
    o_spec = pl.BlockSpec((tm, tn), lambda i, j: (i, j))
    scratch = [pltpu.VMEM((tm, d), BF16)]
    if w_ab is None:
        return pl.pallas_call(
            _inproj_kernel, grid=grid, in_specs=[x_spec, g_spec, w_spec], out_specs=o_spec,
            out_shape=jax.ShapeDtypeStruct((m, n), BF16), scratch_shapes=scratch,
            compiler_params=_params("parallel", "arbitrary"), name="inproj",
        )(x2, gain, w)
    nab = w_ab.shape[1]
    return pl.pallas_call(
        _inproj_ab_kernel, grid=grid,
        in_specs=[x_spec, g_spec, w_spec, pl.BlockSpec((d, nab), lambda i, j: (0, 0))],
        out_specs=[o_spec, pl.BlockSpec((tm, nab), lambda i, j: (i, 0))],
        out_shape=[jax.ShapeDtypeStruct((m, n), BF16), jax.ShapeDtypeStruct((m, nab), F32)],
        scratch_shapes=scratch, compiler_params=_params("parallel", "arbitrary"), name="inproj_ab",
    )(x2, gain, w, w_ab)


def _outproj_kernel(ma_ref, mb_ref, wa_ref, wb_ref, g_ref, x_ref, o_ref):
    y = _dot(ma_ref[...], wa_ref[...]) + _dot(mb_ref[...], wb_ref[...])
    ms = jnp.mean(y * y, axis=-1, keepdims=True)
    o_ref[...] = x_ref[...] + y * lax.rsqrt(ms + RMS_EPS) * g_ref[...]


def _outproj(mix_a, mix_b, w_a, w_b, gain, x2, *, tm):
    m, d = x2.shape
    k = mix_a.shape[1]
    return pl.pallas_call(
        _outproj_kernel, grid=(m // tm,),
        in_specs=[pl.BlockSpec((tm, k), lambda i: (i, 0)), pl.BlockSpec((tm, k), lambda i: (i, 0)),
                  pl.BlockSpec((k, d), lambda i: (0, 0)), pl.BlockSpec((k, d), lambda i: (0, 0)),
                  pl.BlockSpec((1, d), lambda i: (0, 0)), pl.BlockSpec((tm, d), lambda i: (i, 0))],
        out_specs=pl.BlockSpec((tm, d), lambda i: (i, 0)),
        out_shape=jax.ShapeDtypeStruct((m, d), F32),
        compiler_params=_params("parallel"), name="outproj",
    )(mix_a, mix_b, w_a, w_b, gain, x2)


def _hgrn_constants(c):
    n = c // 2
    t = np.arange(n)
    masks = [np.eye(n, dtype=np.float32)]
    b = 1
    while b < n:
        mid = (t // (2 * b)) * (2 * b) + b
        same = (t[:, None] // (2 * b)) == (t[None, :] // (2 * b))
        masks.append((same & (t[:, None] >= mid[:, None]) & (t[None, :] < mid[:, None])).astype(np.float32))
        b *= 2
    rows = np.arange(c)
    tri = (rows[None, :] <= rows[:, None]).astype(np.float32)
    return tri, np.stack(masks, axis=0)


HG_PAD = 8


def _hgrn_kernel(q_ref, f_ref, i_ref, z_ref, lbl_ref, gain_ref, tri_ref, mask_ref, o_ref, *scratch,
                 layer_j, c, heads):
    @pl.when(pl.program_id(2) == 0)
    def _():
        for hd in range(heads):
            scratch[3 * hd][...] = jnp.zeros((HEAD, HEAD), F32)

    _run_staggered([_hgrn_head(slice(hd * HEAD, (hd + 1) * HEAD), q_ref, f_ref, i_ref, z_ref, lbl_ref, gain_ref,
                               tri_ref, mask_ref, o_ref, *scratch[3 * hd:3 * hd + 3], layer_j=layer_j, c=c)
                    for hd in range(heads)], HG_STAGGER)


def _hgrn_head(cols, q_ref, f_ref, i_ref, z_ref, lbl_ref, gain_ref, tri_ref, mask_ref, o_ref,
               st_scr, a_scr, sh_scr, *, layer_j, c):
    n = c // 2

    logits = lbl_ref[:, cols]
    e = jnp.exp(logits - jnp.max(logits, axis=0, keepdims=True))
    p = e / jnp.sum(e, axis=0, keepdims=True)
    lb = jnp.sum(p[0:layer_j + 1], axis=0, keepdims=True) - p[0:1]

    q = _silu(q_ref[0, :, cols].astype(F32))
    sig = _sigmoid(f_ref[0, :, cols].astype(F32))
    g = jnp.log(jnp.maximum(lb + (1.0 - lb) * sig, GATE_FLOOR))
    k = (1.0 - lb) * (1.0 - sig)
    vb = i_ref[0, :, cols]
    qb = q.astype(BF16)
    kb = k.astype(BF16)
    yield

    a = _dot_const3(tri_ref[...], g)
    zeros_pad = jnp.zeros((HG_PAD, HEAD), F32)
    a_scr[0:HG_PAD, :] = zeros_pad
    a_scr[HG_PAD:HG_PAD + c, :] = a
    yield

    def seg_rows(b, first):
        return jnp.concatenate([jnp.broadcast_to(a_scr[pl.ds(first + j * b, 1), :], (b, HEAD))
                                for j in range(c // b)], axis=0)

    def shifted(x, slot):
        sh_scr[slot, 0:HG_PAD, :] = zeros_pad
        sh_scr[slot, HG_PAD + c:HG_PAD + c + HG_PAD, :] = zeros_pad
        sh_scr[slot, HG_PAD:HG_PAD + c, :] = x
        return lambda d: sh_scr[slot, pl.ds(HG_PAD + d, c), :]

    row = lax.broadcasted_iota(jnp.int32, (c, HEAD), 0)
    g_at = shifted(g, 0)
    odd = (row & 1) == 1
    up = {1: g, 2: g + jnp.where(odd, g_at(-1), 0.0)}
    lo = {1: None, 2: jnp.where(odd, 0.0, g_at(1))}
    up2_at = shifted(up[2], 1)
    suffix2_at = shifted(lo[2] + g, 2)
    pos = row & 3
    up[4] = up[2] + jnp.where(pos == 2, up2_at(-1), jnp.where(pos == 3, up2_at(-2), 0.0))
    lo[4] = lo[2] + jnp.where(pos == 0, suffix2_at(2), jnp.where(pos == 1, suffix2_at(1), 0.0))
    b = 8
    while b <= c:
        up[b] = a - seg_rows(b, HG_PAD - 1)
        lo[b] = seg_rows(b, HG_PAD - 1 + b) - a
        b *= 2
    yield

    def decayed(b):
        qd = (q * jnp.exp(up[b])).astype(BF16)
        kd = kb if lo[b] is None else (k * jnp.exp(lo[b])).astype(BF16)
        return qd, kd

    halves = (slice(0, n), slice(n, c))
    scores = [mask_ref[0] * _dot_nt(qb[h], kb[h]) for h in halves]
    b, lv = 1, 1
    while b < n:
        qd, kd = decayed(b)
        scores = [s + mask_ref[lv] * _dot_nt(qd[h], kd[h]) for s, h in zip(scores, halves)]
        b, lv = 2 * b, lv + 1
        yield
    qd, kd = decayed(n)
    cross = _dot_nt(qd[halves[1]], kd[halves[0]])
    o_top = _dot(scores[0].astype(BF16), vb[halves[0]])
    o_bot = _dot(jnp.concatenate([cross, scores[1]], axis=1).astype(BF16), vb)
    yield
    qd, kd = decayed(c)
    st = st_scr[...]
    o = jnp.concatenate([o_top, o_bot], axis=0) + _dot_nt(qd, st.astype(BF16))
    st_scr[...] = jnp.exp(a[c - 1:c, :]) * st + lax.dot_general(vb, kd, (((0,), (0,)), ((), ())),
                                                               preferred_element_type=F32)

    ms = jnp.mean(o * o, axis=-1, keepdims=True)
    y = o * lax.rsqrt(ms + RMS_EPS) * gain_ref[...]
    o_ref[0, :, cols] = (y * _silu(z_ref[0, :, cols].astype(F32))).astype(o_ref.dtype)
    yield


def _hgrn(proj, lb_logits, gain, layer_j, *, c):
    b, l, _ = proj.shape
    tri, masks = _hgrn_constants(c)
    n_even = lb_logits.shape[0]
    heads = HG_HEADS_PER_STEP
    groups = N_HEADS // heads
    width = heads * HEAD

    def col(seg):
        return pl.BlockSpec((1, c, width), lambda bi, h, t, seg=seg: (bi, t, seg * groups + h))

    kern = functools.partial(_hgrn_kernel, layer_j=layer_j, c=c, heads=heads)
    per_head_scratch = [pltpu.VMEM((HEAD, HEAD), F32), pltpu.VMEM((HG_PAD + c, HEAD), F32),
                        pltpu.VMEM((3, c + 2 * HG_PAD, HEAD), F32)]
    return pl.pallas_call(
        kern, grid=(b, groups, l // c),
        in_specs=[col(0), col(1), col(2), col(3),
                  pl.BlockSpec((n_even, width), lambda bi, h, t: (0, h)),
                  pl.BlockSpec((1, HEAD), lambda bi, h, t: (0, 0)),
                  pl.BlockSpec(tri.shape, lambda bi, h, t: (0, 0)),
                  pl.BlockSpec(masks.shape, lambda bi, h, t: (0, 0, 0))],
        out_specs=pl.BlockSpec((1, c, width), lambda bi, h, t: (bi, t, h)),
        out_shape=jax.ShapeDtypeStruct((b, l, BRANCH), BF16),
        scratch_shapes=per_head_scratch * heads,
        compiler_params=_params("parallel", "parallel", "arbitrary"), name="hgrn2",
    )(proj, proj, proj, proj, lb_logits, gain, jnp.asarray(tri, BF16), jnp.asarray(masks))


def _diff_kernel(q_ref, k_ref, v_ref, z_ref, lamv_ref, gain_ref, causal_ref, o_ref, vt_scr, sa_scr, sb_scr,
                 *, lam_init, tq, tk):
    qi = pl.program_id(2)
    n_trips = k_ref.shape[1] // tk

    @pl.when(qi == 0)
    def _():
        def transpose_block(i, carry):
            rows = pl.ds(pl.multiple_of(i * tk, tk), tk)
            vt_scr[i] = v_ref[0, rows, :].astype(F32).T.astype(BF16)
            return carry
        lax.fori_loop(0, n_trips, transpose_block, 0)

    lv = lamv_ref[...]
    d01 = jnp.sum(jnp.sum(lv[0:1] * lv[1:2], axis=-1, keepdims=True), axis=0, keepdims=True)
    d23 = jnp.sum(jnp.sum(lv[2:3] * lv[3:4], axis=-1, keepdims=True), axis=0, keepdims=True)
    lam = jnp.exp(d01) - jnp.exp(d23) + lam_init

    q = q_ref[0].astype(F32) * (DF_DH ** -0.5 * math.log2(math.e))
    lane = lax.broadcasted_iota(jnp.int32, q.shape, 1)
    zero = jnp.zeros_like(q)
    halves = (jnp.where(lane < DF_DH, q, zero).astype(BF16), jnp.where(lane >= DF_DH, q, zero).astype(BF16))

    parts = DIFF_KEY_PARTS
    rows_per_part = tk // parts

    def scores_part(t, s_buf, hf, kp, diagonal=False):
        start = pl.multiple_of(t * tk + kp * rows_per_part, rows_per_part)
        s = _dot_nt(k_ref[0, pl.ds(start, rows_per_part), :], halves[hf])
        if diagonal:
            s = s + causal_ref[kp * rows_per_part:(kp + 1) * rows_per_part, :]
        s_buf[hf, kp * rows_per_part:(kp + 1) * rows_per_part, :] = s
        return jnp.max(s, axis=0, keepdims=True)

    def rescale(st, top):
        m, l, acc = st
        m_new = jnp.maximum(m, top)
        alpha = jnp.exp2(m - m_new)
        return m_new, alpha * l, alpha * acc

    def absorb_part(t, s_buf, hf, kp, st):
        m, l, acc = st
        lanes = slice(kp * rows_per_part, (kp + 1) * rows_per_part)
        p = jnp.exp2(s_buf[hf, lanes, :] - m)
        return m, l + jnp.sum(p, axis=0, keepdims=True), acc + _dot(vt_scr[t, :, lanes], p.astype(BF16))

    def scores(t, s_buf, diagonal=False):
        return tuple(functools.reduce(jnp.maximum, [scores_part(t, s_buf, hf, kp, diagonal) for kp in range(parts)])
                     for hf in range(2))

    def absorb(t, s_buf, state, tops):
        out = []
        for hf in range(2):
            st = rescale(state[hf], tops[hf])
            for kp in range(parts):
                st = absorb_part(t, s_buf, hf, kp, st)
            out.append(st)
        return tuple(out)

    def step(t_next, buf_next, t_cur, buf_cur, state, tops_cur):
        tops_next, out = [], []
        for hf in range(2):
            st = rescale(state[hf], tops_cur[hf])
            top = None
            for kp in range(parts):
                piece = scores_part(t_next, buf_next, hf, kp)
                top = piece if top is None else jnp.maximum(top, piece)
                st = absorb_part(t_cur, buf_cur, hf, kp, st)
            tops_next.append(top)
            out.append(st)
        return tuple(out), tuple(tops_next)

    n_full = qi
    state = tuple((jnp.full((1, tq), NEG, F32), jnp.zeros((1, tq), F32), jnp.zeros((HEAD, tq), F32))
                  for _ in halves)
    tops = scores(n_full, sa_scr, diagonal=True)

    def visits(v, state, tops_a, n_pairs):
        for _ in range(n_pairs):
            state, tops_b = step(n_full - (v + 1), sb_scr, n_full - v, sa_scr, state, tops_a)
            state, tops_a = step(n_full - (v + 2), sa_scr, n_full - (v + 1), sb_scr, state, tops_b)
            v = v + 2
        return state, tops_a

    n_quads = n_full // 4
    state, tops = lax.fori_loop(0, n_quads, lambda i, c: visits(4 * i, c[0], c[1], 2), (state, tops))
    n_pairs = (n_full - 4 * n_quads) // 2
    state, tops = lax.fori_loop(0, n_pairs, lambda i, c: visits(4 * n_quads + 2 * i, c[0], c[1], 1),
                                (state, tops))

    def finish(state):
        (_, l1, acc1), (_, l2, acc2) = state
        o = (acc1 / l1 - lam * (acc2 / l2)).T
        ms = jnp.mean(o * o, axis=-1, keepdims=True)
        y = o * lax.rsqrt(ms + RMS_EPS) * gain_ref[...]
        o_ref[0] = (y * (1.0 - lam_init) * _silu(z_ref[0].astype(F32))).astype(o_ref.dtype)

    @pl.when(n_full % 2 == 0)
    def _():
        finish(absorb(0, sa_scr, state, tops))

    @pl.when(n_full % 2 == 1)
    def _():
        mid, tops_b = step(0, sb_scr, 1, sa_scr, state, tops)
        finish(absorb(0, sb_scr, mid, tops_b))


def _diff_attn(proj, lam_vec, gain, lam_init, *, tq):
    b, l, _ = proj.shape

    def qcol(seg):
        return pl.BlockSpec((1, tq, HEAD), lambda bi, h, t, seg=seg: (bi, t, seg * N_HEADS + h))

    def kvcol(seg):
        return pl.BlockSpec((1, l, HEAD), lambda bi, h, t, seg=seg: (bi, 0, seg * N_HEADS + h))

    tk = tq
    key = np.arange(tk)[:, None]
    causal = np.where(key <= np.arange(tq)[None, :], 0.0, NEG).astype(np.float32)
    kern = functools.partial(_diff_kernel, lam_init=lam_init, tq=tq, tk=tk)
    return pl.pallas_call(
        kern, grid=(b, N_HEADS, l // tq),
        in_specs=[qcol(4), kvcol(5), kvcol(6), qcol(7),
                  pl.BlockSpec(lam_vec.shape, lambda bi, h, t: (0, 0)),
                  pl.BlockSpec((1, HEAD), lambda bi, h, t: (0, 0)),
                  pl.BlockSpec((tk, tq), lambda bi, h, t: (0, 0))],
        out_specs=pl.BlockSpec((1, tq, HEAD), lambda bi, h, t: (bi, t, h)),
        out_shape=jax.ShapeDtypeStruct((b, l, BRANCH), BF16),
        scratch_shapes=[pltpu.VMEM((l // tk, HEAD, tk), BF16), pltpu.VMEM((2, tk, tq), F32), pltpu.VMEM((2, tk, tq), F32)],
        compiler_params=_params("parallel", "parallel", "arbitrary"), name="diff_attn",
    )(proj, proj, proj, proj, lam_vec, gain, jnp.asarray(causal))


def _sb_kernel(q_ref, k_ref, v_ref, z_ref, o_ref, *, tq, heads):
    qi = pl.program_id(2)
    cols = [slice(hd * HEAD, (hd + 1) * HEAD) for hd in range(heads)]
    q = [(q_ref[0, :, c].astype(F32) * (HEAD ** -0.5)).astype(BF16) for c in cols]
    r = lax.broadcasted_iota(jnp.int32, (tq, tq), 0)
    cc = lax.broadcasted_iota(jnp.int32, (tq, tq), 1)
    later = (r > cc).astype(BF16)

    def local(kj, masked):
        rows = pl.ds(pl.multiple_of(kj * tq, tq), tq)
        zz = [_dot_nt(qh, k_ref[0, rows, c]) for qh, c in zip(q, cols)]
        sp = [_softplus(x) for x in zz]
        log_rest = [-x for x in sp]
        if masked:
            log_rest = [jnp.where(cc < r, x, 0.0) for x in log_rest]
        hi = [x.astype(BF16) for x in log_rest]
        lo = [(x - h.astype(F32)).astype(BF16) for x, h in zip(log_rest, hi)]
        inner = [z - s + (_dot(h, later) + _dot(lw, later)) for z, s, h, lw in zip(zz, sp, hi, lo)]
        return inner, [jnp.sum(x, axis=-1, keepdims=True) for x in log_rest], rows

    def weigh(part, run, masked):
        inner, total, rows = part
        w = [jnp.exp(x + rn) for x, rn in zip(inner, run)]
        if masked:
            w = [jnp.where(cc < r, x, 0.0) for x in w]
        pv = [_dot(x.astype(BF16), v_ref[0, rows, c]) for x, c in zip(w, cols)]
        return pv, [rn + t for rn, t in zip(run, total)]

    def alive(run):
        top = run[0]
        for rn in run[1:]:
            top = jnp.maximum(top, rn)
        return jnp.max(top) > -SB_SKIP

    diag = local(qi, True)
    prev = local(jnp.maximum(qi - 1, 0), False)
    acc, run = weigh(diag, [jnp.zeros((tq, 1), F32)] * heads, True)
    has_prev = (qi > 0).astype(F32)
    pv, run_prev = weigh(prev, run, False)
    acc = [a + has_prev * p for a, p in zip(acc, pv)]
    run = [rn + has_prev * (rp - rn) for rn, rp in zip(run, run_prev)]

    def cond(carry):
        kj, go, _, _ = carry
        return jnp.logical_and(kj >= 0, go)

    def body(carry):
        kj, _, run, acc = carry
        pv, run = weigh(local(kj, False), run, False)
        return kj - 1, alive(run), run, [a + p for a, p in zip(acc, pv)]

    _, _, _, acc = lax.while_loop(cond, body, (qi - 2, alive(run), run, acc))
    for a, c in zip(acc, cols):
        o_ref[0, :, c] = (a * _silu(z_ref[0, :, c].astype(F32))).astype(o_ref.dtype)


def _stick_breaking(proj, *, tq):
    b, l, _ = proj.shape
    heads = SB_HEADS_PER_STEP
    groups = N_HEADS // heads
    width = heads * HEAD

    def qcol(seg):
        return pl.BlockSpec((1, tq, width), lambda bi, h, t, seg=seg: (bi, t, seg * groups + h))

    def kvcol(seg):
        return pl.BlockSpec((1, l, width), lambda bi, h, t, seg=seg: (bi, 0, seg * groups + h))

    return pl.pallas_call(
        functools.partial(_sb_kernel, tq=tq, heads=heads), grid=(b, groups, l // tq),
        in_specs=[qcol(4), kvcol(5), kvcol(6), qcol(7)],
        out_specs=pl.BlockSpec((1, tq, width), lambda bi, h, t: (bi, t, h)),
        out_shape=jax.ShapeDtypeStruct((b, l, BRANCH), BF16),
        compiler_params=_params("parallel", "parallel", "arbitrary"), name="stick_breaking",
    )(proj, proj, proj, proj)


def _split2(x):
    hi = x.astype(BF16)
    return hi, (x - hi.astype(F32)).astype(BF16)


def _dot3(a, b):
    a_hi, a_lo = _split2(a)
    b_hi, b_lo = _split2(b)
    return _dot(a_hi, b_hi) + (_dot(a_hi, b_lo) + _dot(a_lo, b_hi))


def _dot_const3(const_bf16, x):
    hi = x.astype(BF16)
    r1 = x - hi.astype(F32)
    mid = r1.astype(BF16)
    lo = (r1 - mid.astype(F32)).astype(BF16)
    return _dot(const_bf16, hi) + (_dot(const_bf16, mid) + _dot(const_bf16, lo))


def _each(fn, *lists):
    return [fn(*args) for args in zip(*lists)]


def _unit_lower_inverse_steps(lws, eye, blockdiag, out):
    ld = [lw * blockdiag for lw in lws]
    lo = _each(lambda a, b: a - b, lws, ld)
    td = [eye - x for x in ld]
    p = _each(_dot3, ld, ld)
    yield
    n_factors = int(math.log2(INV_BLOCK)) - 1
    for i in range(n_factors):
        td = _each(lambda a, b: a + _dot3(a, b), td, p)
        if i < n_factors - 1:
            p = _each(_dot3, p, p)
        yield
    n = _each(_dot3, td, lo)
    yield
    n2 = _each(_dot3, n, n)
    yield
    y = [eye - x for x in n]
    y = _each(lambda a, b: a + _dot3(a, b), y, n2)
    yield
    out.extend(_each(_dot3, y, td))
    yield


_DONE = object()


def _run_staggered(chains, lag):
    live = list(chains)
    for ahead, chain in enumerate(reversed(live)):
        for _ in range(ahead * lag):
            next(chain, None)
    while live:
        live = [chain for chain in live if next(chain, _DONE) is not _DONE]


def _gdn_prep_kernel(q_ref, k_ref, v_ref, qh_ref, kh_ref, vh_ref, ab_ref, wq_ref, wk_ref, wv_ref, alog_ref, dtb_ref,
                     u_ref, w_ref, qg_ref, kd_ref, at_ref, gl_ref, ext_scr, *, n_chunks, c):
    h = pl.program_id(1)
    t = pl.program_id(2)
    t_blk = n_chunks * c

    for idx, (x_ref, halo_ref) in enumerate(((q_ref, qh_ref), (k_ref, kh_ref), (v_ref, vh_ref))):
        ext_scr[idx, 0:GD_HALO, :] = jnp.where(t > 0, halo_ref[0].astype(F32), 0.0)
        ext_scr[idx, GD_HALO:GD_HALO + t_blk, :] = x_ref[0].astype(F32)

    def conv(w_ref, idx, row0, n_rows):
        w = w_ref[...]
        y = w[GD_CONV - 1:GD_CONV] * ext_scr[idx, pl.ds(GD_HALO + row0, n_rows), :]
        for tap in range(GD_CONV - 1):
            y = y + w[tap:tap + 1] * ext_scr[idx, pl.ds(GD_HALO - (GD_CONV - 1) + tap + row0, n_rows), :]
        return _silu(y)

    def l2n(x):
        return x * lax.rsqrt(jnp.sum(x * x, axis=-1, keepdims=True) + RMS_EPS)

    r = lax.broadcasted_iota(jnp.int32, (c, c), 0)
    cc = lax.broadcasted_iota(jnp.int32, (c, c), 1)
    incl = cc <= r
    strict = cc < r
    tri = incl.astype(BF16)
    eye = (cc == r).astype(F32)
    blockdiag = ((r // INV_BLOCK) == (cc // INV_BLOCK)).astype(F32)
    r2 = lax.broadcasted_iota(jnp.int32, (c, HEAD), 0)
    c2 = lax.broadcasted_iota(jnp.int32, (c, HEAD), 1)
    strict_wide = jnp.logical_and(c2 < r2, c2 < c).astype(F32)

    def chain(ids):
        row0, n_rows = ids[0] * c, len(ids) * c
        local = [slice(i * c, (i + 1) * c) for i in range(len(ids))]
        chunks = [slice(ci * c, (ci + 1) * c) for ci in ids]
        q_rows = l2n(conv(wq_ref, 0, row0, n_rows)) * (HEAD ** -0.5)
        yield
        k_rows = l2n(conv(wk_ref, 1, row0, n_rows))
        yield
        v_rows = conv(wv_ref, 2, row0, n_rows)
        ab = ab_ref[0, row0:row0 + n_rows, :]
        lane = lax.broadcasted_iota(jnp.int32, ab.shape, 1)
        g_lanes = -jnp.exp(alog_ref[...]) * _softplus(ab + dtb_ref[...])
        g_rows = jnp.sum(jnp.where(lane == h, g_lanes, 0.0), axis=-1, keepdims=True)
        beta_rows = jnp.sum(jnp.where(lane == h + N_HEADS, _sigmoid(ab), 0.0), axis=-1, keepdims=True)
        yield
        q = [q_rows[sl] for sl in local]
        k = [k_rows[sl] for sl in local]
        v = [v_rows[sl] for sl in local]
        beta = [beta_rows[sl] for sl in local]
        g_wide = [jnp.broadcast_to(g_rows[sl], (c, HEAD)) for sl in local]
        sums = [_dot_const3(tri, jnp.concatenate([gw, gw * strict_wide], axis=1)) for gw in g_wide]
        yield
        gc = [s[:, 0:HEAD] for s in sums]
        decay = [jnp.exp(jnp.where(incl, s[:, HEAD:HEAD + c], NEG)) for s in sums]
        kb = _each(lambda a, b: a * b, k, beta)
        kbf = [x.astype(BF16) for x in k]
        lw = _each(lambda a, b, d: jnp.where(strict, _dot_nt(a.astype(BF16), b) * d, 0.0), kb, kbf, decay)
        attn = _each(lambda a, b, d: _dot_nt(a.astype(BF16), b) * d, q, kbf, decay)
        yield
        inverse = []
        yield from _unit_lower_inverse_steps(lw, eye, blockdiag, inverse)
        tinv = [x.astype(BF16) for x in inverse]
        egc = [jnp.exp(x) for x in gc]
        u = _each(lambda ti, a, b: _dot(ti, (a * b).astype(BF16)), tinv, v, beta)
        w = _each(lambda ti, a, e: _dot(ti, (a * e).astype(BF16)), tinv, kb, egc)
        for i, (ci, sl) in enumerate(zip(ids, chunks)):
            gc_last = gc[i][c - 1:c, :]
            u_ref[0, 0, sl, :] = u[i]
            w_ref[0, 0, sl, :] = w[i].astype(BF16)
            qg_ref[0, 0, sl, :] = (q[i] * egc[i]).astype(BF16)
            kd_ref[0, 0, sl, :] = (k[i] * jnp.exp(gc_last - gc[i])).astype(BF16)
            at_ref[0, 0, sl, :] = attn[i].astype(BF16)
            gl_ref[0, 0, ci] = jnp.exp(gc_last)
        yield

    n_groups = min(GD_GROUPS, n_chunks)
    per_group = n_chunks // n_groups
    _run_staggered([chain(range(gi * per_group, (gi + 1) * per_group)) for gi in range(n_groups)], GD_STAGGER)


def _gdn_scan_kernel(u_ref, w_ref, qg_ref, kd_ref, at_ref, gl_ref, z_ref, gain_ref, o_ref, s_scr, *, n_chunks, c, rows_b):
    @pl.when(pl.program_id(1) == 0)
    def _():
        s_scr[...] = jnp.zeros_like(s_scr)

    lanes = [(bb, h) for bb in range(rows_b) for h in range(N_HEADS)]
    gain = gain_ref[...]
    s = [s_scr[bb * N_HEADS + h] for bb, h in lanes]
    for ci in range(n_chunks):
        rows = pl.ds(ci * c, c)
        sb = [x.astype(BF16) for x in s]
        v_new = [u_ref[bb, h, rows, :] - _dot(w_ref[bb, h, rows, :], x) for (bb, h), x in zip(lanes, sb)]
        vb = [x.astype(BF16) for x in v_new]
        o = [_dot(qg_ref[bb, h, rows, :], x) + _dot(at_ref[bb, h, rows, :], y)
             for (bb, h), x, y in zip(lanes, sb, vb)]
        s = [gl_ref[bb, h, ci] * x
             + lax.dot_general(kd_ref[bb, h, rows, :], y, (((0,), (0,)), ((), ())), preferred_element_type=F32)
             for (bb, h), x, y in zip(lanes, s, vb)]
        for (bb, h), oh in zip(lanes, o):
            cols = slice(h * HEAD, (h + 1) * HEAD)
            ms = jnp.mean(oh * oh, axis=-1, keepdims=True)
            y = oh * lax.rsqrt(ms + RMS_EPS) * gain
            o_ref[bb, rows, cols] = (y * _silu(z_ref[bb, rows, cols].astype(F32))).astype(o_ref.dtype)
    for i, x in enumerate(s):
        s_scr[i] = x


def _gdn(proj, ab, conv_w, a_log, dt_bias, gain, *, t_prep, t_scan, c):
    b, l, _ = proj.shape
    halo_blocks = t_prep // GD_HALO

    def col(seg):
        return pl.BlockSpec((1, t_prep, HEAD), lambda bi, h, t, seg=seg: (bi, t, seg * N_HEADS + h))

    def halo(seg):
        return pl.BlockSpec((1, GD_HALO, HEAD),
                            lambda bi, h, t, seg=seg: (bi, jnp.maximum(t * halo_blocks - 1, 0), seg * N_HEADS + h))

    def wcol(seg):
        return pl.BlockSpec((GD_CONV, HEAD), lambda bi, h, t, seg=seg: (0, seg * N_HEADS + h))

    one = pl.BlockSpec((1, HEAD), lambda bi, h, t: (0, 0))

    def per_head(width):
        return pl.BlockSpec((1, 1, t_prep, width), lambda bi, h, t: (bi, h, t, 0))

    n_prep = t_prep // c
    u, w, qg, kd, at, gl = pl.pallas_call(
        functools.partial(_gdn_prep_kernel, n_chunks=n_prep, c=c), grid=(b, N_HEADS, l // t_prep),
        in_specs=[col(0), col(1), col(2), halo(0), halo(1), halo(2),
                  pl.BlockSpec((1, t_prep, HEAD), lambda bi, h, t: (bi, t, 0)),
                  wcol(0), wcol(1), wcol(2), one, one],
        out_specs=[per_head(HEAD), per_head(HEAD), per_head(HEAD), per_head(HEAD), per_head(c),
                   pl.BlockSpec((1, 1, n_prep, 1, HEAD), lambda bi, h, t: (bi, h, t, 0, 0))],
        out_shape=[jax.ShapeDtypeStruct((b, N_HEADS, l, HEAD), F32),
                   jax.ShapeDtypeStruct((b, N_HEADS, l, HEAD), BF16),
                   jax.ShapeDtypeStruct((b, N_HEADS, l, HEAD), BF16),
                   jax.ShapeDtypeStruct((b, N_HEADS, l, HEAD), BF16),
                   jax.ShapeDtypeStruct((b, N_HEADS, l, c), BF16),
                   jax.ShapeDtypeStruct((b, N_HEADS, l // c, 1, HEAD), F32)],
        scratch_shapes=[pltpu.VMEM((3, t_prep + GD_HALO, HEAD), F32)],
        compiler_params=_params("parallel", "parallel", "parallel"), name="gdn_prep",
    )(proj, proj, proj, proj, proj, proj, ab, conv_w, conv_w, conv_w, a_log, dt_bias)

    rows_b = GD_SCAN_BATCH_ROWS if b % GD_SCAN_BATCH_ROWS == 0 else 1

    def all_heads(width):
        return pl.BlockSpec((rows_b, N_HEADS, t_scan, width), lambda bi, t: (bi, 0, t, 0))

    n_scan = t_scan // c
    return pl.pallas_call(
        functools.partial(_gdn_scan_kernel, n_chunks=n_scan, c=c, rows_b=rows_b), grid=(b // rows_b, l // t_scan),
        in_specs=[all_heads(HEAD), all_heads(HEAD), all_heads(HEAD), all_heads(HEAD), all_heads(c),
                  pl.BlockSpec((rows_b, N_HEADS, n_scan, 1, HEAD), lambda bi, t: (bi, 0, t, 0, 0)),
                  pl.BlockSpec((rows_b, t_scan, BRANCH), lambda bi, t: (bi, t, 3)),
                  pl.BlockSpec((1, HEAD), lambda bi, t: (0, 0))],
        out_specs=pl.BlockSpec((rows_b, t_scan, BRANCH), lambda bi, t: (bi, t, 0)),
        out_shape=jax.ShapeDtypeStruct((b, l, BRANCH), BF16),
        scratch_shapes=[pltpu.VMEM((rows_b * N_HEADS, HEAD, HEAD), F32)],
        compiler_params=_params("parallel", "arbitrary"), name="gdn_scan",
    )(u, w, qg, kd, at, gl, proj, gain)


def _pad_lanes(vec):
    return jnp.zeros((1, HEAD), F32).at[0, :vec.shape[0]].set(vec.astype(F32))


def kernel(x, norm_pre, norm_post, ev_w_in, ev_w_out, hg_lb_logits, hg_norm, df_lambda, df_norm,
           od_w_in, od_w_out, gd_conv, gd_a_log, gd_dt_bias, gd_norm):
    p = dict(norm_pre=norm_pre, norm_post=norm_post, ev_w_in=ev_w_in, ev_w_out=ev_w_out,
             hg_lb_logits=hg_lb_logits, hg_norm=hg_norm, df_lambda=df_lambda, df_norm=df_norm,
             od_w_in=od_w_in, od_w_out=od_w_out, gd_conv=gd_conv, gd_a_log=gd_a_log,
             gd_dt_bias=gd_dt_bias, gd_norm=gd_norm)
    return _run_layers(x, p, range(norm_pre.shape[0]))


def _run_layers(x, p, layers):
    norm_pre, norm_post = p["norm_pre"], p["norm_post"]
    ev_w_in, ev_w_out, od_w_in, od_w_out = p["ev_w_in"], p["ev_w_out"], p["od_w_in"], p["od_w_out"]
    hg_lb_logits, hg_norm, df_lambda, df_norm = p["hg_lb_logits"], p["hg_norm"], p["df_lambda"], p["df_norm"]
    gd_conv, gd_a_log, gd_dt_bias, gd_norm = p["gd_conv"], p["gd_a_log"], p["gd_dt_bias"], p["gd_norm"]
    b, l, d = x.shape
    m = b * l
    tm_in = min(1024, m)
    tm_out = min(512, m)
    t_rec = min(REC_BLOCK, l)
    t_att = min(ATT_BLOCK, l)
    x2 = x.reshape(m, d)
    for layer in layers:
        j = layer // 2
        g_pre = norm_pre[layer].reshape(1, d)
        g_post = norm_post[layer].reshape(1, d)
        if layer % 2 == 0:
            proj = _inproj(x2, g_pre, ev_w_in[j].astype(BF16), tm=tm_in, tn=IN_TILE_N).reshape(b, l, -1)
            lam_init = 0.8 - 0.6 * math.exp(-0.3 * layer)
            mix_a = _hgrn(proj, hg_lb_logits.astype(F32), hg_norm[j].reshape(1, HEAD), j, c=min(HG_CHUNK, l))
            mix_b = _diff_attn(proj, df_lambda[j].astype(F32), df_norm[j].reshape(1, HEAD), lam_init,
                               tq=min(DIFF_Q_BLOCK, l))
            w_out = ev_w_out[j].astype(BF16)
        else:
            w = od_w_in[j]
            w_main = jnp.concatenate([w[:, :4 * BRANCH], w[:, 4 * BRANCH + 2 * N_HEADS:]], axis=1).astype(BF16)
            w_ab = jnp.pad(w[:, 4 * BRANCH:4 * BRANCH + 2 * N_HEADS], ((0, 0), (0, HEAD - 2 * N_HEADS))).astype(BF16)
            proj, ab = _inproj(x2, g_pre, w_main, w_ab, tm=tm_in, tn=IN_TILE_N)
            proj = proj.reshape(b, l, -1)
            mix_a = _gdn(proj, ab.reshape(b, l, HEAD), gd_conv[j].astype(F32), _pad_lanes(gd_a_log[j]),
                         _pad_lanes(gd_dt_bias[j]), gd_norm[j].reshape(1, HEAD), t_prep=min(GD_PREP_BLOCK, l), t_scan=t_rec, c=CHUNK)
            mix_b = _stick_breaking(proj, tq=t_att)
            w_out = od_w_out[j].astype(BF16)
        x2 = _outproj(mix_a.reshape(m, BRANCH), mix_b.reshape(m, BRANCH), w_out[:BRANCH], w_out[BRANCH:],
                      g_post, x2, tm=tm_out)
    return x2.reshape(b, l, d)
```

```python
import functools
import math

import numpy as np
import jax
import jax.numpy as jnp
from jax import lax
from jax.experimental import pallas as pl
from jax.experimental.pallas import tpu as pltpu

F32 = jnp.float32
BF16 = jnp.bfloat16
HIGHEST = lax.Precision.HIGHEST

D_MODEL = 2048
BRANCH = D_MODEL // 2
N_HEADS = 8
HEAD = BRANCH // N_HEADS
DF_DH = HEAD // 2
GD_CONV = 4
RMS_EPS = 1e-6
NEG = -1e30
GATE_FLOOR = 1e-20
SB_SKIP = 104.0

V7X_VMEM_LIMIT = 56 * 1024 * 1024

IN_TILE_N = 2048
CHUNK = 64
HG_CHUNK = 256
HG_HEADS_PER_STEP = 8
HG_STAGGER = 5
REC_BLOCK = 256
ATT_BLOCK = 256
SB_HEADS_PER_STEP = 4
DIFF_Q_BLOCK = 512
DIFF_KEY_PARTS = 2
INV_BLOCK = 16
GD_PREP_BLOCK = 2048
GD_HALO = 16
GD_SCAN_BATCH_ROWS = 2
GD_GROUPS = 4
GD_STAGGER = 2


def _params(*sem):
    return pltpu.CompilerParams(dimension_semantics=sem, vmem_limit_bytes=V7X_VMEM_LIMIT)


def _sigmoid(x):
    return 0.5 * jnp.tanh(0.5 * x) + 0.5


def _silu(x):
    return x * _sigmoid(x)


def _softplus(x):
    return jnp.maximum(x, 0.0) + jnp.log(1.0 + jnp.exp(-jnp.abs(x)))


def _dot(a, b):
    return jnp.dot(a, b, preferred_element_type=F32)


def _dot_nt(a, b):
    return lax.dot_general(a, b, (((1,), (1,)), ((), ())), preferred_element_type=F32)


def _dot_exact(a, b):
    return jnp.dot(a, b, precision=HIGHEST, preferred_element_type=F32)


def _inproj_kernel(x_ref, g_ref, w_ref, o_ref, h_scr):
    @pl.when(pl.program_id(1) == 0)
    def _():
        x = x_ref[...]
        ms = jnp.mean(x * x, axis=-1, keepdims=True)
        h_scr[...] = (x * lax.rsqrt(ms + RMS_EPS) * g_ref[...]).astype(BF16)

    o_ref[...] = _dot(h_scr[...], w_ref[...]).astype(o_ref.dtype)


def _inproj_ab_kernel(x_ref, g_ref, w_ref, wab_ref, o_ref, oab_ref, h_scr):
    @pl.when(pl.program_id(1) == 0)
    def _():
        x = x_ref[...]
        ms = jnp.mean(x * x, axis=-1, keepdims=True)
        h_scr[...] = (x * lax.rsqrt(ms + RMS_EPS) * g_ref[...]).astype(BF16)
        oab_ref[...] = _dot(h_scr[...], wab_ref[...])

    o_ref[...] = _dot(h_scr[...], w_ref[...]).astype(o_ref.dtype)


def _inproj(x2, gain, w, w_ab=None, *, tm, tn):
    m, d = x2.shape
    n = w.shape[1]
    grid = (m // tm, n // tn)
    x_spec = pl.BlockSpec((tm, d), lambda i, j: (i, 0))
    g_spec = pl.BlockSpec((1, d), lambda i, j: (0, 0))
    w_spec = pl.BlockSpec((d, tn), lambda i, j: (0, j))
    o_spec = pl.BlockSpec((tm, tn), lambda i, j: (i, j))
    scratch = [pltpu.VMEM((tm, d), BF16)]
    if w_ab is None:
        return pl.pallas_call(
            _inproj_kernel, grid=grid, in_specs=[x_spec, g_spec, w_spec], out_specs=o_spec,
            out_shape=jax.ShapeDtypeStruct((m, n), BF16), scratch_shapes=scratch,
            compiler_params=_params("parallel", "arbitrary"), name="inproj",
        )(x2, gain, w)
    nab = w_ab.shape[1]
    return pl.pallas_call(
        _inproj_ab_kernel, grid=grid,
        in_specs=[x_spec, g_spec, w_spec, pl.BlockSpec((d, nab), lambda i, j: (0, 0))],
        out_specs=[o_spec, pl.BlockSpec((tm, nab), lambda i, j: (i, 0))],
        out_shape=[jax.ShapeDtypeStruct((m, n), BF16), jax.ShapeDtypeStruct((m, nab), F32)],
        scratch_shapes=scratch, compiler_params=_params("parallel", "arbitrary"), name="inproj_ab",
    )(x2, gain, w, w_ab)


def _outproj_kernel(ma_ref, mb_ref, wa_ref, wb_ref, g_ref, x_ref, o_ref):
    y = _dot(ma_ref[...], wa_ref[...]) + _dot(mb_ref[...], wb_ref[...])
    ms = jnp.mean(y * y, axis=-1, keepdims=True)
    o_ref[...] = x_ref[...] + y * lax.rsqrt(ms + RMS_EPS) * g_ref[...]


def _outproj(mix_a, mix_b, w_a, w_b, gain, x2, *, tm):
    m, d = x2.shape
    k = mix_a.shape[1]
    return pl.pallas_call(
        _outproj_kernel, grid=(m // tm,),
        in_specs=[pl.BlockSpec((tm, k), lambda i: (i, 0)), pl.BlockSpec((tm, k), lambda i: (i, 0)),
                  pl.BlockSpec((k, d), lambda i: (0, 0)), pl.BlockSpec((k, d), lambda i: (0, 0)),
                  pl.BlockSpec((1, d), lambda i: (0, 0)), pl.BlockSpec((tm, d), lambda i: (i, 0))],
        out_specs=pl.BlockSpec((tm, d), lambda i: (i, 0)),
        out_shape=jax.ShapeDtypeStruct((m, d), F32),
        compiler_params=_params("parallel"), name="outproj",
    )(mix_a, mix_b, w_a, w_b, gain, x2)


def _hgrn_constants(c):
    n = c // 2
    t = np.arange(n)
    masks = [np.eye(n, dtype=np.float32)]
    b = 1
    while b < n:
        mid = (t // (2 * b)) * (2 * b) + b
        same = (t[:, None] // (2 * b)) == (t[None, :] // (2 * b))
        masks.append((same & (t[:, None] >= mid[:, None]) & (t[None, :] < mid[:, None])).astype(np.float32))
        b *= 2
    rows = np.arange(c)
    tri = (rows[None, :] <= rows[:, None]).astype(np.float32)
    return tri, np.stack(masks, axis=0)


HG_PAD = 8


def _hgrn_kernel(q_ref, f_ref, i_ref, z_ref, lbl_ref, gain_ref, tri_ref, mask_ref, o_ref, *scratch,
                 layer_j, c, heads):
    @pl.when(pl.program_id(2) == 0)
    def _():
        for hd in range(heads):
            scratch[3 * hd][...] = jnp.zeros((HEAD, HEAD), F32)

    _run_staggered([_hgrn_head(slice(hd * HEAD, (hd + 1) * HEAD), q_ref, f_ref, i_ref, z_ref, lbl_ref, gain_ref,
                               tri_ref, mask_ref, o_ref, *scratch[3 * hd:3 * hd + 3], layer_j=layer_j, c=c)
                    for hd in range(heads)], HG_STAGGER)


def _hgrn_head(cols, q_ref, f_ref, i_ref, z_ref, lbl_ref, gain_ref, tri_ref, mask_ref, o_ref,
               st_scr, a_scr, sh_scr, *, layer_j, c):
    n = c // 2

    logits = lbl_ref[:, cols]
    e = jnp.exp(logits - jnp.max(logits, axis=0, keepdims=True))
    p = e / jnp.sum(e, axis=0, keepdims=True)
    lb = jnp.sum(p[0:layer_j + 1], axis=0, keepdims=True) - p[0:1]

    q = _silu(q_ref[0, :, cols].astype(F32))
    sig = _sigmoid(f_ref[0, :, cols].astype(F32))
    g = jnp.log(jnp.maximum(lb + (1.0 - lb) * sig, GATE_FLOOR))
    k = (1.0 - lb) * (1.0 - sig)
    vb = i_ref[0, :, cols]
    qb = q.astype(BF16)
    kb = k.astype(BF16)
    yield

    a = _dot_const3(tri_ref[...], g)
    zeros_pad = jnp.zeros((HG_PAD, HEAD), F32)
    a_scr[0:HG_PAD, :] = zeros_pad
    a_scr[HG_PAD:HG_PAD + c, :] = a
    yield

    def seg_rows(b, first):
        return jnp.concatenate([jnp.broadcast_to(a_scr[pl.ds(first + j * b, 1), :], (b, HEAD))
                                for j in range(c // b)], axis=0)

    def shifted(x, slot):
        sh_scr[slot, 0:HG_PAD, :] = zeros_pad
        sh_scr[slot, HG_PAD + c:HG_PAD + c + HG_PAD, :] = zeros_pad
        sh_scr[slot, HG_PAD:HG_PAD + c, :] = x
        return lambda d: sh_scr[slot, pl.ds(HG_PAD + d, c), :]

    row = lax.broadcasted_iota(jnp.int32, (c, HEAD), 0)
    g_at = shifted(g, 0)
    odd = (row & 1) == 1
    up = {1: g, 2: g + jnp.where(odd, g_at(-1), 0.0)}
    lo = {1: None, 2: jnp.where(odd, 0.0, g_at(1))}
    up2_at = shifted(up[2], 1)
    suffix2_at = shifted(lo[2] + g, 2)
    pos = row & 3
    up[4] = up[2] + jnp.where(pos == 2, up2_at(-1), jnp.where(pos == 3, up2_at(-2), 0.0))
    lo[4] = lo[2] + jnp.where(pos == 0, suffix2_at(2), jnp.where(pos == 1, suffix2_at(1), 0.0))
    b = 8
    while b <= c:
        up[b] = a - seg_rows(b, HG_PAD - 1)
        lo[b] = seg_rows(b, HG_PAD - 1 + b) - a
        b *= 2
    yield

    def decayed(b):
        qd = (q * jnp.exp(up[b])).astype(BF16)
        kd = kb if lo[b] is None else (k * jnp.exp(lo[b])).astype(BF16)
        return qd, kd

    halves = (slice(0, n), slice(n, c))
    scores = [mask_ref[0] * _dot_nt(qb[h], kb[h]) for h in halves]
    b, lv = 1, 1
    while b < n:
        qd, kd = decayed(b)
        scores = [s + mask_ref[lv] * _dot_nt(qd[h], kd[h]) for s, h in zip(scores, halves)]
        b, lv = 2 * b, lv + 1
        yield
    qd, kd = decayed(n)
    cross = _dot_nt(qd[halves[1]], kd[halves[0]])
    o_top = _dot(scores[0].astype(BF16), vb[halves[0]])
    o_bot = _dot(jnp.concatenate([cross, scores[1]], axis=1).astype(BF16), vb)
    yield
    qd, kd = decayed(c)
    st = st_scr[...]
    o = jnp.concatenate([o_top, o_bot], axis=0) + _dot_nt(qd, st.astype(BF16))
    st_scr[...] = jnp.exp(a[c - 1:c, :]) * st + lax.dot_general(vb, kd, (((0,), (0,)), ((), ())),
                                                               preferred_element_type=F32)

    ms = jnp.mean(o * o, axis=-1, keepdims=True)
    y = o * lax.rsqrt(ms + RMS_EPS) * gain_ref[...]
    o_ref[0, :, cols] = (y * _silu(z_ref[0, :, cols].astype(F32))).astype(o_ref.dtype)
    yield


def _hgrn(proj, lb_logits, gain, layer_j, *, c):
    b, l, _ = proj.shape
    tri, masks = _hgrn_constants(c)
    n_even = lb_logits.shape[0]
    heads = HG_HEADS_PER_STEP
    groups = N_HEADS // heads
    width = heads * HEAD

    def col(seg):
        return pl.BlockSpec((1, c, width), lambda bi, h, t, seg=seg: (bi, t, seg * groups + h))

    kern = functools.partial(_hgrn_kernel, layer_j=layer_j, c=c, heads=heads)
    per_head_scratch = [pltpu.VMEM((HEAD, HEAD), F32), pltpu.VMEM((HG_PAD + c, HEAD), F32),
                        pltpu.VMEM((3, c + 2 * HG_PAD, HEAD), F32)]
    return pl.pallas_call(
        kern, grid=(b, groups, l // c),
        in_specs=[col(0), col(1), col(2), col(3),
                  pl.BlockSpec((n_even, width), lambda bi, h, t: (0, h)),
                  pl.BlockSpec((1, HEAD), lambda bi, h, t: (0, 0)),
                  pl.BlockSpec(tri.shape, lambda bi, h, t: (0, 0)),
                  pl.BlockSpec(masks.shape, lambda bi, h, t: (0, 0, 0))],
        out_specs=pl.BlockSpec((1, c, width), lambda bi, h, t: (bi, t, h)),
        out_shape=jax.ShapeDtypeStruct((b, l, BRANCH), BF16),
        scratch_shapes=per_head_scratch * heads,
        compiler_params=_params("parallel", "parallel", "arbitrary"), name="hgrn2",
    )(proj, proj, proj, proj, lb_logits, gain, jnp.asarray(tri, BF16), jnp.asarray(masks))


def _diff_kernel(q_ref, k_ref, v_ref, z_ref, lamv_ref, gain_ref, causal_ref, o_ref, vt_scr, sa_scr, sb_scr,
                 *, lam_init, tq, tk):
    qi = pl.program_id(2)
    n_trips = k_ref.shape[1] // tk

    @pl.when(qi == 0)
    def _():
        def transpose_block(i, carry):
            rows = pl.ds(pl.multiple_of(i * tk, tk), tk)
            vt_scr[i] = v_ref[0, rows, :].astype(F32).T.astype(BF16)
            return carry
        lax.fori_loop(0, n_trips, transpose_block, 0)

    lv = lamv_ref[...]
    d01 = jnp.sum(jnp.sum(lv[0:1] * lv[1:2], axis=-1, keepdims=True), axis=0, keepdims=True)
    d23 = jnp.sum(jnp.sum(lv[2:3] * lv[3:4], axis=-1, keepdims=True), axis=0, keepdims=True)
    lam = jnp.exp(d01) - jnp.exp(d23) + lam_init

    def half_heads(sub):
        q = q_ref[0, sub * tq:(sub + 1) * tq, :].astype(F32) * (DF_DH ** -0.5 * math.log2(math.e))
        lane = lax.broadcasted_iota(jnp.int32, q.shape, 1)
        zero = jnp.zeros_like(q)
        return (jnp.where(lane < DF_DH, q, zero).astype(BF16), jnp.where(lane >= DF_DH, q, zero).astype(BF16))

    parts = DIFF_KEY_PARTS
    rows_per_part = tk // parts

    def scores_part(t, s_buf, hv, hf, kp, diagonal=False):
        start = pl.multiple_of(t * tk + kp * rows_per_part, rows_per_part)
        s = _dot_nt(k_ref[0, pl.ds(start, rows_per_part), :], hv[hf])
        if diagonal:
            s = s + causal_ref[kp * rows_per_part:(kp + 1) * rows_per_part, :]
        s_buf[hf, kp * rows_per_part:(kp + 1) * rows_per_part, :] = s
        return jnp.max(s, axis=0, keepdims=True)

    def rescale(st, top):
        m, l, acc = st
        m_new = jnp.maximum(m, top)
        alpha = jnp.exp2(m - m_new)
        return m_new, alpha * l, alpha * acc

    def absorb_part(t, s_buf, hf, kp, st):
        m, l, acc = st
        lanes = slice(kp * rows_per_part, (kp + 1) * rows_per_part)
        p = jnp.exp2(s_buf[hf, lanes, :] - m)
        return m, l + jnp.sum(p, axis=0, keepdims=True), acc + _dot(vt_scr[t, :, lanes], p.astype(BF16))

    def scores(t, s_buf, hv, diagonal=False):
        return tuple(functools.reduce(jnp.maximum,
                                      [scores_part(t, s_buf, hv, hf, kp, diagonal) for kp in range(parts)])
                     for hf in range(2))

    def absorb(t, s_buf, state, tops):
        out = []
        for hf in range(2):
            st = rescale(state[hf], tops[hf])
            for kp in range(parts):
                st = absorb_part(t, s_buf, hf, kp, st)
            out.append(st)
        return tuple(out)

    def step(t_next, buf_next, hv_next, t_cur, buf_cur, state, tops_cur, diagonal_next=False):
        tops_next, out = [], []
        for hf in range(2):
            st = rescale(state[hf], tops_cur[hf])
            top = None
            for kp in range(parts):
                piece = scores_part(t_next, buf_next, hv_next, hf, kp, diagonal_next)
                top = piece if top is None else jnp.maximum(top, piece)
                st = absorb_part(t_cur, buf_cur, hf, kp, st)
            tops_next.append(top)
            out.append(st)
        return tuple(out), tuple(tops_next)

    def visits(v, n_full, hv, state, tops_cur, n_pairs, cur, oth):
        for _ in range(n_pairs):
            state, tops_oth = step(n_full - (v + 1), oth, hv, n_full - v, cur, state, tops_cur)
            state, tops_cur = step(n_full - (v + 2), cur, hv, n_full - (v + 1), oth, state, tops_oth)
            v = v + 2
        return state, tops_cur

    def loops(n_full, hv, state, tops, cur, oth):
        n_quads = n_full // 4
        state, tops = lax.fori_loop(
            0, n_quads, lambda i, c: visits(4 * i, n_full, hv, c[0], c[1], 2, cur, oth), (state, tops))
        n_pairs = (n_full - 4 * n_quads) // 2
        return lax.fori_loop(
            0, n_pairs, lambda i, c: visits(4 * n_quads + 2 * i, n_full, hv, c[0], c[1], 1, cur, oth),
            (state, tops))

    def finish(state, sub):
        (_, l1, acc1), (_, l2, acc2) = state
        rows = slice(sub * tq, (sub + 1) * tq)
        o = (acc1 / l1 - lam * (acc2 / l2)).T
        ms = jnp.mean(o * o, axis=-1, keepdims=True)
        y = o * lax.rsqrt(ms + RMS_EPS) * gain_ref[...]
        o_ref[0, rows, :] = (y * (1.0 - lam_init) * _silu(z_ref[0, rows, :].astype(F32))).astype(o_ref.dtype)

    def fresh():
        return tuple((jnp.full((1, tq), NEG, F32), jnp.zeros((1, tq), F32), jnp.zeros((HEAD, tq), F32))
                     for _ in range(2))

    hv_a, hv_b = half_heads(0), half_heads(1)
    n_a, n_b = 2 * qi, 2 * qi + 1
    state, tops = loops(n_a, hv_a, fresh(), scores(n_a, sa_scr, hv_a, diagonal=True), sa_scr, sb_scr)
    state, tops_b = step(n_b, sb_scr, hv_b, 0, sa_scr, state, tops, diagonal_next=True)
    finish(state, 0)
    state, tops = loops(n_b, hv_b, fresh(), tops_b, sb_scr, sa_scr)
    state, tops_a = step(0, sa_scr, hv_b, 1, sb_scr, state, tops)
    finish(absorb(0, sa_scr, state, tops_a), 1)


def _diff_attn(proj, lam_vec, gain, lam_init, *, tq):
    b, l, _ = proj.shape

    def qcol(seg):
        return pl.BlockSpec((1, 2 * tq, HEAD), lambda bi, h, t, seg=seg: (bi, t, seg * N_HEADS + h))

    def kvcol(seg):
        return pl.BlockSpec((1, l, HEAD), lambda bi, h, t, seg=seg: (bi, 0, seg * N_HEADS + h))

    tk = tq
    key = np.arange(tk)[:, None]
    causal = np.where(key <= np.arange(tq)[None, :], 0.0, NEG).astype(np.float32)
    kern = functools.partial(_diff_kernel, lam_init=lam_init, tq=tq, tk=tk)
    return pl.pallas_call(
        kern, grid=(b, N_HEADS, l // (2 * tq)),
        in_specs=[qcol(4), kvcol(5), kvcol(6), qcol(7),
                  pl.BlockSpec(lam_vec.shape, lambda bi, h, t: (0, 0)),
                  pl.BlockSpec((1, HEAD), lambda bi, h, t: (0, 0)),
                  pl.BlockSpec((tk, tq), lambda bi, h, t: (0, 0))],
        out_specs=pl.BlockSpec((1, 2 * tq, HEAD), lambda bi, h, t: (bi, t, h)),
        out_shape=jax.ShapeDtypeStruct((b, l, BRANCH), BF16),
        scratch_shapes=[pltpu.VMEM((l // tk, HEAD, tk), BF16), pltpu.VMEM((2, tk, tq), F32), pltpu.VMEM((2, tk, tq), F32)],
        compiler_params=_params("parallel", "parallel", "arbitrary"), name="diff_attn",
    )(proj, proj, proj, proj, lam_vec, gain, jnp.asarray(causal))


def _sb_kernel(q_ref, k_ref, v_ref, z_ref, o_ref, *, tq, heads):
    qi = pl.program_id(2)
    cols = [slice(hd * HEAD, (hd + 1) * HEAD) for hd in range(heads)]
    q = [(q_ref[0, :, c].astype(F32) * (HEAD ** -0.5)).astype(BF16) for c in cols]
    r = lax.broadcasted_iota(jnp.int32, (tq, tq), 0)
    cc = lax.broadcasted_iota(jnp.int32, (tq, tq), 1)
    later = (r > cc).astype(BF16)

    def local(kj, masked):
        rows = pl.ds(pl.multiple_of(kj * tq, tq), tq)
        zz = [_dot_nt(qh, k_ref[0, rows, c]) for qh, c in zip(q, cols)]
        sp = [_softplus(x) for x in zz]
        log_rest = [-x for x in sp]
        if masked:
            log_rest = [jnp.where(cc < r, x, 0.0) for x in log_rest]
        hi = [x.astype(BF16) for x in log_rest]
        lo = [(x - h.astype(F32)).astype(BF16) for x, h in zip(log_rest, hi)]
        inner = [z - s + (_dot(h, later) + _dot(lw, later)) for z, s, h, lw in zip(zz, sp, hi, lo)]
        return inner, [jnp.sum(x, axis=-1, keepdims=True) for x in log_rest], rows

    def weigh(part, run, masked):
        inner, total, rows = part
        w = [jnp.exp(x + rn) for x, rn in zip(inner, run)]
        if masked:
            w = [jnp.where(cc < r, x, 0.0) for x in w]
        pv = [_dot(x.astype(BF16), v_ref[0, rows, c]) for x, c in zip(w, cols)]
        return pv, [rn + t for rn, t in zip(run, total)]

    def alive(run):
        top = run[0]
        for rn in run[1:]:
            top = jnp.maximum(top, rn)
        return jnp.max(top) > -SB_SKIP

    diag = local(qi, True)
    prev = local(jnp.maximum(qi - 1, 0), False)
    acc, run = weigh(diag, [jnp.zeros((tq, 1), F32)] * heads, True)
    has_prev = (qi > 0).astype(F32)
    pv, run_prev = weigh(prev, run, False)
    acc = [a + has_prev * p for a, p in zip(acc, pv)]
    run = [rn + has_prev * (rp - rn) for rn, rp in zip(run, run_prev)]

    def cond(carry):
        kj, go, _, _ = carry
        return jnp.logical_and(kj >= 0, go)

    def body(carry):
        kj, _, run, acc = carry
        pv, run = weigh(local(kj, False), run, False)
        return kj - 1, alive(run), run, [a + p for a, p in zip(acc, pv)]

    _, _, _, acc = lax.while_loop(cond, body, (qi - 2, alive(run), run, acc))
    for a, c in zip(acc, cols):
        o_ref[0, :, c] = (a * _silu(z_ref[0, :, c].astype(F32))).astype(o_ref.dtype)


def _stick_breaking(proj, *, tq):
    b, l, _ = proj.shape
    heads = SB_HEADS_PER_STEP
    groups = N_HEADS // heads
    width = heads * HEAD

    def qcol(seg):
        return pl.BlockSpec((1, tq, width), lambda bi, h, t, seg=seg: (bi, t, seg * groups + h))

    def kvcol(seg):
        return pl.BlockSpec((1, l, width), lambda bi, h, t, seg=seg: (bi, 0, seg * groups + h))

    return pl.pallas_call(
        functools.partial(_sb_kernel, tq=tq, heads=heads), grid=(b, groups, l // tq),
        in_specs=[qcol(4), kvcol(5), kvcol(6), qcol(7)],
        out_specs=pl.BlockSpec((1, tq, width), lambda bi, h, t: (bi, t, h)),
        out_shape=jax.ShapeDtypeStruct((b, l, BRANCH), BF16),
        compiler_params=_params("parallel", "parallel", "arbitrary"), name="stick_breaking",
    )(proj, proj, proj, proj)


def _split2(x):
    hi = x.astype(BF16)
    return hi, (x - hi.astype(F32)).astype(BF16)


def _dot3(a, b):
    a_hi, a_lo = _split2(a)
    b_hi, b_lo = _split2(b)
    return _dot(a_hi, b_hi) + (_dot(a_hi, b_lo) + _dot(a_lo, b_hi))


def _dot_const3(const_bf16, x):
    hi = x.astype(BF16)
    r1 = x - hi.astype(F32)
    mid = r1.astype(BF16)
    lo = (r1 - mid.astype(F32)).astype(BF16)
    return _dot(const_bf16, hi) + (_dot(const_bf16, mid) + _dot(const_bf16, lo))


def _each(fn, *lists):
    return [fn(*args) for args in zip(*lists)]


def _unit_lower_inverse_steps(lws, eye, blockdiag, out):
    ld = [lw * blockdiag for lw in lws]
    lo = _each(lambda a, b: a - b, lws, ld)
    td = [eye - x for x in ld]
    p = _each(_dot3, ld, ld)
    yield
    n_factors = int(math.log2(INV_BLOCK)) - 1
    for i in range(n_factors):
        td = _each(lambda a, b: a + _dot3(a, b), td, p)
        if i < n_factors - 1:
            p = _each(_dot3, p, p)
        yield
    n = _each(_dot3, td, lo)
    yield
    n2 = _each(_dot3, n, n)
    yield
    y = [eye - x for x in n]
    y = _each(lambda a, b: a + _dot3(a, b), y, n2)
    yield
    out.extend(_each(_dot3, y, td))
    yield


_DONE = object()


def _run_staggered(chains, lag):
    live = list(chains)
    for ahead, chain in enumerate(reversed(live)):
        for _ in range(ahead * lag):
            next(chain, None)
    while live:
        live = [chain for chain in live if next(chain, _DONE) is not _DONE]


def _gdn_prep_kernel(q_ref, k_ref, v_ref, qh_ref, kh_ref, vh_ref, ab_ref, wq_ref, wk_ref, wv_ref, alog_ref, dtb_ref,
                     u_ref, w_ref, qg_ref, kd_ref, at_ref, gl_ref, ext_scr, *, n_chunks, c):
    h = pl.program_id(1)
    t = pl.program_id(2)
    t_blk = n_chunks * c

    for idx, (x_ref, halo_ref) in enumerate(((q_ref, qh_ref), (k_ref, kh_ref), (v_ref, vh_ref))):
        ext_scr[idx, 0:GD_HALO, :] = jnp.where(t > 0, halo_ref[0].astype(F32), 0.0)
        ext_scr[idx, GD_HALO:GD_HALO + t_blk, :] = x_ref[0].astype(F32)

    def conv(w_ref, idx, row0, n_rows):
        w = w_ref[...]
        y = w[GD_CONV - 1:GD_CONV] * ext_scr[idx, pl.ds(GD_HALO + row0, n_rows), :]
        for tap in range(GD_CONV - 1):
            y = y + w[tap:tap + 1] * ext_scr[idx, pl.ds(GD_HALO - (GD_CONV - 1) + tap + row0, n_rows), :]
        return _silu(y)

    def l2n(x):
        return x * lax.rsqrt(jnp.sum(x * x, axis=-1, keepdims=True) + RMS_EPS)

    r = lax.broadcasted_iota(jnp.int32, (c, c), 0)
    cc = lax.broadcasted_iota(jnp.int32, (c, c), 1)
    incl = cc <= r
    strict = cc < r
    tri = incl.astype(BF16)
    eye = (cc == r).astype(F32)
    blockdiag = ((r // INV_BLOCK) == (cc // INV_BLOCK)).astype(F32)
    r2 = lax.broadcasted_iota(jnp.int32, (c, HEAD), 0)
    c2 = lax.broadcasted_iota(jnp.int32, (c, HEAD), 1)
    strict_wide = jnp.logical_and(c2 < r2, c2 < c).astype(F32)

    def chain(ids):
        row0, n_rows = ids[0] * c, len(ids) * c
        local = [slice(i * c, (i + 1) * c) for i in range(len(ids))]
        chunks = [slice(ci * c, (ci + 1) * c) for ci in ids]
        q_rows = l2n(conv(wq_ref, 0, row0, n_rows)) * (HEAD ** -0.5)
        yield
        k_rows = l2n(conv(wk_ref, 1, row0, n_rows))
        yield
        v_rows = conv(wv_ref, 2, row0, n_rows)
        ab = ab_ref[0, row0:row0 + n_rows, :]
        lane = lax.broadcasted_iota(jnp.int32, ab.shape, 1)
        g_lanes = -jnp.exp(alog_ref[...]) * _softplus(ab + dtb_ref[...])
        g_rows = jnp.sum(jnp.where(lane == h, g_lanes, 0.0), axis=-1, keepdims=True)
        beta_rows = jnp.sum(jnp.where(lane == h + N_HEADS, _sigmoid(ab), 0.0), axis=-1, keepdims=True)
        yield
        q = [q_rows[sl] for sl in local]
        k = [k_rows[sl] for sl in local]
        v = [v_rows[sl] for sl in local]
        beta = [beta_rows[sl] for sl in local]
        g_wide = [jnp.broadcast_to(g_rows[sl], (c, HEAD)) for sl in local]
        sums = [_dot_const3(tri, jnp.concatenate([gw, gw * strict_wide], axis=1)) for gw in g_wide]
        yield
        gc = [s[:, 0:HEAD] for s in sums]
        decay = [jnp.exp(jnp.where(incl, s[:, HEAD:HEAD + c], NEG)) for s in sums]
        kb = _each(lambda a, b: a * b, k, beta)
        kbf = [x.astype(BF16) for x in k]
        lw = _each(lambda a, b, d: jnp.where(strict, _dot_nt(a.astype(BF16), b) * d, 0.0), kb, kbf, decay)
        attn = _each(lambda a, b, d: _dot_nt(a.astype(BF16), b) * d, q, kbf, decay)
        yield
        inverse = []
        yield from _unit_lower_inverse_steps(lw, eye, blockdiag, inverse)
        tinv = [x.astype(BF16) for x in inverse]
        egc = [jnp.exp(x) for x in gc]
        u = _each(lambda ti, a, b: _dot(ti, (a * b).astype(BF16)), tinv, v, beta)
        w = _each(lambda ti, a, e: _dot(ti, (a * e).astype(BF16)), tinv, kb, egc)
        for i, (ci, sl) in enumerate(zip(ids, chunks)):
            gc_last = gc[i][c - 1:c, :]
            u_ref[0, 0, sl, :] = u[i]
            w_ref[0, 0, sl, :] = w[i].astype(BF16)
            qg_ref[0, 0, sl, :] = (q[i] * egc[i]).astype(BF16)
            kd_ref[0, 0, sl, :] = (k[i] * jnp.exp(gc_last - gc[i])).astype(BF16)
            at_ref[0, 0, sl, :] = attn[i].astype(BF16)
            gl_ref[0, 0, ci] = jnp.exp(gc_last)
        yield

    n_groups = min(GD_GROUPS, n_chunks)
    per_group = n_chunks // n_groups
    _run_staggered([chain(range(gi * per_group, (gi + 1) * per_group)) for gi in range(n_groups)], GD_STAGGER)


def _gdn_scan_kernel(u_ref, w_ref, qg_ref, kd_ref, at_ref, gl_ref, z_ref, gain_ref, o_ref, s_scr, *, n_chunks, c, rows_b):
    @pl.when(pl.program_id(1) == 0)
    def _():
        s_scr[...] = jnp.zeros_like(s_scr)

    lanes = [(bb, h) for bb in range(rows_b) for h in range(N_HEADS)]
    gain = gain_ref[...]
    s = [s_scr[bb * N_HEADS + h] for bb, h in lanes]
    for ci in range(n_chunks):
        rows = pl.ds(ci * c, c)
        sb = [x.astype(BF16) for x in s]
        v_new = [u_ref[bb, h, rows, :] - _dot(w_ref[bb, h, rows, :], x) for (bb, h), x in zip(lanes, sb)]
        vb = [x.astype(BF16) for x in v_new]
        o = [_dot(qg_ref[bb, h, rows, :], x) + _dot(at_ref[bb, h, rows, :], y)
             for (bb, h), x, y in zip(lanes, sb, vb)]
        s = [gl_ref[bb, h, ci] * x
             + lax.dot_general(kd_ref[bb, h, rows, :], y, (((0,), (0,)), ((), ())), preferred_element_type=F32)
             for (bb, h), x, y in zip(lanes, s, vb)]
        for (bb, h), oh in zip(lanes, o):
            cols = slice(h * HEAD, (h + 1) * HEAD)
            ms = jnp.mean(oh * oh, axis=-1, keepdims=True)
            y = oh * lax.rsqrt(ms + RMS_EPS) * gain
            o_ref[bb, rows, cols] = (y * _silu(z_ref[bb, rows, cols].astype(F32))).astype(o_ref.dtype)
    for i, x in enumerate(s):
        s_scr[i] = x


def _gdn(proj, ab, conv_w, a_log, dt_bias, gain, *, t_prep, t_scan, c):
    b, l, _ = proj.shape
    halo_blocks = t_prep // GD_HALO

    def col(seg):
        return pl.BlockSpec((1, t_prep, HEAD), lambda bi, h, t, seg=seg: (bi, t, seg * N_HEADS + h))

    def halo(seg):
        return pl.BlockSpec((1, GD_HALO, HEAD),
                            lambda bi, h, t, seg=seg: (bi, jnp.maximum(t * halo_blocks - 1, 0), seg * N_HEADS + h))

    def wcol(seg):
        return pl.BlockSpec((GD_CONV, HEAD), lambda bi, h, t, seg=seg: (0, seg * N_HEADS + h))

    one = pl.BlockSpec((1, HEAD), lambda bi, h, t: (0, 0))

    def per_head(width):
        return pl.BlockSpec((1, 1, t_prep, width), lambda bi, h, t: (bi, h, t, 0))

    n_prep = t_prep // c
    u, w, qg, kd, at, gl = pl.pallas_call(
        functools.partial(_gdn_prep_kernel, n_chunks=n_prep, c=c), grid=(b, N_HEADS, l // t_prep),
        in_specs=[col(0), col(1), col(2), halo(0), halo(1), halo(2),
                  pl.BlockSpec((1, t_prep, HEAD), lambda bi, h, t: (bi, t, 0)),
                  wcol(0), wcol(1), wcol(2), one, one],
        out_specs=[per_head(HEAD), per_head(HEAD), per_head(HEAD), per_head(HEAD), per_head(c),
                   pl.BlockSpec((1, 1, n_prep, 1, HEAD), lambda bi, h, t: (bi, h, t, 0, 0))],
        out_shape=[jax.ShapeDtypeStruct((b, N_HEADS, l, HEAD), F32),
                   jax.ShapeDtypeStruct((b, N_HEADS, l, HEAD), BF16),
                   jax.ShapeDtypeStruct((b, N_HEADS, l, HEAD), BF16),
                   jax.ShapeDtypeStruct((b, N_HEADS, l, HEAD), BF16),
                   jax.ShapeDtypeStruct((b, N_HEADS, l, c), BF16),
                   jax.ShapeDtypeStruct((b, N_HEADS, l // c, 1, HEAD), F32)],
        scratch_shapes=[pltpu.VMEM((3, t_prep + GD_HALO, HEAD), F32)],
        compiler_params=_params("parallel", "parallel", "parallel"), name="gdn_prep",
    )(proj, proj, proj, proj, proj, proj, ab, conv_w, conv_w, conv_w, a_log, dt_bias)

    rows_b = GD_SCAN_BATCH_ROWS if b % GD_SCAN_BATCH_ROWS == 0 else 1

    def all_heads(width):
        return pl.BlockSpec((rows_b, N_HEADS, t_scan, width), lambda bi, t: (bi, 0, t, 0))

    n_scan = t_scan // c
    return pl.pallas_call(
        functools.partial(_gdn_scan_kernel, n_chunks=n_scan, c=c, rows_b=rows_b), grid=(b // rows_b, l // t_scan),
        in_specs=[all_heads(HEAD), all_heads(HEAD), all_heads(HEAD), all_heads(HEAD), all_heads(c),
                  pl.BlockSpec((rows_b, N_HEADS, n_scan, 1, HEAD), lambda bi, t: (bi, 0, t, 0, 0)),
                  pl.BlockSpec((rows_b, t_scan, BRANCH), lambda bi, t: (bi, t, 3)),
                  pl.BlockSpec((1, HEAD), lambda bi, t: (0, 0))],
        out_specs=pl.BlockSpec((rows_b, t_scan, BRANCH), lambda bi, t: (bi, t, 0)),
        out_shape=jax.ShapeDtypeStruct((b, l, BRANCH), BF16),
        scratch_shapes=[pltpu.VMEM((rows_b * N_HEADS, HEAD, HEAD), F32)],
        compiler_params=_params("parallel", "arbitrary"), name="gdn_scan",
    )(u, w, qg, kd, at, gl, proj, gain)


def _pad_lanes(vec):
    return jnp.zeros((1, HEAD), F32).at[0, :vec.shape[0]].set(vec.astype(F32))


def kernel(x, norm_pre, norm_post, ev_w_in, ev_w_out, hg_lb_logits, hg_norm, df_lambda, df_norm,
           od_w_in, od_w_out, gd_conv, gd_a_log, gd_dt_bias, gd_norm):
    p = dict(norm_pre=norm_pre, norm_post=norm_post, ev_w_in=ev_w_in, ev_w_out=ev_w_out,
             hg_lb_logits=hg_lb_logits, hg_norm=hg_norm, df_lambda=df_lambda, df_norm=df_norm,
             od_w_in=od_w_in, od_w_out=od_w_out, gd_conv=gd_conv, gd_a_log=gd_a_log,
             gd_dt_bias=gd_dt_bias, gd_norm=gd_norm)
    return _run_layers(x, p, range(norm_pre.shape[0]))


def _run_layers(x, p, layers):
    norm_pre, norm_post = p["norm_pre"], p["norm_post"]
    ev_w_in, ev_w_out, od_w_in, od_w_out = p["ev_w_in"], p["ev_w_out"], p["od_w_in"], p["od_w_out"]
    hg_lb_logits, hg_norm, df_lambda, df_norm = p["hg_lb_logits"], p["hg_norm"], p["df_lambda"], p["df_norm"]
    gd_conv, gd_a_log, gd_dt_bias, gd_norm = p["gd_conv"], p["gd_a_log"], p["gd_dt_bias"], p["gd_norm"]
    b, l, d = x.shape
    m = b * l
    tm_in = min(1024, m)
    tm_out = min(512, m)
    t_rec = min(REC_BLOCK, l)
    t_att = min(ATT_BLOCK, l)
    x2 = x.reshape(m, d)
    for layer in layers:
        j = layer // 2
        g_pre = norm_pre[layer].reshape(1, d)
        g_post = norm_post[layer].reshape(1, d)
        if layer % 2 == 0:
            proj = _inproj(x2, g_pre, ev_w_in[j].astype(BF16), tm=tm_in, tn=IN_TILE_N).reshape(b, l, -1)
            lam_init = 0.8 - 0.6 * math.exp(-0.3 * layer)
            mix_a = _hgrn(proj, hg_lb_logits.astype(F32), hg_norm[j].reshape(1, HEAD), j, c=min(HG_CHUNK, l))
            mix_b = _diff_attn(proj, df_lambda[j].astype(F32), df_norm[j].reshape(1, HEAD), lam_init,
                               tq=min(DIFF_Q_BLOCK, l // 2))
            w_out = ev_w_out[j].astype(BF16)
        else:
            w = od_w_in[j]
            w_main = jnp.concatenate([w[:, :4 * BRANCH], w[:, 4 * BRANCH + 2 * N_HEADS:]], axis=1).astype(BF16)
            w_ab = jnp.pad(w[:, 4 * BRANCH:4 * BRANCH + 2 * N_HEADS], ((0, 0), (0, HEAD - 2 * N_HEADS))).astype(BF16)
            proj, ab = _inproj(x2, g_pre, w_main, w_ab, tm=tm_in, tn=IN_TILE_N)
            proj = proj.reshape(b, l, -1)
            mix_a = _gdn(proj, ab.reshape(b, l, HEAD), gd_conv[j].astype(F32), _pad_lanes(gd_a_log[j]),
                         _pad_lanes(gd_dt_bias[j]), gd_norm[j].reshape(1, HEAD), t_prep=min(GD_PREP_BLOCK, l), t_scan=t_rec, c=CHUNK)
            mix_b = _stick_breaking(proj, tq=t_att)
            w_out = od_w_out[j].astype(BF16)
        x2 = _outproj(mix_a.reshape(m, BRANCH), mix_b.reshape(m, BRANCH), w_out[:BRANCH], w_out[BRANCH:],
                      g_post, x2, tm=tm_out)
    return x2.reshape(b, l, d)
```

```python
import functools
import math

import numpy as np
import jax
import jax.numpy as jnp
from jax import lax
from jax.experimental import pallas as pl
from jax.experimental.pallas import tpu as pltpu

F32 = jnp.float32
BF16 = jnp.bfloat16
HIGHEST = lax.Precision.HIGHEST

D_MODEL = 2048
BRANCH = D_MODEL // 2
N_HEADS = 8
HEAD = BRANCH // N_HEADS
DF_DH = HEAD // 2
GD_CONV = 4
RMS_EPS = 1e-6
NEG = -1e30
GATE_FLOOR = 1e-20
SB_SKIP = 104.0

V7X_VMEM_LIMIT = 56 * 1024 * 1024

IN_TILE_N = 2048
CHUNK = 64
HG_CHUNK = 256
HG_HEADS_PER_STEP = 8
HG_STAGGER = 5
REC_BLOCK = 256
ATT_BLOCK = 256
SB_HEADS_PER_STEP = 4
DIFF_Q_BLOCK = 512
DIFF_BLOCKS_PER_STEP = 4
DIFF_KEY_PARTS = 2
INV_BLOCK = 16
GD_PREP_BLOCK = 2048
GD_HALO = 16
GD_SCAN_BATCH_ROWS = 2
GD_GROUPS = 4
GD_STAGGER = 2


def _params(*sem):
    return pltpu.CompilerParams(dimension_semantics=sem, vmem_limit_bytes=V7X_VMEM_LIMIT)


def _sigmoid(x):
    return 0.5 * jnp.tanh(0.5 * x) + 0.5


def _silu(x):
    return x * _sigmoid(x)


def _softplus(x):
    return jnp.maximum(x, 0.0) + jnp.log(1.0 + jnp.exp(-jnp.abs(x)))


def _dot(a, b):
    return jnp.dot(a, b, preferred_element_type=F32)


def _dot_nt(a, b):
    return lax.dot_general(a, b, (((1,), (1,)), ((), ())), preferred_element_type=F32)


def _dot_exact(a, b):
    return jnp.dot(a, b, precision=HIGHEST, preferred_element_type=F32)


def _inproj_kernel(x_ref, g_ref, w_ref, o_ref, h_scr):
    @pl.when(pl.program_id(1) == 0)
    def _():
        x = x_ref[...]
        ms = jnp.mean(x * x, axis=-1, keepdims=True)
        h_scr[...] = (x * lax.rsqrt(ms + RMS_EPS) * g_ref[...]).astype(BF16)

    o_ref[...] = _dot(h_scr[...], w_ref[...]).astype(o_ref.dtype)


def _inproj_ab_kernel(x_ref, g_ref, w_ref, wab_ref, o_ref, oab_ref, h_scr):
    @pl.when(pl.program_id(1) == 0)
    def _():
        x = x_ref[...]
        ms = jnp.mean(x * x, axis=-1, keepdims=True)
        h_scr[...] = (x * lax.rsqrt(ms + RMS_EPS) * g_ref[...]).astype(BF16)
        oab_ref[...] = _dot(h_scr[...], wab_ref[...])

    o_ref[...] = _dot(h_scr[...], w_ref[...]).astype(o_ref.dtype)


def _inproj(x2, gain, w, w_ab=None, *, tm, tn):
    m, d = x2.shape
    n = w.shape[1]
    grid = (m // tm, n // tn)
    x_spec = pl.BlockSpec((tm, d), lambda i, j: (i, 0))
    g_spec = pl.BlockSpec((1, d), lambda i, j: (0, 0))
    w_spec = pl.BlockSpec((d, tn), lambda i, j: (0, j))
    o_spec = pl.BlockSpec((tm, tn), lambda i, j: (i, j))
    scratch = [pltpu.VMEM((tm, d), BF16)]
    if w_ab is None:
        return pl.pallas_call(
            _inproj_kernel, grid=grid, in_specs=[x_spec, g_spec, w_spec], out_specs=o_spec,
            out_shape=jax.ShapeDtypeStruct((m, n), BF16), scratch_shapes=scratch,
            compiler_params=_params("parallel", "arbitrary"), name="inproj",
        )(x2, gain, w)
    nab = w_ab.shape[1]
    return pl.pallas_call(
        _inproj_ab_kernel, grid=grid,
        in_specs=[x_spec, g_spec, w_spec, pl.BlockSpec((d, nab), lambda i, j: (0, 0))],
        out_specs=[o_spec, pl.BlockSpec((tm, nab), lambda i, j: (i, 0))],
        out_shape=[jax.ShapeDtypeStruct((m, n), BF16), jax.ShapeDtypeStruct((m, nab), F32)],
        scratch_shapes=scratch, compiler_params=_params("parallel", "arbitrary"), name="inproj_ab",
    )(x2, gain, w, w_ab)


def _outproj_kernel(ma_ref, mb_ref, wa_ref, wb_ref, g_ref, x_ref, o_ref):
    y = _dot(ma_ref[...], wa_ref[...]) + _dot(mb_ref[...], wb_ref[...])
    ms = jnp.mean(y * y, axis=-1, keepdims=True)
    o_ref[...] = x_ref[...] + y * lax.rsqrt(ms + RMS_EPS) * g_ref[...]


def _outproj(mix_a, mix_b, w_a, w_b, gain, x2, *, tm):
    m, d = x2.shape
    k = mix_a.shape[1]
    return pl.pallas_call(
        _outproj_kernel, grid=(m // tm,),
        in_specs=[pl.BlockSpec((tm, k), lambda i: (i, 0)), pl.BlockSpec((tm, k), lambda i: (i, 0)),
                  pl.BlockSpec((k, d), lambda i: (0, 0)), pl.BlockSpec((k, d), lambda i: (0, 0)),
                  pl.BlockSpec((1, d), lambda i: (0, 0)), pl.BlockSpec((tm, d), lambda i: (i, 0))],
        out_specs=pl.BlockSpec((tm, d), lambda i: (i, 0)),
        out_shape=jax.ShapeDtypeStruct((m, d), F32),
        compiler_params=_params("parallel"), name="outproj",
    )(mix_a, mix_b, w_a, w_b, gain, x2)


def _hgrn_constants(c):
    n = c // 2
    t = np.arange(n)
    masks = [np.eye(n, dtype=np.float32)]
    b = 1
    while b < n:
        mid = (t // (2 * b)) * (2 * b) + b
        same = (t[:, None] // (2 * b)) == (t[None, :] // (2 * b))
        masks.append((same & (t[:, None] >= mid[:, None]) & (t[None, :] < mid[:, None])).astype(np.float32))
        b *= 2
    rows = np.arange(c)
    tri = (rows[None, :] <= rows[:, None]).astype(np.float32)
    return tri, np.stack(masks, axis=0)


HG_PAD = 8


def _hgrn_kernel(q_ref, f_ref, i_ref, z_ref, lbl_ref, gain_ref, tri_ref, mask_ref, o_ref, *scratch,
                 layer_j, c, heads):
    @pl.when(pl.program_id(2) == 0)
    def _():
        for hd in range(heads):
            scratch[3 * hd][...] = jnp.zeros((HEAD, HEAD), F32)

    _run_staggered([_hgrn_head(slice(hd * HEAD, (hd + 1) * HEAD), q_ref, f_ref, i_ref, z_ref, lbl_ref, gain_ref,
                               tri_ref, mask_ref, o_ref, *scratch[3 * hd:3 * hd + 3], layer_j=layer_j, c=c)
                    for hd in range(heads)], HG_STAGGER)


def _hgrn_head(cols, q_ref, f_ref, i_ref, z_ref, lbl_ref, gain_ref, tri_ref, mask_ref, o_ref,
               st_scr, a_scr, sh_scr, *, layer_j, c):
    n = c // 2

    logits = lbl_ref[:, cols]
    e = jnp.exp(logits - jnp.max(logits, axis=0, keepdims=True))
    p = e / jnp.sum(e, axis=0, keepdims=True)
    lb = jnp.sum(p[0:layer_j + 1], axis=0, keepdims=True) - p[0:1]

    q = _silu(q_ref[0, :, cols].astype(F32))
    sig = _sigmoid(f_ref[0, :, cols].astype(F32))
    g = jnp.log(jnp.maximum(lb + (1.0 - lb) * sig, GATE_FLOOR))
    k = (1.0 - lb) * (1.0 - sig)
    vb = i_ref[0, :, cols]
    qb = q.astype(BF16)
    kb = k.astype(BF16)
    yield

    a = _dot_const3(tri_ref[...], g)
    zeros_pad = jnp.zeros((HG_PAD, HEAD), F32)
    a_scr[0:HG_PAD, :] = zeros_pad
    a_scr[HG_PAD:HG_PAD + c, :] = a
    yield

    def seg_rows(b, first):
        return jnp.concatenate([jnp.broadcast_to(a_scr[pl.ds(first + j * b, 1), :], (b, HEAD))
                                for j in range(c // b)], axis=0)

    def shifted(x, slot):
        sh_scr[slot, 0:HG_PAD, :] = zeros_pad
        sh_scr[slot, HG_PAD + c:HG_PAD + c + HG_PAD, :] = zeros_pad
        sh_scr[slot, HG_PAD:HG_PAD + c, :] = x
        return lambda d: sh_scr[slot, pl.ds(HG_PAD + d, c), :]

    row = lax.broadcasted_iota(jnp.int32, (c, HEAD), 0)
    g_at = shifted(g, 0)
    odd = (row & 1) == 1
    up = {1: g, 2: g + jnp.where(odd, g_at(-1), 0.0)}
    lo = {1: None, 2: jnp.where(odd, 0.0, g_at(1))}
    up2_at = shifted(up[2], 1)
    suffix2_at = shifted(lo[2] + g, 2)
    pos = row & 3
    up[4] = up[2] + jnp.where(pos == 2, up2_at(-1), jnp.where(pos == 3, up2_at(-2), 0.0))
    lo[4] = lo[2] + jnp.where(pos == 0, suffix2_at(2), jnp.where(pos == 1, suffix2_at(1), 0.0))
    b = 8
    while b <= c:
        up[b] = a - seg_rows(b, HG_PAD - 1)
        lo[b] = seg_rows(b, HG_PAD - 1 + b) - a
        b *= 2
    yield

    def decayed(b):
        qd = (q * jnp.exp(up[b])).astype(BF16)
        kd = kb if lo[b] is None else (k * jnp.exp(lo[b])).astype(BF16)
        return qd, kd

    halves = (slice(0, n), slice(n, c))
    scores = [mask_ref[0] * _dot_nt(qb[h], kb[h]) for h in halves]
    b, lv = 1, 1
    while b < n:
        qd, kd = decayed(b)
        scores = [s + mask_ref[lv] * _dot_nt(qd[h], kd[h]) for s, h in zip(scores, halves)]
        b, lv = 2 * b, lv + 1
        yield
    qd, kd = decayed(n)
    cross = _dot_nt(qd[halves[1]], kd[halves[0]])
    o_top = _dot(scores[0].astype(BF16), vb[halves[0]])
    o_bot = _dot(jnp.concatenate([cross, scores[1]], axis=1).astype(BF16), vb)
    yield
    qd, kd = decayed(c)
    st = st_scr[...]
    o = jnp.concatenate([o_top, o_bot], axis=0) + _dot_nt(qd, st.astype(BF16))
    st_scr[...] = jnp.exp(a[c - 1:c, :]) * st + lax.dot_general(vb, kd, (((0,), (0,)), ((), ())),
                                                               preferred_element_type=F32)

    ms = jnp.mean(o * o, axis=-1, keepdims=True)
    y = o * lax.rsqrt(ms + RMS_EPS) * gain_ref[...]
    o_ref[0, :, cols] = (y * _silu(z_ref[0, :, cols].astype(F32))).astype(o_ref.dtype)
    yield


def _hgrn(proj, lb_logits, gain, layer_j, *, c):
    b, l, _ = proj.shape
    tri, masks = _hgrn_constants(c)
    n_even = lb_logits.shape[0]
    heads = HG_HEADS_PER_STEP
    groups = N_HEADS // heads
    width = heads * HEAD

    def col(seg):
        return pl.BlockSpec((1, c, width), lambda bi, h, t, seg=seg: (bi, t, seg * groups + h))

    kern = functools.partial(_hgrn_kernel, layer_j=layer_j, c=c, heads=heads)
    per_head_scratch = [pltpu.VMEM((HEAD, HEAD), F32), pltpu.VMEM((HG_PAD + c, HEAD), F32),
                        pltpu.VMEM((3, c + 2 * HG_PAD, HEAD), F32)]
    return pl.pallas_call(
        kern, grid=(b, groups, l // c),
        in_specs=[col(0), col(1), col(2), col(3),
                  pl.BlockSpec((n_even, width), lambda bi, h, t: (0, h)),
                  pl.BlockSpec((1, HEAD), lambda bi, h, t: (0, 0)),
                  pl.BlockSpec(tri.shape, lambda bi, h, t: (0, 0)),
                  pl.BlockSpec(masks.shape, lambda bi, h, t: (0, 0, 0))],
        out_specs=pl.BlockSpec((1, c, width), lambda bi, h, t: (bi, t, h)),
        out_shape=jax.ShapeDtypeStruct((b, l, BRANCH), BF16),
        scratch_shapes=per_head_scratch * heads,
        compiler_params=_params("parallel", "parallel", "arbitrary"), name="hgrn2",
    )(proj, proj, proj, proj, lb_logits, gain, jnp.asarray(tri, BF16), jnp.asarray(masks))


def _diff_kernel(q_ref, k_ref, v_ref, z_ref, lamv_ref, gain_ref, causal_ref, o_ref, vt_scr, sa_scr, sb_scr,
                 *, lam_init, tq, tk):
    qi = pl.program_id(2)
    n_trips = k_ref.shape[1] // tk

    @pl.when(qi == 0)
    def _():
        def transpose_block(i, carry):
            rows = pl.ds(pl.multiple_of(i * tk, tk), tk)
            vt_scr[i] = v_ref[0, rows, :].astype(F32).T.astype(BF16)
            return carry
        lax.fori_loop(0, n_trips, transpose_block, 0)

    lv = lamv_ref[...]
    d01 = jnp.sum(jnp.sum(lv[0:1] * lv[1:2], axis=-1, keepdims=True), axis=0, keepdims=True)
    d23 = jnp.sum(jnp.sum(lv[2:3] * lv[3:4], axis=-1, keepdims=True), axis=0, keepdims=True)
    lam = jnp.exp(d01) - jnp.exp(d23) + lam_init

    def half_heads(sub):
        q = q_ref[0, sub * tq:(sub + 1) * tq, :].astype(F32) * (DF_DH ** -0.5 * math.log2(math.e))
        lane = lax.broadcasted_iota(jnp.int32, q.shape, 1)
        zero = jnp.zeros_like(q)
        return (jnp.where(lane < DF_DH, q, zero).astype(BF16), jnp.where(lane >= DF_DH, q, zero).astype(BF16))

    parts = DIFF_KEY_PARTS
    rows_per_part = tk // parts

    def scores_part(t, s_buf, hv, hf, kp, diagonal=False):
        start = pl.multiple_of(t * tk + kp * rows_per_part, rows_per_part)
        s = _dot_nt(k_ref[0, pl.ds(start, rows_per_part), :], hv[hf])
        if diagonal:
            s = s + causal_ref[kp * rows_per_part:(kp + 1) * rows_per_part, :]
        s_buf[hf, kp * rows_per_part:(kp + 1) * rows_per_part, :] = s
        return jnp.max(s, axis=0, keepdims=True)

    def rescale(st, top):
        m, l, acc = st
        m_new = jnp.maximum(m, top)
        alpha = jnp.exp2(m - m_new)
        return m_new, alpha * l, alpha * acc

    def absorb_part(t, s_buf, hf, kp, st):
        m, l, acc = st
        lanes = slice(kp * rows_per_part, (kp + 1) * rows_per_part)
        p = jnp.exp2(s_buf[hf, lanes, :] - m)
        return m, l + jnp.sum(p, axis=0, keepdims=True), acc + _dot(vt_scr[t, :, lanes], p.astype(BF16))

    def scores(t, s_buf, hv, diagonal=False):
        return tuple(functools.reduce(jnp.maximum,
                                      [scores_part(t, s_buf, hv, hf, kp, diagonal) for kp in range(parts)])
                     for hf in range(2))

    def absorb(t, s_buf, state, tops):
        out = []
        for hf in range(2):
            st = rescale(state[hf], tops[hf])
            for kp in range(parts):
                st = absorb_part(t, s_buf, hf, kp, st)
            out.append(st)
        return tuple(out)

    def step(t_next, buf_next, hv_next, t_cur, buf_cur, state, tops_cur, diagonal_next=False):
        tops_next, out = [], []
        for hf in range(2):
            st = rescale(state[hf], tops_cur[hf])
            top = None
            for kp in range(parts):
                piece = scores_part(t_next, buf_next, hv_next, hf, kp, diagonal_next)
                top = piece if top is None else jnp.maximum(top, piece)
                st = absorb_part(t_cur, buf_cur, hf, kp, st)
            tops_next.append(top)
            out.append(st)
        return tuple(out), tuple(tops_next)

    def visits(v, n_full, hv, state, tops_cur, n_pairs, cur, oth):
        for _ in range(n_pairs):
            state, tops_oth = step(n_full - (v + 1), oth, hv, n_full - v, cur, state, tops_cur)
            state, tops_cur = step(n_full - (v + 2), cur, hv, n_full - (v + 1), oth, state, tops_oth)
            v = v + 2
        return state, tops_cur

    def loops(n_full, hv, state, tops, cur, oth):
        n_quads = n_full // 4
        state, tops = lax.fori_loop(
            0, n_quads, lambda i, c: visits(4 * i, n_full, hv, c[0], c[1], 2, cur, oth), (state, tops))
        n_pairs = (n_full - 4 * n_quads) // 2
        return lax.fori_loop(
            0, n_pairs, lambda i, c: visits(4 * n_quads + 2 * i, n_full, hv, c[0], c[1], 1, cur, oth),
            (state, tops))

    def finish(state, sub):
        (_, l1, acc1), (_, l2, acc2) = state
        rows = slice(sub * tq, (sub + 1) * tq)
        o = (acc1 / l1 - lam * (acc2 / l2)).T
        ms = jnp.mean(o * o, axis=-1, keepdims=True)
        y = o * lax.rsqrt(ms + RMS_EPS) * gain_ref[...]
        o_ref[0, rows, :] = (y * (1.0 - lam_init) * _silu(z_ref[0, rows, :].astype(F32))).astype(o_ref.dtype)

    def fresh():
        return tuple((jnp.full((1, tq), NEG, F32), jnp.zeros((1, tq), F32), jnp.zeros((HEAD, tq), F32))
                     for _ in range(2))

    n_blocks = DIFF_BLOCKS_PER_STEP
    hv = [half_heads(sub) for sub in range(n_blocks)]
    cur, oth = sa_scr, sb_scr
    tops = scores(n_blocks * qi, cur, hv[0], diagonal=True)
    for sub in range(n_blocks):
        n_full = n_blocks * qi + sub
        state, tops = loops(n_full, hv[sub], fresh(), tops, cur, oth)
        if sub % 2 == 1:
            state, tops = step(0, oth, hv[sub], 1, cur, state, tops)
            cur, oth = oth, cur
        if sub + 1 < n_blocks:
            state, tops = step(n_full + 1, oth, hv[sub + 1], 0, cur, state, tops, diagonal_next=True)
            cur, oth = oth, cur
        else:
            state = absorb(0, cur, state, tops)
        finish(state, sub)


def _diff_attn(proj, lam_vec, gain, lam_init, *, tq):
    b, l, _ = proj.shape

    def qcol(seg):
        return pl.BlockSpec((1, DIFF_BLOCKS_PER_STEP * tq, HEAD), lambda bi, h, t, seg=seg: (bi, t, seg * N_HEADS + h))

    def kvcol(seg):
        return pl.BlockSpec((1, l, HEAD), lambda bi, h, t, seg=seg: (bi, 0, seg * N_HEADS + h))

    tk = tq
    key = np.arange(tk)[:, None]
    causal = np.where(key <= np.arange(tq)[None, :], 0.0, NEG).astype(np.float32)
    kern = functools.partial(_diff_kernel, lam_init=lam_init, tq=tq, tk=tk)
    return pl.pallas_call(
        kern, grid=(b, N_HEADS, l // (DIFF_BLOCKS_PER_STEP * tq)),
        in_specs=[qcol(4), kvcol(5), kvcol(6), qcol(7),
                  pl.BlockSpec(lam_vec.shape, lambda bi, h, t: (0, 0)),
                  pl.BlockSpec((1, HEAD), lambda bi, h, t: (0, 0)),
                  pl.BlockSpec((tk, tq), lambda bi, h, t: (0, 0))],
        out_specs=pl.BlockSpec((1, DIFF_BLOCKS_PER_STEP * tq, HEAD), lambda bi, h, t: (bi, t, h)),
        out_shape=jax.ShapeDtypeStruct((b, l, BRANCH), BF16),
        scratch_shapes=[pltpu.VMEM((l // tk, HEAD, tk), BF16), pltpu.VMEM((2, tk, tq), F32), pltpu.VMEM((2, tk, tq), F32)],
        compiler_params=_params("parallel", "parallel", "arbitrary"), name="diff_attn",
    )(proj, proj, proj, proj, lam_vec, gain, jnp.asarray(causal))


def _sb_kernel(q_ref, k_ref, v_ref, z_ref, o_ref, *, tq, heads):
    qi = pl.program_id(2)
    cols = [slice(hd * HEAD, (hd + 1) * HEAD) for hd in range(heads)]
    q = [(q_ref[0, :, c].astype(F32) * (HEAD ** -0.5)).astype(BF16) for c in cols]
    r = lax.broadcasted_iota(jnp.int32, (tq, tq), 0)
    cc = lax.broadcasted_iota(jnp.int32, (tq, tq), 1)
    later = (r > cc).astype(BF16)

    def local(kj, masked):
        rows = pl.ds(pl.multiple_of(kj * tq, tq), tq)
        zz = [_dot_nt(qh, k_ref[0, rows, c]) for qh, c in zip(q, cols)]
        sp = [_softplus(x) for x in zz]
        log_rest = [-x for x in sp]
        if masked:
            log_rest = [jnp.where(cc < r, x, 0.0) for x in log_rest]
        hi = [x.astype(BF16) for x in log_rest]
        lo = [(x - h.astype(F32)).astype(BF16) for x, h in zip(log_rest, hi)]
        inner = [z - s + (_dot(h, later) + _dot(lw, later)) for z, s, h, lw in zip(zz, sp, hi, lo)]
        return inner, [jnp.sum(x, axis=-1, keepdims=True) for x in log_rest], rows

    def weigh(part, run, masked):
        inner, total, rows = part
        w = [jnp.exp(x + rn) for x, rn in zip(inner, run)]
        if masked:
            w = [jnp.where(cc < r, x, 0.0) for x in w]
        pv = [_dot(x.astype(BF16), v_ref[0, rows, c]) for x, c in zip(w, cols)]
        return pv, [rn + t for rn, t in zip(run, total)]

    def alive(run):
        top = run[0]
        for rn in run[1:]:
            top = jnp.maximum(top, rn)
        return jnp.max(top) > -SB_SKIP

    diag = local(qi, True)
    prev = local(jnp.maximum(qi - 1, 0), False)
    acc, run = weigh(diag, [jnp.zeros((tq, 1), F32)] * heads, True)
    has_prev = (qi > 0).astype(F32)
    pv, run_prev = weigh(prev, run, False)
    acc = [a + has_prev * p for a, p in zip(acc, pv)]
    run = [rn + has_prev * (rp - rn) for rn, rp in zip(run, run_prev)]

    def cond(carry):
        kj, go, _, _ = carry
        return jnp.logical_and(kj >= 0, go)

    def body(carry):
        kj, _, run, acc = carry
        pv, run = weigh(local(kj, False), run, False)
        return kj - 1, alive(run), run, [a + p for a, p in zip(acc, pv)]

    _, _, _, acc = lax.while_loop(cond, body, (qi - 2, alive(run), run, acc))
    for a, c in zip(acc, cols):
        o_ref[0, :, c] = (a * _silu(z_ref[0, :, c].astype(F32))).astype(o_ref.dtype)


def _stick_breaking(proj, *, tq):
    b, l, _ = proj.shape
    heads = SB_HEADS_PER_STEP
    groups = N_HEADS // heads
    width = heads * HEAD

    def qcol(seg):
        return pl.BlockSpec((1, tq, width), lambda bi, h, t, seg=seg: (bi, t, seg * groups + h))

    def kvcol(seg):
        return pl.BlockSpec((1, l, width), lambda bi, h, t, seg=seg: (bi, 0, seg * groups + h))

    return pl.pallas_call(
        functools.partial(_sb_kernel, tq=tq, heads=heads), grid=(b, groups, l // tq),
        in_specs=[qcol(4), kvcol(5), kvcol(6), qcol(7)],
        out_specs=pl.BlockSpec((1, tq, width), lambda bi, h, t: (bi, t, h)),
        out_shape=jax.ShapeDtypeStruct((b, l, BRANCH), BF16),
        compiler_params=_params("parallel", "parallel", "arbitrary"), name="stick_breaking",
    )(proj, proj, proj, proj)


def _split2(x):
    hi = x.astype(BF16)
    return hi, (x - hi.astype(F32)).astype(BF16)


def _dot3(a, b):
    a_hi, a_lo = _split2(a)
    b_hi, b_lo = _split2(b)
    return _dot(a_hi, b_hi) + (_dot(a_hi, b_lo) + _dot(a_lo, b_hi))


def _dot_const3(const_bf16, x):
    hi = x.astype(BF16)
    r1 = x - hi.astype(F32)
    mid = r1.astype(BF16)
    lo = (r1 - mid.astype(F32)).astype(BF16)
    return _dot(const_bf16, hi) + (_dot(const_bf16, mid) + _dot(const_bf16, lo))


def _each(fn, *lists):
    return [fn(*args) for args in zip(*lists)]


def _unit_lower_inverse_steps(lws, eye, blockdiag, out):
    ld = [lw * blockdiag for lw in lws]
    lo = _each(lambda a, b: a - b, lws, ld)
    td = [eye - x for x in ld]
    p = _each(_dot3, ld, ld)
    yield
    n_factors = int(math.log2(INV_BLOCK)) - 1
    for i in range(n_factors):
        td = _each(lambda a, b: a + _dot3(a, b), td, p)
        if i < n_factors - 1:
            p = _each(_dot3, p, p)
        yield
    n = _each(_dot3, td, lo)
    yield
    n2 = _each(_dot3, n, n)
    yield
    y = [eye - x for x in n]
    y = _each(lambda a, b: a + _dot3(a, b), y, n2)
    yield
    out.extend(_each(_dot3, y, td))
    yield


_DONE = object()


def _run_staggered(chains, lag):
    live = list(chains)
    for ahead, chain in enumerate(reversed(live)):
        for _ in range(ahead * lag):
            next(chain, None)
    while live:
        live = [chain for chain in live if next(chain, _DONE) is not _DONE]


def _gdn_prep_kernel(q_ref, k_ref, v_ref, qh_ref, kh_ref, vh_ref, ab_ref, wq_ref, wk_ref, wv_ref, alog_ref, dtb_ref,
                     u_ref, w_ref, qg_ref, kd_ref, at_ref, gl_ref, ext_scr, *, n_chunks, c):
    h = pl.program_id(1)
    t = pl.program_id(2)
    t_blk = n_chunks * c

    for idx, (x_ref, halo_ref) in enumerate(((q_ref, qh_ref), (k_ref, kh_ref), (v_ref, vh_ref))):
        ext_scr[idx, 0:GD_HALO, :] = jnp.where(t > 0, halo_ref[0].astype(F32), 0.0)
        ext_scr[idx, GD_HALO:GD_HALO + t_blk, :] = x_ref[0].astype(F32)

    def conv(w_ref, idx, row0, n_rows):
        w = w_ref[...]
        y = w[GD_CONV - 1:GD_CONV] * ext_scr[idx, pl.ds(GD_HALO + row0, n_rows), :]
        for tap in range(GD_CONV - 1):
            y = y + w[tap:tap + 1] * ext_scr[idx, pl.ds(GD_HALO - (GD_CONV - 1) + tap + row0, n_rows), :]
        return _silu(y)

    def l2n(x):
        return x * lax.rsqrt(jnp.sum(x * x, axis=-1, keepdims=True) + RMS_EPS)

    r = lax.broadcasted_iota(jnp.int32, (c, c), 0)
    cc = lax.broadcasted_iota(jnp.int32, (c, c), 1)
    incl = cc <= r
    strict = cc < r
    tri = incl.astype(BF16)
    eye = (cc == r).astype(F32)
    blockdiag = ((r // INV_BLOCK) == (cc // INV_BLOCK)).astype(F32)
    r2 = lax.broadcasted_iota(jnp.int32, (c, HEAD), 0)
    c2 = lax.broadcasted_iota(jnp.int32, (c, HEAD), 1)
    strict_wide = jnp.logical_and(c2 < r2, c2 < c).astype(F32)

    def chain(ids):
        row0, n_rows = ids[0] * c, len(ids) * c
        local = [slice(i * c, (i + 1) * c) for i in range(len(ids))]
        chunks = [slice(ci * c, (ci + 1) * c) for ci in ids]
        q_rows = l2n(conv(wq_ref, 0, row0, n_rows)) * (HEAD ** -0.5)
        yield
        k_rows = l2n(conv(wk_ref, 1, row0, n_rows))
        yield
        v_rows = conv(wv_ref, 2, row0, n_rows)
        ab = ab_ref[0, row0:row0 + n_rows, :]
        lane = lax.broadcasted_iota(jnp.int32, ab.shape, 1)
        g_lanes = -jnp.exp(alog_ref[...]) * _softplus(ab + dtb_ref[...])
        g_rows = jnp.sum(jnp.where(lane == h, g_lanes, 0.0), axis=-1, keepdims=True)
        beta_rows = jnp.sum(jnp.where(lane == h + N_HEADS, _sigmoid(ab), 0.0), axis=-1, keepdims=True)
        yield
        q = [q_rows[sl] for sl in local]
        k = [k_rows[sl] for sl in local]
        v = [v_rows[sl] for sl in local]
        beta = [beta_rows[sl] for sl in local]
        g_wide = [jnp.broadcast_to(g_rows[sl], (c, HEAD)) for sl in local]
        sums = [_dot_const3(tri, jnp.concatenate([gw, gw * strict_wide], axis=1)) for gw in g_wide]
        yield
        gc = [s[:, 0:HEAD] for s in sums]
        decay = [jnp.exp(jnp.where(incl, s[:, HEAD:HEAD + c], NEG)) for s in sums]
        kb = _each(lambda a, b: a * b, k, beta)
        kbf = [x.astype(BF16) for x in k]
        lw = _each(lambda a, b, d: jnp.where(strict, _dot_nt(a.astype(BF16), b) * d, 0.0), kb, kbf, decay)
        attn = _each(lambda a, b, d: _dot_nt(a.astype(BF16), b) * d, q, kbf, decay)
        yield
        inverse = []
        yield from _unit_lower_inverse_steps(lw, eye, blockdiag, inverse)
        tinv = [x.astype(BF16) for x in inverse]
        egc = [jnp.exp(x) for x in gc]
        u = _each(lambda ti, a, b: _dot(ti, (a * b).astype(BF16)), tinv, v, beta)
        w = _each(lambda ti, a, e: _dot(ti, (a * e).astype(BF16)), tinv, kb, egc)
        for i, (ci, sl) in enumerate(zip(ids, chunks)):
            gc_last = gc[i][c - 1:c, :]
            u_ref[0, 0, sl, :] = u[i]
            w_ref[0, 0, sl, :] = w[i].astype(BF16)
            qg_ref[0, 0, sl, :] = (q[i] * egc[i]).astype(BF16)
            kd_ref[0, 0, sl, :] = (k[i] * jnp.exp(gc_last - gc[i])).astype(BF16)
            at_ref[0, 0, sl, :] = attn[i].astype(BF16)
            gl_ref[0, 0, ci] = jnp.exp(gc_last)
        yield

    n_groups = min(GD_GROUPS, n_chunks)
    per_group = n_chunks // n_groups
    _run_staggered([chain(range(gi * per_group, (gi + 1) * per_group)) for gi in range(n_groups)], GD_STAGGER)


def _gdn_scan_kernel(u_ref, w_ref, qg_ref, kd_ref, at_ref, gl_ref, z_ref, gain_ref, o_ref, s_scr, *, n_chunks, c, rows_b):
    @pl.when(pl.program_id(1) == 0)
    def _():
        s_scr[...] = jnp.zeros_like(s_scr)

    lanes = [(bb, h) for bb in range(rows_b) for h in range(N_HEADS)]
    gain = gain_ref[...]
    s = [s_scr[bb * N_HEADS + h] for bb, h in lanes]
    for ci in range(n_chunks):
        rows = pl.ds(ci * c, c)
        sb = [x.astype(BF16) for x in s]
        v_new = [u_ref[bb, h, rows, :] - _dot(w_ref[bb, h, rows, :], x) for (bb, h), x in zip(lanes, sb)]
        vb = [x.astype(BF16) for x in v_new]
        o = [_dot(qg_ref[bb, h, rows, :], x) + _dot(at_ref[bb, h, rows, :], y)
             for (bb, h), x, y in zip(lanes, sb, vb)]
        s = [gl_ref[bb, h, ci] * x
             + lax.dot_general(kd_ref[bb, h, rows, :], y, (((0,), (0,)), ((), ())), preferred_element_type=F32)
             for (bb, h), x, y in zip(lanes, s, vb)]
        for (bb, h), oh in zip(lanes, o):
            cols = slice(h * HEAD, (h + 1) * HEAD)
            ms = jnp.mean(oh * oh, axis=-1, keepdims=True)
            y = oh * lax.rsqrt(ms + RMS_EPS) * gain
            o_ref[bb, rows, cols] = (y * _silu(z_ref[bb, rows, cols].astype(F32))).astype(o_ref.dtype)
    for i, x in enumerate(s):
        s_scr[i] = x


def _gdn(proj, ab, conv_w, a_log, dt_bias, gain, *, t_prep, t_scan, c):
    b, l, _ = proj.shape
    halo_blocks = t_prep // GD_HALO

    def col(seg):
        return pl.BlockSpec((1, t_prep, HEAD), lambda bi, h, t, seg=seg: (bi, t, seg * N_HEADS + h))

    def halo(seg):
        return pl.BlockSpec((1, GD_HALO, HEAD),
                            lambda bi, h, t, seg=seg: (bi, jnp.maximum(t * halo_blocks - 1, 0), seg * N_HEADS + h))

    def wcol(seg):
        return pl.BlockSpec((GD_CONV, HEAD), lambda bi, h, t, seg=seg: (0, seg * N_HEADS + h))

    one = pl.BlockSpec((1, HEAD), lambda bi, h, t: (0, 0))

    def per_head(width):
        return pl.BlockSpec((1, 1, t_prep, width), lambda bi, h, t: (bi, h, t, 0))

    n_prep = t_prep // c
    u, w, qg, kd, at, gl = pl.pallas_call(
        functools.partial(_gdn_prep_kernel, n_chunks=n_prep, c=c), grid=(b, N_HEADS, l // t_prep),
        in_specs=[col(0), col(1), col(2), halo(0), halo(1), halo(2),
                  pl.BlockSpec((1, t_prep, HEAD), lambda bi, h, t: (bi, t, 0)),
                  wcol(0), wcol(1), wcol(2), one, one],
        out_specs=[per_head(HEAD), per_head(HEAD), per_head(HEAD), per_head(HEAD), per_head(c),
                   pl.BlockSpec((1, 1, n_prep, 1, HEAD), lambda bi, h, t: (bi, h, t, 0, 0))],
        out_shape=[jax.ShapeDtypeStruct((b, N_HEADS, l, HEAD), F32),
                   jax.ShapeDtypeStruct((b, N_HEADS, l, HEAD), BF16),
                   jax.ShapeDtypeStruct((b, N_HEADS, l, HEAD), BF16),
                   jax.ShapeDtypeStruct((b, N_HEADS, l, HEAD), BF16),
                   jax.ShapeDtypeStruct((b, N_HEADS, l, c), BF16),
                   jax.ShapeDtypeStruct((b, N_HEADS, l // c, 1, HEAD), F32)],
        scratch_shapes=[pltpu.VMEM((3, t_prep + GD_HALO, HEAD), F32)],
        compiler_params=_params("parallel", "parallel", "parallel"), name="gdn_prep",
    )(proj, proj, proj, proj, proj, proj, ab, conv_w, conv_w, conv_w, a_log, dt_bias)

    rows_b = GD_SCAN_BATCH_ROWS if b % GD_SCAN_BATCH_ROWS == 0 else 1

    def all_heads(width):
        return pl.BlockSpec((rows_b, N_HEADS, t_scan, width), lambda bi, t: (bi, 0, t, 0))

    n_scan = t_scan // c
    return pl.pallas_call(
        functools.partial(_gdn_scan_kernel, n_chunks=n_scan, c=c, rows_b=rows_b), grid=(b // rows_b, l // t_scan),
        in_specs=[all_heads(HEAD), all_heads(HEAD), all_heads(HEAD), all_heads(HEAD), all_heads(c),
                  pl.BlockSpec((rows_b, N_HEADS, n_scan, 1, HEAD), lambda bi, t: (bi, 0, t, 0, 0)),
                  pl.BlockSpec((rows_b, t_scan, BRANCH), lambda bi, t: (bi, t, 3)),
                  pl.BlockSpec((1, HEAD), lambda bi, t: (0, 0))],
        out_specs=pl.BlockSpec((rows_b, t_scan, BRANCH), lambda bi, t: (bi, t, 0)),
        out_shape=jax.ShapeDtypeStruct((b, l, BRANCH), BF16),
        scratch_shapes=[pltpu.VMEM((rows_b * N_HEADS, HEAD, HEAD), F32)],
        compiler_params=_params("parallel", "arbitrary"), name="gdn_scan",
    )(u, w, qg, kd, at, gl, proj, gain)


def _pad_lanes(vec):
    return jnp.zeros((1, HEAD), F32).at[0, :vec.shape[0]].set(vec.astype(F32))


def kernel(x, norm_pre, norm_post, ev_w_in, ev_w_out, hg_lb_logits, hg_norm, df_lambda, df_norm,
           od_w_in, od_w_out, gd_conv, gd_a_log, gd_dt_bias, gd_norm):
    p = dict(norm_pre=norm_pre, norm_post=norm_post, ev_w_in=ev_w_in, ev_w_out=ev_w_out,
             hg_lb_logits=hg_lb_logits, hg_norm=hg_norm, df_lambda=df_lambda, df_norm=df_norm,
             od_w_in=od_w_in, od_w_out=od_w_out, gd_conv=gd_conv, gd_a_log=gd_a_log,
             gd_dt_bias=gd_dt_bias, gd_norm=gd_norm)
    return _run_layers(x, p, range(norm_pre.shape[0]))


def _run_layers(x, p, layers):
    norm_pre, norm_post = p["norm_pre"], p["norm_post"]
    ev_w_in, ev_w_out, od_w_in, od_w_out = p["ev_w_in"], p["ev_w_out"], p["od_w_in"], p["od_w_out"]
    hg_lb_logits, hg_norm, df_lambda, df_norm = p["hg_lb_logits"], p["hg_norm"], p["df_lambda"], p["df_norm"]
    gd_conv, gd_a_log, gd_dt_bias, gd_norm = p["gd_conv"], p["gd_a_log"], p["gd_dt_bias"], p["gd_norm"]
    b, l, d = x.shape
    m = b * l
    tm_in = min(1024, m)
    tm_out = min(512, m)
    t_rec = min(REC_BLOCK, l)
    t_att = min(ATT_BLOCK, l)
    x2 = x.reshape(m, d)
    for layer in layers:
        j = layer // 2
        g_pre = norm_pre[layer].reshape(1, d)
        g_post = norm_post[layer].reshape(1, d)
        if layer % 2 == 0:
            proj = _inproj(x2, g_pre, ev_w_in[j].astype(BF16), tm=tm_in, tn=IN_TILE_N).reshape(b, l, -1)
            lam_init = 0.8 - 0.6 * math.exp(-0.3 * layer)
            mix_a = _hgrn(proj, hg_lb_logits.astype(F32), hg_norm[j].reshape(1, HEAD), j, c=min(HG_CHUNK, l))
            mix_b = _diff_attn(proj, df_lambda[j].astype(F32), df_norm[j].reshape(1, HEAD), lam_init,
                               tq=min(DIFF_Q_BLOCK, l // DIFF_BLOCKS_PER_STEP))
            w_out = ev_w_out[j].astype(BF16)
        else:
            w = od_w_in[j]
            w_main = jnp.concatenate([w[:, :4 * BRANCH], w[:, 4 * BRANCH + 2 * N_HEADS:]], axis=1).astype(BF16)
            w_ab = jnp.pad(w[:, 4 * BRANCH:4 * BRANCH + 2 * N_HEADS], ((0, 0), (0, HEAD - 2 * N_HEADS))).astype(BF16)
            proj, ab = _inproj(x2, g_pre, w_main, w_ab, tm=tm_in, tn=IN_TILE_N)
            proj = proj.reshape(b, l, -1)
            mix_a = _gdn(proj, ab.reshape(b, l, HEAD), gd_conv[j].astype(F32), _pad_lanes(gd_a_log[j]),
                         _pad_lanes(gd_dt_bias[j]), gd_norm[j].reshape(1, HEAD), t_prep=min(GD_PREP_BLOCK, l), t_scan=t_rec, c=CHUNK)
            mix_b = _stick_breaking(proj, tq=t_att)
            w_out = od_w_out[j].astype(BF16)
        x2 = _outproj(mix_a.reshape(m, BRANCH), mix_b.reshape(m, BRANCH), w_out[:BRANCH], w_out[BRANCH:],
                      g_post, x2, tm=tm_out)
    return x2.reshape(b, l, d)
```

```python
import functools
import math

import numpy as np
import jax
import jax.numpy as jnp
from jax import lax
from jax.experimental import pallas as pl
from jax.experimental.pallas import tpu as pltpu

F32 = jnp.float32
BF16 = jnp.bfloat16
HIGHEST = lax.Precision.HIGHEST

D_MODEL = 2048
BRANCH = D_MODEL // 2
N_HEADS = 8
HEAD = BRANCH // N_HEADS
DF_DH = HEAD // 2
GD_CONV = 4
RMS_EPS = 1e-6
NEG = -1e30
GATE_FLOOR = 1e-20
SB_SKIP = 104.0

V7X_VMEM_LIMIT = 56 * 1024 * 1024

IN_TILE_N = 2048
CHUNK = 64
HG_CHUNK = 256
HG_HEADS_PER_STEP = 8
HG_STAGGER = 5
REC_BLOCK = 256
ATT_BLOCK = 256
SB_HEADS_PER_STEP = 4
DIFF_Q_BLOCK = 512
DIFF_BLOCKS_PER_STEP = 4
DIFF_KEY_PARTS = 2
INV_BLOCK = 16
GD_PREP_BLOCK = 2048
GD_HALO = 16
GD_SCAN_BATCH_ROWS = 2
GD_GROUPS = 4
GD_STAGGER = 2


def _params(*sem):
    return pltpu.CompilerParams(dimension_semantics=sem, vmem_limit_bytes=V7X_VMEM_LIMIT)


def _sigmoid(x):
    return 0.5 * jnp.tanh(0.5 * x) + 0.5


def _silu(x):
    return x * _sigmoid(x)


def _softplus(x):
    return jnp.maximum(x, 0.0) + jnp.log(1.0 + jnp.exp(-jnp.abs(x)))


def _dot(a, b):
    return jnp.dot(a, b, preferred_element_type=F32)


def _dot_nt(a, b):
    return lax.dot_general(a, b, (((1,), (1,)), ((), ())), preferred_element_type=F32)


def _dot_exact(a, b):
    return jnp.dot(a, b, precision=HIGHEST, preferred_element_type=F32)


def _inproj_kernel(x_ref, g_ref, w_ref, o_ref, h_scr):
    @pl.when(pl.program_id(1) == 0)
    def _():
        x = x_ref[...]
        ms = jnp.mean(x * x, axis=-1, keepdims=True)
        h_scr[...] = (x * lax.rsqrt(ms + RMS_EPS) * g_ref[...]).astype(BF16)

    o_ref[...] = _dot(h_scr[...], w_ref[...]).astype(o_ref.dtype)


def _inproj_ab_kernel(x_ref, g_ref, w_ref, wab_ref, o_ref, oab_ref, h_scr):
    @pl.when(pl.program_id(1) == 0)
    def _():
        x = x_ref[...]
        ms = jnp.mean(x * x, axis=-1, keepdims=True)
        h_scr[...] = (x * lax.rsqrt(ms + RMS_EPS) * g_ref[...]).astype(BF16)
        oab_ref[...] = _dot(h_scr[...], wab_ref[...])

    o_ref[...] = _dot(h_scr[...], w_ref[...]).astype(o_ref.dtype)


def _proj_kernel(h_ref, w_ref, o_ref):
    o_ref[...] = _dot(h_ref[...], w_ref[...]).astype(o_ref.dtype)


def _proj_ab_kernel(h_ref, w_ref, wab_ref, o_ref, oab_ref):
    @pl.when(pl.program_id(1) == 0)
    def _():
        oab_ref[...] = _dot(h_ref[...], wab_ref[...])

    o_ref[...] = _dot(h_ref[...], w_ref[...]).astype(o_ref.dtype)


def _proj(h2, w, w_ab=None, *, tm, tn):
    m, d = h2.shape
    n = w.shape[1]
    grid = (m // tm, n // tn)
    h_spec = pl.BlockSpec((tm, d), lambda i, j: (i, 0))
    w_spec = pl.BlockSpec((d, tn), lambda i, j: (0, j))
    o_spec = pl.BlockSpec((tm, tn), lambda i, j: (i, j))
    if w_ab is None:
        return pl.pallas_call(
            _proj_kernel, grid=grid, in_specs=[h_spec, w_spec], out_specs=o_spec,
            out_shape=jax.ShapeDtypeStruct((m, n), BF16),
            compiler_params=_params("parallel", "arbitrary"), name="proj",
        )(h2, w)
    nab = w_ab.shape[1]
    return pl.pallas_call(
        _proj_ab_kernel, grid=grid,
        in_specs=[h_spec, w_spec, pl.BlockSpec((d, nab), lambda i, j: (0, 0))],
        out_specs=[o_spec, pl.BlockSpec((tm, nab), lambda i, j: (i, 0))],
        out_shape=[jax.ShapeDtypeStruct((m, n), BF16), jax.ShapeDtypeStruct((m, nab), F32)],
        compiler_params=_params("parallel", "arbitrary"), name="proj_ab",
    )(h2, w, w_ab)


def _inproj(x2, gain, w, w_ab=None, *, tm, tn):
    if x2.dtype == BF16:
        return _proj(x2, w, w_ab, tm=tm, tn=tn)
    m, d = x2.shape
    n = w.shape[1]
    grid = (m // tm, n // tn)
    x_spec = pl.BlockSpec((tm, d), lambda i, j: (i, 0))
    g_spec = pl.BlockSpec((1, d), lambda i, j: (0, 0))
    w_spec = pl.BlockSpec((d, tn), lambda i, j: (0, j))
    o_spec = pl.BlockSpec((tm, tn), lambda i, j: (i, j))
    scratch = [pltpu.VMEM((tm, d), BF16)]
    if w_ab is None:
        return pl.pallas_call(
            _inproj_kernel, grid=grid, in_specs=[x_spec, g_spec, w_spec], out_specs=o_spec,
            out_shape=jax.ShapeDtypeStruct((m, n), BF16), scratch_shapes=scratch,
            compiler_params=_params("parallel", "arbitrary"), name="inproj",
        )(x2, gain, w)
    nab = w_ab.shape[1]
    return pl.pallas_call(
        _inproj_ab_kernel, grid=grid,
        in_specs=[x_spec, g_spec, w_spec, pl.BlockSpec((d, nab), lambda i, j: (0, 0))],
        out_specs=[o_spec, pl.BlockSpec((tm, nab), lambda i, j: (i, 0))],
        out_shape=[jax.ShapeDtypeStruct((m, n), BF16), jax.ShapeDtypeStruct((m, nab), F32)],
        scratch_shapes=scratch, compiler_params=_params("parallel", "arbitrary"), name="inproj_ab",
    )(x2, gain, w, w_ab)


def _outproj_kernel(ma_ref, mb_ref, wa_ref, wb_ref, g_ref, x_ref, o_ref):
    y = _dot(ma_ref[...], wa_ref[...]) + _dot(mb_ref[...], wb_ref[...])
    ms = jnp.mean(y * y, axis=-1, keepdims=True)
    o_ref[...] = x_ref[...] + y * lax.rsqrt(ms + RMS_EPS) * g_ref[...]


def _outproj_next_kernel(ma_ref, mb_ref, wa_ref, wb_ref, g_ref, x_ref, gn_ref, o_ref, hn_ref):
    y = _dot(ma_ref[...], wa_ref[...]) + _dot(mb_ref[...], wb_ref[...])
    ms = jnp.mean(y * y, axis=-1, keepdims=True)
    x_new = x_ref[...] + y * lax.rsqrt(ms + RMS_EPS) * g_ref[...]
    o_ref[...] = x_new
    ms_new = jnp.mean(x_new * x_new, axis=-1, keepdims=True)
    hn_ref[...] = (x_new * lax.rsqrt(ms_new + RMS_EPS) * gn_ref[...]).astype(BF16)


def _outproj(mix_a, mix_b, w_a, w_b, gain, x2, gain_next=None, *, tm):
    m, d = x2.shape
    k = mix_a.shape[1]
    if gain_next is not None:
        row = pl.BlockSpec((tm, d), lambda i: (i, 0))
        one = pl.BlockSpec((1, d), lambda i: (0, 0))
        return pl.pallas_call(
            _outproj_next_kernel, grid=(m // tm,),
            in_specs=[pl.BlockSpec((tm, k), lambda i: (i, 0)), pl.BlockSpec((tm, k), lambda i: (i, 0)),
                      pl.BlockSpec((k, d), lambda i: (0, 0)), pl.BlockSpec((k, d), lambda i: (0, 0)),
                      one, row, one],
            out_specs=[row, row],
            out_shape=[jax.ShapeDtypeStruct((m, d), F32), jax.ShapeDtypeStruct((m, d), BF16)],
            compiler_params=_params("parallel"), name="outproj_next",
        )(mix_a, mix_b, w_a, w_b, gain, x2, gain_next)
    return pl.pallas_call(
        _outproj_kernel, grid=(m // tm,),
        in_specs=[pl.BlockSpec((tm, k), lambda i: (i, 0)), pl.BlockSpec((tm, k), lambda i: (i, 0)),
                  pl.BlockSpec((k, d), lambda i: (0, 0)), pl.BlockSpec((k, d), lambda i: (0, 0)),
                  pl.BlockSpec((1, d), lambda i: (0, 0)), pl.BlockSpec((tm, d), lambda i: (i, 0))],
        out_specs=pl.BlockSpec((tm, d), lambda i: (i, 0)),
        out_shape=jax.ShapeDtypeStruct((m, d), F32),
        compiler_params=_params("parallel"), name="outproj",
    )(mix_a, mix_b, w_a, w_b, gain, x2)


def _hgrn_constants(c):
    n = c // 2
    t = np.arange(n)
    masks = [np.eye(n, dtype=np.float32)]
    b = 1
    while b < n:
        mid = (t // (2 * b)) * (2 * b) + b
        same = (t[:, None] // (2 * b)) == (t[None, :] // (2 * b))
        masks.append((same & (t[:, None] >= mid[:, None]) & (t[None, :] < mid[:, None])).astype(np.float32))
        b *= 2
    rows = np.arange(c)
    tri = (rows[None, :] <= rows[:, None]).astype(np.float32)
    return tri, np.stack(masks, axis=0)


HG_PAD = 8


def _hgrn_kernel(q_ref, f_ref, i_ref, z_ref, lbl_ref, gain_ref, tri_ref, mask_ref, o_ref, *scratch,
                 layer_j, c, heads):
    @pl.when(pl.program_id(2) == 0)
    def _():
        for hd in range(heads):
            scratch[3 * hd][...] = jnp.zeros((HEAD, HEAD), F32)

    _run_staggered([_hgrn_head(slice(hd * HEAD, (hd + 1) * HEAD), q_ref, f_ref, i_ref, z_ref, lbl_ref, gain_ref,
                               tri_ref, mask_ref, o_ref, *scratch[3 * hd:3 * hd + 3], layer_j=layer_j, c=c)
                    for hd in range(heads)], HG_STAGGER)


def _hgrn_head(cols, q_ref, f_ref, i_ref, z_ref, lbl_ref, gain_ref, tri_ref, mask_ref, o_ref,
               st_scr, a_scr, sh_scr, *, layer_j, c):
    n = c // 2

    logits = lbl_ref[:, cols]
    e = jnp.exp(logits - jnp.max(logits, axis=0, keepdims=True))
    p = e / jnp.sum(e, axis=0, keepdims=True)
    lb = jnp.sum(p[0:layer_j + 1], axis=0, keepdims=True) - p[0:1]

    q = _silu(q_ref[0, :, cols].astype(F32))
    sig = _sigmoid(f_ref[0, :, cols].astype(F32))
    g = jnp.log(jnp.maximum(lb + (1.0 - lb) * sig, GATE_FLOOR))
    k = (1.0 - lb) * (1.0 - sig)
    vb = i_ref[0, :, cols]
    qb = q.astype(BF16)
    kb = k.astype(BF16)
    yield

    a = _dot_const3(tri_ref[...], g)
    zeros_pad = jnp.zeros((HG_PAD, HEAD), F32)
    a_scr[0:HG_PAD, :] = zeros_pad
    a_scr[HG_PAD:HG_PAD + c, :] = a
    yield

    def seg_rows(b, first):
        return jnp.concatenate([jnp.broadcast_to(a_scr[pl.ds(first + j * b, 1), :], (b, HEAD))
                                for j in range(c // b)], axis=0)

    def shifted(x, slot):
        sh_scr[slot, 0:HG_PAD, :] = zeros_pad
        sh_scr[slot, HG_PAD + c:HG_PAD + c + HG_PAD, :] = zeros_pad
        sh_scr[slot, HG_PAD:HG_PAD + c, :] = x
        return lambda d: sh_scr[slot, pl.ds(HG_PAD + d, c), :]

    row = lax.broadcasted_iota(jnp.int32, (c, HEAD), 0)
    g_at = shifted(g, 0)
    odd = (row & 1) == 1
    up = {1: g, 2: g + jnp.where(odd, g_at(-1), 0.0)}
    lo = {1: None, 2: jnp.where(odd, 0.0, g_at(1))}
    up2_at = shifted(up[2], 1)
    suffix2_at = shifted(lo[2] + g, 2)
    pos = row & 3
    up[4] = up[2] + jnp.where(pos == 2, up2_at(-1), jnp.where(pos == 3, up2_at(-2), 0.0))
    lo[4] = lo[2] + jnp.where(pos == 0, suffix2_at(2), jnp.where(pos == 1, suffix2_at(1), 0.0))
    b = 8
    while b <= c:
        up[b] = a - seg_rows(b, HG_PAD - 1)
        lo[b] = seg_rows(b, HG_PAD - 1 + b) - a
        b *= 2
    yield

    def decayed(b):
        qd = (q * jnp.exp(up[b])).astype(BF16)
        kd = kb if lo[b] is None else (k * jnp.exp(lo[b])).astype(BF16)
        return qd, kd

    halves = (slice(0, n), slice(n, c))
    scores = [mask_ref[0] * _dot_nt(qb[h], kb[h]) for h in halves]
    b, lv = 1, 1
    while b < n:
        qd, kd = decayed(b)
        scores = [s + mask_ref[lv] * _dot_nt(qd[h], kd[h]) for s, h in zip(scores, halves)]
        b, lv = 2 * b, lv + 1
        yield
    qd, kd = decayed(n)
    cross = _dot_nt(qd[halves[1]], kd[halves[0]])
    o_top = _dot(scores[0].astype(BF16), vb[halves[0]])
    o_bot = _dot(jnp.concatenate([cross, scores[1]], axis=1).astype(BF16), vb)
    yield
    qd, kd = decayed(c)
    st = st_scr[...]
    o = jnp.concatenate([o_top, o_bot], axis=0) + _dot_nt(qd, st.astype(BF16))
    st_scr[...] = jnp.exp(a[c - 1:c, :]) * st + lax.dot_general(vb, kd, (((0,), (0,)), ((), ())),
                                                               preferred_element_type=F32)

    ms = jnp.mean(o * o, axis=-1, keepdims=True)
    y = o * lax.rsqrt(ms + RMS_EPS) * gain_ref[...]
    o_ref[0, :, cols] = (y * _silu(z_ref[0, :, cols].astype(F32))).astype(o_ref.dtype)
    yield


def _hgrn(proj, lb_logits, gain, layer_j, *, c):
    b, l, _ = proj.shape
    tri, masks = _hgrn_constants(c)
    n_even = lb_logits.shape[0]
    heads = HG_HEADS_PER_STEP
    groups = N_HEADS // heads
    width = heads * HEAD

    def col(seg):
        return pl.BlockSpec((1, c, width), lambda bi, h, t, seg=seg: (bi, t, seg * groups + h))

    kern = functools.partial(_hgrn_kernel, layer_j=layer_j, c=c, heads=heads)
    per_head_scratch = [pltpu.VMEM((HEAD, HEAD), F32), pltpu.VMEM((HG_PAD + c, HEAD), F32),
                        pltpu.VMEM((3, c + 2 * HG_PAD, HEAD), F32)]
    return pl.pallas_call(
        kern, grid=(b, groups, l // c),
        in_specs=[col(0), col(1), col(2), col(3),
                  pl.BlockSpec((n_even, width), lambda bi, h, t: (0, h)),
                  pl.BlockSpec((1, HEAD), lambda bi, h, t: (0, 0)),
                  pl.BlockSpec(tri.shape, lambda bi, h, t: (0, 0)),
                  pl.BlockSpec(masks.shape, lambda bi, h, t: (0, 0, 0))],
        out_specs=pl.BlockSpec((1, c, width), lambda bi, h, t: (bi, t, h)),
        out_shape=jax.ShapeDtypeStruct((b, l, BRANCH), BF16),
        scratch_shapes=per_head_scratch * heads,
        compiler_params=_params("parallel", "parallel", "arbitrary"), name="hgrn2",
    )(proj, proj, proj, proj, lb_logits, gain, jnp.asarray(tri, BF16), jnp.asarray(masks))


def _diff_kernel(q_ref, k_ref, v_ref, z_ref, lamv_ref, gain_ref, causal_ref, o_ref, vt_scr, sa_scr, sb_scr,
                 *, lam_init, tq, tk):
    qi = pl.program_id(2)
    n_trips = k_ref.shape[1] // tk

    @pl.when(qi == 0)
    def _():
        def transpose_block(i, carry):
            rows = pl.ds(pl.multiple_of(i * tk, tk), tk)
            vt_scr[i] = v_ref[0, rows, :].astype(F32).T.astype(BF16)
            return carry
        lax.fori_loop(0, n_trips, transpose_block, 0)

    lv = lamv_ref[...]
    d01 = jnp.sum(jnp.sum(lv[0:1] * lv[1:2], axis=-1, keepdims=True), axis=0, keepdims=True)
    d23 = jnp.sum(jnp.sum(lv[2:3] * lv[3:4], axis=-1, keepdims=True), axis=0, keepdims=True)
    lam = jnp.exp(d01) - jnp.exp(d23) + lam_init

    def half_heads(sub):
        q = q_ref[0, sub * tq:(sub + 1) * tq, :].astype(F32) * (DF_DH ** -0.5 * math.log2(math.e))
        lane = lax.broadcasted_iota(jnp.int32, q.shape, 1)
        zero = jnp.zeros_like(q)
        return (jnp.where(lane < DF_DH, q, zero).astype(BF16), jnp.where(lane >= DF_DH, q, zero).astype(BF16))

    parts = DIFF_KEY_PARTS
    rows_per_part = tk // parts

    def scores_part(t, s_buf, hv, hf, kp, diagonal=False):
        start = pl.multiple_of(t * tk + kp * rows_per_part, rows_per_part)
        s = _dot_nt(k_ref[0, pl.ds(start, rows_per_part), :], hv[hf])
        if diagonal:
            s = s + causal_ref[kp * rows_per_part:(kp + 1) * rows_per_part, :]
        s_buf[hf, kp * rows_per_part:(kp + 1) * rows_per_part, :] = s
        return jnp.max(s, axis=0, keepdims=True)

    def rescale(st, top):
        m, l, acc = st
        m_new = jnp.maximum(m, top)
        alpha = jnp.exp2(m - m_new)
        return m_new, alpha * l, alpha * acc

    def absorb_part(t, s_buf, hf, kp, st):
        m, l, acc = st
        lanes = slice(kp * rows_per_part, (kp + 1) * rows_per_part)
        p = jnp.exp2(s_buf[hf, lanes, :] - m)
        return m, l + jnp.sum(p, axis=0, keepdims=True), acc + _dot(vt_scr[t, :, lanes], p.astype(BF16))

    def scores(t, s_buf, hv, diagonal=False):
        return tuple(functools.reduce(jnp.maximum,
                                      [scores_part(t, s_buf, hv, hf, kp, diagonal) for kp in range(parts)])
                     for hf in range(2))

    def absorb(t, s_buf, state, tops):
        out = []
        for hf in range(2):
            st = rescale(state[hf], tops[hf])
            for kp in range(parts):
                st = absorb_part(t, s_buf, hf, kp, st)
            out.append(st)
        return tuple(out)

    def step(t_next, buf_next, hv_next, t_cur, buf_cur, state, tops_cur, diagonal_next=False):
        tops_next, out = [], []
        for hf in range(2):
            st = rescale(state[hf], tops_cur[hf])
            top = None
            for kp in range(parts):
                piece = scores_part(t_next, buf_next, hv_next, hf, kp, diagonal_next)
                top = piece if top is None else jnp.maximum(top, piece)
                st = absorb_part(t_cur, buf_cur, hf, kp, st)
            tops_next.append(top)
            out.append(st)
        return tuple(out), tuple(tops_next)

    def visits(v, n_full, hv, state, tops_cur, n_pairs, cur, oth):
        for _ in range(n_pairs):
            state, tops_oth = step(n_full - (v + 1), oth, hv, n_full - v, cur, state, tops_cur)
            state, tops_cur = step(n_full - (v + 2), cur, hv, n_full - (v + 1), oth, state, tops_oth)
            v = v + 2
        return state, tops_cur

    def loops(n_full, hv, state, tops, cur, oth):
        n_quads = n_full // 4
        state, tops = lax.fori_loop(
            0, n_quads, lambda i, c: visits(4 * i, n_full, hv, c[0], c[1], 2, cur, oth), (state, tops))
        n_pairs = (n_full - 4 * n_quads) // 2
        return lax.fori_loop(
            0, n_pairs, lambda i, c: visits(4 * n_quads + 2 * i, n_full, hv, c[0], c[1], 1, cur, oth),
            (state, tops))

    def finish(state, sub):
        (_, l1, acc1), (_, l2, acc2) = state
        rows = slice(sub * tq, (sub + 1) * tq)
        o = (acc1 / l1 - lam * (acc2 / l2)).T
        ms = jnp.mean(o * o, axis=-1, keepdims=True)
        y = o * lax.rsqrt(ms + RMS_EPS) * gain_ref[...]
        o_ref[0, rows, :] = (y * (1.0 - lam_init) * _silu(z_ref[0, rows, :].astype(F32))).astype(o_ref.dtype)

    def fresh():
        return tuple((jnp.full((1, tq), NEG, F32), jnp.zeros((1, tq), F32), jnp.zeros((HEAD, tq), F32))
                     for _ in range(2))

    n_blocks = DIFF_BLOCKS_PER_STEP
    hv = [half_heads(sub) for sub in range(n_blocks)]
    cur, oth = sa_scr, sb_scr
    tops = scores(n_blocks * qi, cur, hv[0], diagonal=True)
    for sub in range(n_blocks):
        n_full = n_blocks * qi + sub
        state, tops = loops(n_full, hv[sub], fresh(), tops, cur, oth)
        if sub % 2 == 1:
            state, tops = step(0, oth, hv[sub], 1, cur, state, tops)
            cur, oth = oth, cur
        if sub + 1 < n_blocks:
            state, tops = step(n_full + 1, oth, hv[sub + 1], 0, cur, state, tops, diagonal_next=True)
            cur, oth = oth, cur
        else:
            state = absorb(0, cur, state, tops)
        finish(state, sub)


def _diff_attn(proj, lam_vec, gain, lam_init, *, tq):
    b, l, _ = proj.shape

    def qcol(seg):
        return pl.BlockSpec((1, DIFF_BLOCKS_PER_STEP * tq, HEAD), lambda bi, h, t, seg=seg: (bi, t, seg * N_HEADS + h))

    def kvcol(seg):
        return pl.BlockSpec((1, l, HEAD), lambda bi, h, t, seg=seg: (bi, 0, seg * N_HEADS + h))

    tk = tq
    key = np.arange(tk)[:, None]
    causal = np.where(key <= np.arange(tq)[None, :], 0.0, NEG).astype(np.float32)
    kern = functools.partial(_diff_kernel, lam_init=lam_init, tq=tq, tk=tk)
    return pl.pallas_call(
        kern, grid=(b, N_HEADS, l // (DIFF_BLOCKS_PER_STEP * tq)),
        in_specs=[qcol(4), kvcol(5), kvcol(6), qcol(7),
                  pl.BlockSpec(lam_vec.shape, lambda bi, h, t: (0, 0)),
                  pl.BlockSpec((1, HEAD), lambda bi, h, t: (0, 0)),
                  pl.BlockSpec((tk, tq), lambda bi, h, t: (0, 0))],
        out_specs=pl.BlockSpec((1, DIFF_BLOCKS_PER_STEP * tq, HEAD), lambda bi, h, t: (bi, t, h)),
        out_shape=jax.ShapeDtypeStruct((b, l, BRANCH), BF16),
        scratch_shapes=[pltpu.VMEM((l // tk, HEAD, tk), BF16), pltpu.VMEM((2, tk, tq), F32), pltpu.VMEM((2, tk, tq), F32)],
        compiler_params=_params("parallel", "parallel", "arbitrary"), name="diff_attn",
    )(proj, proj, proj, proj, lam_vec, gain, jnp.asarray(causal))


def _sb_kernel(q_ref, k_ref, v_ref, z_ref, o_ref, *, tq, heads):
    qi = pl.program_id(2)
    cols = [slice(hd * HEAD, (hd + 1) * HEAD) for hd in range(heads)]
    q = [(q_ref[0, :, c].astype(F32) * (HEAD ** -0.5)).astype(BF16) for c in cols]
    r = lax.broadcasted_iota(jnp.int32, (tq, tq), 0)
    cc = lax.broadcasted_iota(jnp.int32, (tq, tq), 1)
    later = (r > cc).astype(BF16)

    def local(kj, masked):
        rows = pl.ds(pl.multiple_of(kj * tq, tq), tq)
        zz = [_dot_nt(qh, k_ref[0, rows, c]) for qh, c in zip(q, cols)]
        sp = [_softplus(x) for x in zz]
        log_rest = [-x for x in sp]
        if masked:
            log_rest = [jnp.where(cc < r, x, 0.0) for x in log_rest]
        hi = [x.astype(BF16) for x in log_rest]
        lo = [(x - h.astype(F32)).astype(BF16) for x, h in zip(log_rest, hi)]
        inner = [z - s + (_dot(h, later) + _dot(lw, later)) for z, s, h, lw in zip(zz, sp, hi, lo)]
        return inner, [jnp.sum(x, axis=-1, keepdims=True) for x in log_rest], rows

    def weigh(part, run, masked):
        inner, total, rows = part
        w = [jnp.exp(x + rn) for x, rn in zip(inner, run)]
        if masked:
            w = [jnp.where(cc < r, x, 0.0) for x in w]
        pv = [_dot(x.astype(BF16), v_ref[0, rows, c]) for x, c in zip(w, cols)]
        return pv, [rn + t for rn, t in zip(run, total)]

    def alive(run):
        top = run[0]
        for rn in run[1:]:
            top = jnp.maximum(top, rn)
        return jnp.max(top) > -SB_SKIP

    diag = local(qi, True)
    prev = local(jnp.maximum(qi - 1, 0), False)
    acc, run = weigh(diag, [jnp.zeros((tq, 1), F32)] * heads, True)
    has_prev = (qi > 0).astype(F32)
    pv, run_prev = weigh(prev, run, False)
    acc = [a + has_prev * p for a, p in zip(acc, pv)]
    run = [rn + has_prev * (rp - rn) for rn, rp in zip(run, run_prev)]

    def cond(carry):
        kj, go, _, _ = carry
        return jnp.logical_and(kj >= 0, go)

    def body(carry):
        kj, _, run, acc = carry
        pv, run = weigh(local(kj, False), run, False)
        return kj - 1, alive(run), run, [a + p for a, p in zip(acc, pv)]

    _, _, _, acc = lax.while_loop(cond, body, (qi - 2, alive(run), run, acc))
    for a, c in zip(acc, cols):
        o_ref[0, :, c] = (a * _silu(z_ref[0, :, c].astype(F32))).astype(o_ref.dtype)


def _stick_breaking(proj, *, tq):
    b, l, _ = proj.shape
    heads = SB_HEADS_PER_STEP
    groups = N_HEADS // heads
    width = heads * HEAD

    def qcol(seg):
        return pl.BlockSpec((1, tq, width), lambda bi, h, t, seg=seg: (bi, t, seg * groups + h))

    def kvcol(seg):
        return pl.BlockSpec((1, l, width), lambda bi, h, t, seg=seg: (bi, 0, seg * groups + h))

    return pl.pallas_call(
        functools.partial(_sb_kernel, tq=tq, heads=heads), grid=(b, groups, l // tq),
        in_specs=[qcol(4), kvcol(5), kvcol(6), qcol(7)],
        out_specs=pl.BlockSpec((1, tq, width), lambda bi, h, t: (bi, t, h)),
        out_shape=jax.ShapeDtypeStruct((b, l, BRANCH), BF16),
        compiler_params=_params("parallel", "parallel", "arbitrary"), name="stick_breaking",
    )(proj, proj, proj, proj)


def _split2(x):
    hi = x.astype(BF16)
    return hi, (x - hi.astype(F32)).astype(BF16)


def _dot3(a, b):
    a_hi, a_lo = _split2(a)
    b_hi, b_lo = _split2(b)
    return _dot(a_hi, b_hi) + (_dot(a_hi, b_lo) + _dot(a_lo, b_hi))


def _dot_const3(const_bf16, x):
    hi = x.astype(BF16)
    r1 = x - hi.astype(F32)
    mid = r1.astype(BF16)
    lo = (r1 - mid.astype(F32)).astype(BF16)
    return _dot(const_bf16, hi) + (_dot(const_bf16, mid) + _dot(const_bf16, lo))


def _each(fn, *lists):
    return [fn(*args) for args in zip(*lists)]


def _unit_lower_inverse_steps(lws, eye, blockdiag, out):
    ld = [lw * blockdiag for lw in lws]
    lo = _each(lambda a, b: a - b, lws, ld)
    td = [eye - x for x in ld]
    p = _each(_dot3, ld, ld)
    yield
    n_factors = int(math.log2(INV_BLOCK)) - 1
    for i in range(n_factors):
        td = _each(lambda a, b: a + _dot3(a, b), td, p)
        if i < n_factors - 1:
            p = _each(_dot3, p, p)
        yield
    n = _each(_dot3, td, lo)
    yield
    n2 = _each(_dot3, n, n)
    yield
    y = [eye - x for x in n]
    y = _each(lambda a, b: a + _dot3(a, b), y, n2)
    yield
    out.extend(_each(_dot3, y, td))
    yield


_DONE = object()


def _run_staggered(chains, lag):
    live = list(chains)
    for ahead, chain in enumerate(reversed(live)):
        for _ in range(ahead * lag):
            next(chain, None)
    while live:
        live = [chain for chain in live if next(chain, _DONE) is not _DONE]


def _gdn_prep_kernel(q_ref, k_ref, v_ref, qh_ref, kh_ref, vh_ref, ab_ref, wq_ref, wk_ref, wv_ref, alog_ref, dtb_ref,
                     u_ref, w_ref, qg_ref, kd_ref, at_ref, gl_ref, ext_scr, *, n_chunks, c):
    h = pl.program_id(1)
    t = pl.program_id(2)
    t_blk = n_chunks * c

    for idx, (x_ref, halo_ref) in enumerate(((q_ref, qh_ref), (k_ref, kh_ref), (v_ref, vh_ref))):
        ext_scr[idx, 0:GD_HALO, :] = jnp.where(t > 0, halo_ref[0].astype(F32), 0.0)
        ext_scr[idx, GD_HALO:GD_HALO + t_blk, :] = x_ref[0].astype(F32)

    def conv(w_ref, idx, row0, n_rows):
        w = w_ref[...]
        y = w[GD_CONV - 1:GD_CONV] * ext_scr[idx, pl.ds(GD_HALO + row0, n_rows), :]
        for tap in range(GD_CONV - 1):
            y = y + w[tap:tap + 1] * ext_scr[idx, pl.ds(GD_HALO - (GD_CONV - 1) + tap + row0, n_rows), :]
        return _silu(y)

    def l2n(x):
        return x * lax.rsqrt(jnp.sum(x * x, axis=-1, keepdims=True) + RMS_EPS)

    r = lax.broadcasted_iota(jnp.int32, (c, c), 0)
    cc = lax.broadcasted_iota(jnp.int32, (c, c), 1)
    incl = cc <= r
    strict = cc < r
    tri = incl.astype(BF16)
    eye = (cc == r).astype(F32)
    blockdiag = ((r // INV_BLOCK) == (cc // INV_BLOCK)).astype(F32)
    r2 = lax.broadcasted_iota(jnp.int32, (c, HEAD), 0)
    c2 = lax.broadcasted_iota(jnp.int32, (c, HEAD), 1)
    strict_wide = jnp.logical_and(c2 < r2, c2 < c).astype(F32)

    def chain(ids):
        row0, n_rows = ids[0] * c, len(ids) * c
        local = [slice(i * c, (i + 1) * c) for i in range(len(ids))]
        chunks = [slice(ci * c, (ci + 1) * c) for ci in ids]
        q_rows = l2n(conv(wq_ref, 0, row0, n_rows)) * (HEAD ** -0.5)
        yield
        k_rows = l2n(conv(wk_ref, 1, row0, n_rows))
        yield
        v_rows = conv(wv_ref, 2, row0, n_rows)
        ab = ab_ref[0, row0:row0 + n_rows, :]
        lane = lax.broadcasted_iota(jnp.int32, ab.shape, 1)
        g_lanes = -jnp.exp(alog_ref[...]) * _softplus(ab + dtb_ref[...])
        g_rows = jnp.sum(jnp.where(lane == h, g_lanes, 0.0), axis=-1, keepdims=True)
        beta_rows = jnp.sum(jnp.where(lane == h + N_HEADS, _sigmoid(ab), 0.0), axis=-1, keepdims=True)
        yield
        q = [q_rows[sl] for sl in local]
        k = [k_rows[sl] for sl in local]
        v = [v_rows[sl] for sl in local]
        beta = [beta_rows[sl] for sl in local]
        g_wide = [jnp.broadcast_to(g_rows[sl], (c, HEAD)) for sl in local]
        sums = [_dot_const3(tri, jnp.concatenate([gw, gw * strict_wide], axis=1)) for gw in g_wide]
        yield
        gc = [s[:, 0:HEAD] for s in sums]
        decay = [jnp.exp(jnp.where(incl, s[:, HEAD:HEAD + c], NEG)) for s in sums]
        kb = _each(lambda a, b: a * b, k, beta)
        kbf = [x.astype(BF16) for x in k]
        lw = _each(lambda a, b, d: jnp.where(strict, _dot_nt(a.astype(BF16), b) * d, 0.0), kb, kbf, decay)
        attn = _each(lambda a, b, d: _dot_nt(a.astype(BF16), b) * d, q, kbf, decay)
        yield
        inverse = []
        yield from _unit_lower_inverse_steps(lw, eye, blockdiag, inverse)
        tinv = [x.astype(BF16) for x in inverse]
        egc = [jnp.exp(x) for x in gc]
        u = _each(lambda ti, a, b: _dot(ti, (a * b).astype(BF16)), tinv, v, beta)
        w = _each(lambda ti, a, e: _dot(ti, (a * e).astype(BF16)), tinv, kb, egc)
        for i, (ci, sl) in enumerate(zip(ids, chunks)):
            gc_last = gc[i][c - 1:c, :]
            u_ref[0, 0, sl, :] = u[i]
            w_ref[0, 0, sl, :] = w[i].astype(BF16)
            qg_ref[0, 0, sl, :] = (q[i] * egc[i]).astype(BF16)
            kd_ref[0, 0, sl, :] = (k[i] * jnp.exp(gc_last - gc[i])).astype(BF16)
            at_ref[0, 0, sl, :] = attn[i].astype(BF16)
            gl_ref[0, 0, ci] = jnp.exp(gc_last)
        yield

    n_groups = min(GD_GROUPS, n_chunks)
    per_group = n_chunks // n_groups
    _run_staggered([chain(range(gi * per_group, (gi + 1) * per_group)) for gi in range(n_groups)], GD_STAGGER)


def _gdn_scan_kernel(u_ref, w_ref, qg_ref, kd_ref, at_ref, gl_ref, z_ref, gain_ref, o_ref, s_scr, *, n_chunks, c, rows_b):
    @pl.when(pl.program_id(1) == 0)
    def _():
        s_scr[...] = jnp.zeros_like(s_scr)

    lanes = [(bb, h) for bb in range(rows_b) for h in range(N_HEADS)]
    gain = gain_ref[...]
    s = [s_scr[bb * N_HEADS + h] for bb, h in lanes]
    for ci in range(n_chunks):
        rows = pl.ds(ci * c, c)
        sb = [x.astype(BF16) for x in s]
        v_new = [u_ref[bb, h, rows, :] - _dot(w_ref[bb, h, rows, :], x) for (bb, h), x in zip(lanes, sb)]
        vb = [x.astype(BF16) for x in v_new]
        o = [_dot(qg_ref[bb, h, rows, :], x) + _dot(at_ref[bb, h, rows, :], y)
             for (bb, h), x, y in zip(lanes, sb, vb)]
        s = [gl_ref[bb, h, ci] * x
             + lax.dot_general(kd_ref[bb, h, rows, :], y, (((0,), (0,)), ((), ())), preferred_element_type=F32)
             for (bb, h), x, y in zip(lanes, s, vb)]
        for (bb, h), oh in zip(lanes, o):
            cols = slice(h * HEAD, (h + 1) * HEAD)
            ms = jnp.mean(oh * oh, axis=-1, keepdims=True)
            y = oh * lax.rsqrt(ms + RMS_EPS) * gain
            o_ref[bb, rows, cols] = (y * _silu(z_ref[bb, rows, cols].astype(F32))).astype(o_ref.dtype)
    for i, x in enumerate(s):
        s_scr[i] = x


def _gdn(proj, ab, conv_w, a_log, dt_bias, gain, *, t_prep, t_scan, c):
    b, l, _ = proj.shape
    halo_blocks = t_prep // GD_HALO

    def col(seg):
        return pl.BlockSpec((1, t_prep, HEAD), lambda bi, h, t, seg=seg: (bi, t, seg * N_HEADS + h))

    def halo(seg):
        return pl.BlockSpec((1, GD_HALO, HEAD),
                            lambda bi, h, t, seg=seg: (bi, jnp.maximum(t * halo_blocks - 1, 0), seg * N_HEADS + h))

    def wcol(seg):
        return pl.BlockSpec((GD_CONV, HEAD), lambda bi, h, t, seg=seg: (0, seg * N_HEADS + h))

    one = pl.BlockSpec((1, HEAD), lambda bi, h, t: (0, 0))

    def per_head(width):
        return pl.BlockSpec((1, 1, t_prep, width), lambda bi, h, t: (bi, h, t, 0))

    n_prep = t_prep // c
    u, w, qg, kd, at, gl = pl.pallas_call(
        functools.partial(_gdn_prep_kernel, n_chunks=n_prep, c=c), grid=(b, N_HEADS, l // t_prep),
        in_specs=[col(0), col(1), col(2), halo(0), halo(1), halo(2),
                  pl.BlockSpec((1, t_prep, HEAD), lambda bi, h, t: (bi, t, 0)),
                  wcol(0), wcol(1), wcol(2), one, one],
        out_specs=[per_head(HEAD), per_head(HEAD), per_head(HEAD), per_head(HEAD), per_head(c),
                   pl.BlockSpec((1, 1, n_prep, 1, HEAD), lambda bi, h, t: (bi, h, t, 0, 0))],
        out_shape=[jax.ShapeDtypeStruct((b, N_HEADS, l, HEAD), F32),
                   jax.ShapeDtypeStruct((b, N_HEADS, l, HEAD), BF16),
                   jax.ShapeDtypeStruct((b, N_HEADS, l, HEAD), BF16),
                   jax.ShapeDtypeStruct((b, N_HEADS, l, HEAD), BF16),
                   jax.ShapeDtypeStruct((b, N_HEADS, l, c), BF16),
                   jax.ShapeDtypeStruct((b, N_HEADS, l // c, 1, HEAD), F32)],
        scratch_shapes=[pltpu.VMEM((3, t_prep + GD_HALO, HEAD), F32)],
        compiler_params=_params("parallel", "parallel", "parallel"), name="gdn_prep",
    )(proj, proj, proj, proj, proj, proj, ab, conv_w, conv_w, conv_w, a_log, dt_bias)

    rows_b = GD_SCAN_BATCH_ROWS if b % GD_SCAN_BATCH_ROWS == 0 else 1

    def all_heads(width):
        return pl.BlockSpec((rows_b, N_HEADS, t_scan, width), lambda bi, t: (bi, 0, t, 0))

    n_scan = t_scan // c
    return pl.pallas_call(
        functools.partial(_gdn_scan_kernel, n_chunks=n_scan, c=c, rows_b=rows_b), grid=(b // rows_b, l // t_scan),
        in_specs=[all_heads(HEAD), all_heads(HEAD), all_heads(HEAD), all_heads(HEAD), all_heads(c),
                  pl.BlockSpec((rows_b, N_HEADS, n_scan, 1, HEAD), lambda bi, t: (bi, 0, t, 0, 0)),
                  pl.BlockSpec((rows_b, t_scan, BRANCH), lambda bi, t: (bi, t, 3)),
                  pl.BlockSpec((1, HEAD), lambda bi, t: (0, 0))],
        out_specs=pl.BlockSpec((rows_b, t_scan, BRANCH), lambda bi, t: (bi, t, 0)),
        out_shape=jax.ShapeDtypeStruct((b, l, BRANCH), BF16),
        scratch_shapes=[pltpu.VMEM((rows_b * N_HEADS, HEAD, HEAD), F32)],
        compiler_params=_params("parallel", "arbitrary"), name="gdn_scan",
    )(u, w, qg, kd, at, gl, proj, gain)


def _pad_lanes(vec):
    return jnp.zeros((1, HEAD), F32).at[0, :vec.shape[0]].set(vec.astype(F32))


def kernel(x, norm_pre, norm_post, ev_w_in, ev_w_out, hg_lb_logits, hg_norm, df_lambda, df_norm,
           od_w_in, od_w_out, gd_conv, gd_a_log, gd_dt_bias, gd_norm):
    p = dict(norm_pre=norm_pre, norm_post=norm_post, ev_w_in=ev_w_in, ev_w_out=ev_w_out,
             hg_lb_logits=hg_lb_logits, hg_norm=hg_norm, df_lambda=df_lambda, df_norm=df_norm,
             od_w_in=od_w_in, od_w_out=od_w_out, gd_conv=gd_conv, gd_a_log=gd_a_log,
             gd_dt_bias=gd_dt_bias, gd_norm=gd_norm)
    return _run_layers(x, p, range(norm_pre.shape[0]))


def _run_layers(x, p, layers):
    norm_pre, norm_post = p["norm_pre"], p["norm_post"]
    ev_w_in, ev_w_out, od_w_in, od_w_out = p["ev_w_in"], p["ev_w_out"], p["od_w_in"], p["od_w_out"]
    hg_lb_logits, hg_norm, df_lambda, df_norm = p["hg_lb_logits"], p["hg_norm"], p["df_lambda"], p["df_norm"]
    gd_conv, gd_a_log, gd_dt_bias, gd_norm = p["gd_conv"], p["gd_a_log"], p["gd_dt_bias"], p["gd_norm"]
    b, l, d = x.shape
    m = b * l
    tm_in = min(1024, m)
    tm_out = min(512, m)
    t_rec = min(REC_BLOCK, l)
    t_att = min(ATT_BLOCK, l)
    x2 = x.reshape(m, d)
    layers = list(layers)
    h2 = None
    for pos, layer in enumerate(layers):
        j = layer // 2
        src = x2 if h2 is None else h2
        g_pre = norm_pre[layer].reshape(1, d)
        g_post = norm_post[layer].reshape(1, d)
        if layer % 2 == 0:
            proj = _inproj(src, g_pre, ev_w_in[j].astype(BF16), tm=tm_in, tn=IN_TILE_N).reshape(b, l, -1)
            lam_init = 0.8 - 0.6 * math.exp(-0.3 * layer)
            mix_a = _hgrn(proj, hg_lb_logits.astype(F32), hg_norm[j].reshape(1, HEAD), j, c=min(HG_CHUNK, l))
            mix_b = _diff_attn(proj, df_lambda[j].astype(F32), df_norm[j].reshape(1, HEAD), lam_init,
                               tq=min(DIFF_Q_BLOCK, l // DIFF_BLOCKS_PER_STEP))
            w_out = ev_w_out[j].astype(BF16)
        else:
            w = od_w_in[j]
            w_main = jnp.concatenate([w[:, :4 * BRANCH], w[:, 4 * BRANCH + 2 * N_HEADS:]], axis=1).astype(BF16)
            w_ab = jnp.pad(w[:, 4 * BRANCH:4 * BRANCH + 2 * N_HEADS], ((0, 0), (0, HEAD - 2 * N_HEADS))).astype(BF16)
            proj, ab = _inproj(src, g_pre, w_main, w_ab, tm=tm_in, tn=IN_TILE_N)
            proj = proj.reshape(b, l, -1)
            mix_a = _gdn(proj, ab.reshape(b, l, HEAD), gd_conv[j].astype(F32), _pad_lanes(gd_a_log[j]),
                         _pad_lanes(gd_dt_bias[j]), gd_norm[j].reshape(1, HEAD), t_prep=min(GD_PREP_BLOCK, l), t_scan=t_rec, c=CHUNK)
            mix_b = _stick_breaking(proj, tq=t_att)
            w_out = od_w_out[j].astype(BF16)
        mixes = (mix_a.reshape(m, BRANCH), mix_b.reshape(m, BRANCH), w_out[:BRANCH], w_out[BRANCH:], g_post, x2)
        if pos + 1 < len(layers):
            x2, h2 = _outproj(*mixes, norm_pre[layers[pos + 1]].reshape(1, d), tm=tm_out)
        else:
            x2 = _outproj(*mixes, tm=tm_out)
    return x2.reshape(b, l, d)
```
